```python
import jax, jax.numpy as jnp
from jax import lax
import numpy as np

D_MODEL = 2048
BATCH = 32
SEQ = 256
DEPTH = 4
DEC_BATCH = 8
DEC_SEQ = 1024
PAST_LEN = 256

GRID_W = 64
N_MIXERS = 2
N_ATTN_LAYERS = (DEPTH + 1) // 2
N_REC_LAYERS = DEPTH // 2
N_HEADS = 16
Q_LORA = 512
KV_LORA = 512
NOPE_DIM = 128
ROPE_DIM = 64
V_DIM = 128
ROPE_BASE = 10000.0
Q_BLOCK = 128
LRU_WIDTH = D_MODEL
N_LRU_BLOCKS = 16
LRU_BLOCK = LRU_WIDTH // N_LRU_BLOCKS
LRU_CONV_W = 4
LRU_CONV_PAD_LEFT = 2
LRU_C = 8.0
D_FF = 5632
FFN_CONV_W = 3
FFN_CONV_PAD_LEFT = 1
NORM_EPS = 1e-6

kernel_name = "hybrid_mla_rglru_convffn_diffusion_step"


def rmsnorm(x, g):
    xf = x.astype(jnp.float32)
    y = xf * lax.rsqrt(jnp.mean(xf * xf, axis=-1, keepdims=True) + NORM_EPS)
    return (y * g.astype(jnp.float32)).astype(x.dtype)


def adaln(cond, w, b):
    m = jax.nn.silu(cond) @ w + b
    return [t[:, None, :] for t in jnp.split(m, 6, axis=-1)]


def depthwise_conv(x, w, b, pad_left):
    k = w.shape[0]
    s = x.shape[1]
    xp = jnp.pad(x, ((0, 0), (pad_left, k - 1 - pad_left), (0, 0)))
    out = b
    for j in range(k):
        out = out + xp[:, j:j + s] * w[j]
    return out


def axial_rope_tables(n_tokens):
    rows = n_tokens // GRID_W
    row = jnp.repeat(jnp.arange(rows), GRID_W).astype(jnp.float32)
    col = jnp.tile(jnp.arange(GRID_W), rows).astype(jnp.float32)
    half = ROPE_DIM // 2
    inv = 1.0 / (ROPE_BASE ** (jnp.arange(0, half, 2, dtype=jnp.float32) / half))
    ang = jnp.concatenate([row[:, None] * inv, col[:, None] * inv], axis=-1)
    return jnp.cos(ang), jnp.sin(ang)


def apply_axial_rope(x, cos, sin):
    q = ROPE_DIM // 4
    xr1, xr2, xc1, xc2 = jnp.split(x.astype(jnp.float32), 4, axis=-1)
    cr, cc = cos[..., :q], cos[..., q:]
    sr, sc = sin[..., :q], sin[..., q:]
    y = jnp.concatenate([xr1 * cr - xr2 * sr, xr2 * cr + xr1 * sr,
                         xc1 * cc - xc2 * sc, xc2 * cc + xc1 * sc], axis=-1)
    return y.astype(x.dtype)


def mla_project(h, w_in, g_q, g_kv, w_uq):
    proj = h @ w_in
    c_q, c_kv, k_rope = jnp.split(proj, [Q_LORA, Q_LORA + KV_LORA], axis=-1)
    q = rmsnorm(c_q, g_q) @ w_uq
    q = q.reshape(q.shape[0], q.shape[1], N_HEADS, NOPE_DIM + ROPE_DIM)
    return q[..., :NOPE_DIM], q[..., NOPE_DIM:], rmsnorm(c_kv, g_kv), k_rope


def mla_expand(ckv, w_uk, w_uv):
    b, s = ckv.shape[:2]
    k_nope = (ckv @ w_uk).reshape(b, s, N_HEADS, NOPE_DIM)
    v = (ckv @ w_uv).reshape(b, s, N_HEADS, V_DIM)
    return k_nope, v


def mla_attention(q_nope, q_rope, k_nope, k_rope, v):
    b, sq = q_nope.shape[:2]
    nblk = sq // Q_BLOCK
    scale = (NOPE_DIM + ROPE_DIM) ** -0.5

    def blk(qs):
        qn, qr = qs
        s = (jnp.einsum('bqhd,bkhd->bhqk', qn, k_nope, preferred_element_type=jnp.float32)
             + jnp.einsum('bqhr,bkr->bhqk', qr, k_rope, preferred_element_type=jnp.float32))
        p = jax.nn.softmax(s * scale, axis=-1).astype(v.dtype)
        return jnp.einsum('bhqk,bkhd->bqhd', p, v)

    to_blocks = lambda t: jnp.moveaxis(t.reshape(b, nblk, Q_BLOCK, *t.shape[2:]), 1, 0)
    out = lax.map(blk, (to_blocks(q_nope), to_blocks(q_rope)))
    return jnp.moveaxis(out, 0, 1).reshape(b, sq, N_HEADS * V_DIM)


def block_diag(x, w, b):
    xb = x.reshape(x.shape[0], x.shape[1], N_LRU_BLOCKS, LRU_BLOCK)
    return jnp.einsum('bsni,nij->bsnj', xb, w).reshape(x.shape) + b


def rg_lru(x, lam, w_gx, b_gx, w_ga, b_ga, h0, reverse):
    gate_x = jax.nn.sigmoid(block_diag(x, w_gx, b_gx).astype(jnp.float32))
    gate_a = jax.nn.sigmoid(block_diag(x, w_ga, b_ga).astype(jnp.float32))
    log_a = -LRU_C * gate_a * jax.nn.softplus(-lam.astype(jnp.float32))
    a = jnp.exp(log_a)
    u = jnp.sqrt(-jnp.expm1(2.0 * log_a)) * (gate_x * x)

    def step(h, au):
        a_t, u_t = au
        h = a_t * h + u_t
        return h, h

    h_last, hs = lax.scan(step, h0.astype(jnp.float32),
                          (jnp.swapaxes(a, 0, 1), jnp.swapaxes(u, 0, 1)), reverse=reverse)
    return jnp.swapaxes(hs, 0, 1), h_last


def recurrent_mixer(h, w_in, w_conv, b_conv, w_gx, b_gx, w_ga, b_ga, lam, w_out, h0_f, h0_b):
    proj = h @ w_in
    y, xb = jnp.split(proj, 2, axis=-1)
    xb = depthwise_conv(xb, w_conv, b_conv, LRU_CONV_PAD_LEFT).astype(jnp.float32)
    hf, sf = rg_lru(xb, lam[0], w_gx[0], b_gx[0], w_ga[0], b_ga[0], h0_f, False)
    hb, sb = rg_lru(xb, lam[1], w_gx[1], b_gx[1], w_ga[1], b_ga[1], h0_b, True)
    mixed = ((hf + hb) * jax.nn.gelu(y.astype(jnp.float32))).astype(h.dtype)
    return mixed @ w_out, sf.astype(h.dtype), sb.astype(h.dtype)


def conv_ffn(h, w_up, w_conv, b_conv, w_down):
    u = depthwise_conv(h @ w_up, w_conv, b_conv, FFN_CONV_PAD_LEFT)
    g, v = jnp.split(u, 2, axis=-1)
    return (jax.nn.silu(g) * v) @ w_down


def setup_inputs(seed: int = 0) -> dict:
    key = jax.random.key(seed)
    ks = iter(jax.random.split(key, 40))
    D = D_MODEL

    def nrm(shape, s=1.0):
        return jax.random.normal(next(ks), shape, jnp.float32) * s

    def gain(shape):
        return 1.0 + nrm(shape, 0.02)

    inp = {}
    inp["x_prompt"] = nrm((BATCH, SEQ, D))
    inp["x_sample"] = nrm((DEC_BATCH, DEC_SEQ, D))
    inp["cache_ckv"] = nrm((DEC_BATCH, N_ATTN_LAYERS, PAST_LEN, KV_LORA))
    inp["cache_krope"] = nrm((DEC_BATCH, N_ATTN_LAYERS, PAST_LEN, ROPE_DIM))
    inp["state_lru"] = nrm((DEC_BATCH, N_REC_LAYERS, 2, LRU_WIDTH), 0.5)
    inp["c"] = nrm((DEC_BATCH, D))
    inp["c_ctx"] = nrm((D,))
    inp["g_mix"] = gain((DEPTH, D))
    inp["g_ffn"] = gain((DEPTH, D))
    inp["g_final"] = gain((D,))
    inp["w_ada"] = nrm((DEPTH, D, 6 * D), 0.5 * D ** -0.5)
    inp["b_ada"] = nrm((DEPTH, 6 * D), 0.02)
    inp["w_mla_in"] = nrm((N_ATTN_LAYERS, D, Q_LORA + KV_LORA + ROPE_DIM), D ** -0.5)
    inp["g_mla_q"] = gain((N_ATTN_LAYERS, Q_LORA))
    inp["g_mla_kv"] = gain((N_ATTN_LAYERS, KV_LORA))
    inp["w_mla_uq"] = nrm((N_ATTN_LAYERS, Q_LORA, N_HEADS * (NOPE_DIM + ROPE_DIM)), Q_LORA ** -0.5)
    inp["w_mla_uk"] = nrm((N_ATTN_LAYERS, KV_LORA, N_HEADS * NOPE_DIM), KV_LORA ** -0.5)
    inp["w_mla_uv"] = nrm((N_ATTN_LAYERS, KV_LORA, N_HEADS * V_DIM), KV_LORA ** -0.5)
    inp["w_mla_o"] = nrm((N_ATTN_LAYERS, N_HEADS * V_DIM, D), (N_HEADS * V_DIM) ** -0.5)
    inp["w_rec_in"] = nrm((N_REC_LAYERS, D, 2 * LRU_WIDTH), D ** -0.5)
    inp["w_rec_conv"] = nrm((N_REC_LAYERS, LRU_CONV_W, LRU_WIDTH), LRU_CONV_W ** -0.5)
    inp["b_rec_conv"] = nrm((N_REC_LAYERS, LRU_WIDTH), 0.01)
    inp["w_rec_gx"] = nrm((N_REC_LAYERS, 2, N_LRU_BLOCKS, LRU_BLOCK, LRU_BLOCK), LRU_BLOCK ** -0.5)
    inp["b_rec_gx"] = nrm((N_REC_LAYERS, 2, LRU_WIDTH), 0.01)
    inp["w_rec_ga"] = nrm((N_REC_LAYERS, 2, N_LRU_BLOCKS, LRU_BLOCK, LRU_BLOCK), LRU_BLOCK ** -0.5)
    inp["b_rec_ga"] = nrm((N_REC_LAYERS, 2, LRU_WIDTH), 0.01)
    a0 = jax.random.uniform(next(ks), (N_REC_LAYERS, 2, LRU_WIDTH), jnp.float32, 0.9, 0.999)
    inp["rec_lambda"] = jnp.log(a0) - jnp.log1p(-a0)
    inp["w_rec_out"] = nrm((N_REC_LAYERS, LRU_WIDTH, D), LRU_WIDTH ** -0.5)
    inp["w_ffn_up"] = nrm((DEPTH, D, 2 * D_FF), D ** -0.5)
    inp["w_ffn_conv"] = nrm((DEPTH, FFN_CONV_W, 2 * D_FF), FFN_CONV_W ** -0.5)
    inp["b_ffn_conv"] = nrm((DEPTH, 2 * D_FF), 0.01)
    inp["w_ffn_down"] = nrm((DEPTH, D_FF, D), D_FF ** -0.5)
    return inp


def reference(x_prompt, x_sample, cache_ckv, cache_krope, state_lru, c, c_ctx,
              g_mix, g_ffn, g_final, w_ada, b_ada,
              w_mla_in, g_mla_q, g_mla_kv, w_mla_uq, w_mla_uk, w_mla_uv, w_mla_o,
              w_rec_in, w_rec_conv, b_rec_conv, w_rec_gx, b_rec_gx, w_rec_ga, b_rec_ga,
              rec_lambda, w_rec_out,
              w_ffn_up, w_ffn_conv, b_ffn_conv, w_ffn_down):
    xp = x_prompt
    ckv_new, kr_new, lru_new = [], [], []
    for layer in range(DEPTH):
        sh1, sc1, g1, sh2, sc2, g2 = adaln(c_ctx[None, :], w_ada[layer], b_ada[layer])
        h = rmsnorm(xp, g_mix[layer]) * (1 + sc1) + sh1
        j = layer // N_MIXERS
        if layer % N_MIXERS == 0:
            qn, qr, ckv, kr = mla_project(h, w_mla_in[j], g_mla_q[j], g_mla_kv[j], w_mla_uq[j])
            kn, v = mla_expand(ckv, w_mla_uk[j], w_mla_uv[j])
            out = mla_attention(qn, qr, kn, kr, v) @ w_mla_o[j]
            ckv_new.append(ckv)
            kr_new.append(kr)
        else:
            h0 = jnp.zeros((xp.shape[0], LRU_WIDTH), jnp.float32)
            out, sf, sb = recurrent_mixer(h, w_rec_in[j], w_rec_conv[j], b_rec_conv[j],
                                          w_rec_gx[j], b_rec_gx[j], w_rec_ga[j], b_rec_ga[j],
                                          rec_lambda[j], w_rec_out[j], h0, h0)
            lru_new.append(jnp.stack([sf, sb], axis=1))
        xp = xp + g1 * out
        h = rmsnorm(xp, g_ffn[layer]) * (1 + sc2) + sh2
        xp = xp + g2 * conv_ffn(h, w_ffn_up[layer], w_ffn_conv[layer], b_ffn_conv[layer], w_ffn_down[layer])
    y_prompt = rmsnorm(xp, g_final)
    new_cache_ckv = jnp.stack(ckv_new, axis=1)
    new_cache_krope = jnp.stack(kr_new, axis=1)
    new_state_lru = jnp.stack(lru_new, axis=1)

    xs = x_sample
    cos, sin = axial_rope_tables(xs.shape[1])
    for layer in range(DEPTH):
        sh1, sc1, g1, sh2, sc2, g2 = adaln(c, w_ada[layer], b_ada[layer])
        h = rmsnorm(xs, g_mix[layer]) * (1 + sc1) + sh1
        j = layer // N_MIXERS
        if layer % N_MIXERS == 0:
            qn, qr, ckv_l, kr_l = mla_project(h, w_mla_in[j], g_mla_q[j], g_mla_kv[j], w_mla_uq[j])
            qr = apply_axial_rope(qr, cos[:, None, :], sin[:, None, :])
            kr_l = apply_axial_rope(kr_l, cos, sin)
            kn_c, v_c = mla_expand(cache_ckv[:, j], w_mla_uk[j], w_mla_uv[j])
            kn_l, v_l = mla_expand(ckv_l, w_mla_uk[j], w_mla_uv[j])
            kn = jnp.concatenate([kn_c, kn_l], axis=1)
            kr = jnp.concatenate([cache_krope[:, j], kr_l], axis=1)
            v = jnp.concatenate([v_c, v_l], axis=1)
            out = mla_attention(qn, qr, kn, kr, v) @ w_mla_o[j]
        else:
            out, _, _ = recurrent_mixer(h, w_rec_in[j], w_rec_conv[j], b_rec_conv[j],
                                        w_rec_gx[j], b_rec_gx[j], w_rec_ga[j], b_rec_ga[j],
                                        rec_lambda[j], w_rec_out[j],
                                        state_lru[:, j, 0], state_lru[:, j, 1])
        xs = xs + g1 * out
        h = rmsnorm(xs, g_ffn[layer]) * (1 + sc2) + sh2
        xs = xs + g2 * conv_ffn(h, w_ffn_up[layer], w_ffn_conv[layer], b_ffn_conv[layer], w_ffn_down[layer])
    y_sample = rmsnorm(xs, g_final)

    return (y_prompt, y_sample, new_cache_ckv, new_cache_krope, new_state_lru)
```

```python
import functools
from typing import NamedTuple

import jax
import jax.numpy as jnp
from jax import lax
from jax.experimental import pallas as pl
from jax.experimental.pallas import tpu as pltpu

F32 = jnp.float32
MM_DTYPE = jnp.bfloat16

LANES = 128
SUBLANES = 8
VMEM_LIMIT_BYTES = 60 * 1024 * 1024
N_MOD = 6
MOD_ROWS = 16


class Cfg(NamedTuple):
    n_heads: int
    q_lora: int
    kv_lora: int
    nope: int
    rope: int
    v_dim: int
    grid_w: int
    rope_base: float
    lru_blocks: int
    lru_c: float
    eps: float


def _params(sem):
    return pltpu.CompilerParams(dimension_semantics=sem, vmem_limit_bytes=VMEM_LIMIT_BYTES)


def _dot(a, b):
    return jnp.dot(a, b, preferred_element_type=F32)


def _dot_nt(a, b):
    return lax.dot_general(a, b, (((1,), (1,)), ((), ())), preferred_element_type=F32)


def _const_spec(shape):
    nd = len(shape)
    return pl.BlockSpec(shape, lambda *_: (0,) * nd)


def _rms(x, g, eps):
    return x * lax.rsqrt(jnp.mean(x * x, axis=-1, keepdims=True) + eps) * g


def _norm_mod(x, g, mod_ref, shift_idx, scale_idx, eps):
    y = _rms(x, g, eps)
    return y * (1.0 + mod_ref[scale_idx:scale_idx + 1, :]) + mod_ref[shift_idx:shift_idx + 1, :]


def _adaln_body(cond_ref, w_ref, b_ref, o_ref):
    c = cond_ref[...]
    s = (c * jax.nn.sigmoid(c)).astype(MM_DTYPE)
    o_ref[...] = _dot(s, w_ref[...].astype(MM_DTYPE)) + b_ref[...]


def _adaln(cond, w_ada, b_ada, tn):
    n_layers, d, n_out = w_ada.shape
    return pl.pallas_call(
        _adaln_body,
        out_shape=jax.ShapeDtypeStruct((n_layers, MOD_ROWS, n_out), F32),
        grid=(n_layers, n_out // tn),
        in_specs=[
            _const_spec((MOD_ROWS, d)),
            pl.BlockSpec((None, d, tn), lambda l, n: (l, 0, n)),
            pl.BlockSpec((None, 1, tn), lambda l, n: (l, 0, n)),
        ],
        out_specs=pl.BlockSpec((None, MOD_ROWS, tn), lambda l, n: (l, 0, n)),
        compiler_params=_params(("arbitrary", "arbitrary")),
        name="adaln",
    )(cond, w_ada, b_ada.reshape(n_layers, 1, n_out))


def _nmm_body(x_ref, mod_ref, g_ref, w_ref, o_ref, h_ref, *, shift_idx, scale_idx, eps):
    @pl.when(pl.program_id(1) == 0)
    def _():
        h = _norm_mod(x_ref[...], g_ref[...], mod_ref, shift_idx, scale_idx, eps)
        h_ref[...] = h.astype(MM_DTYPE)

    o_ref[...] = _dot(h_ref[...], w_ref[...])


def _norm_mod_matmul(x, mod, mod_row, g, w, *, tm, tn, eps):
    n, d = x.shape
    m = w.shape[1]
    return pl.pallas_call(
        functools.partial(_nmm_body, shift_idx=0, scale_idx=1, eps=eps),
        out_shape=jax.ShapeDtypeStruct((n, m), F32),
        grid=(n // tm, m // tn),
        in_specs=[
            pl.BlockSpec((tm, d), lambda i, j: (i, 0)),
            pl.BlockSpec((None, N_MOD, d), lambda i, j: (mod_row(i, tm), 0, 0)),
            _const_spec((1, d)),
            pl.BlockSpec((d, tn), lambda i, j: (0, j)),
        ],
        out_specs=pl.BlockSpec((tm, tn), lambda i, j: (i, j)),
        scratch_shapes=[pltpu.VMEM((tm, d), MM_DTYPE)],
        compiler_params=_params(("arbitrary", "arbitrary")),
        name="norm_mod_matmul",
    )(x, mod, g, w)


def _mmres_body(a_ref, w_ref, x_ref, mod_ref, o_ref, *, gate_idx):
    gate = mod_ref[gate_idx:gate_idx + 1, :]
    o_ref[...] = x_ref[...] + gate * _dot(a_ref[...], w_ref[...])


def _matmul_residual(a, w, x, mod, mod_row, *, tm):
    n, k = a.shape
    d = w.shape[1]
    return pl.pallas_call(
        functools.partial(_mmres_body, gate_idx=2),
        out_shape=jax.ShapeDtypeStruct((n, d), F32),
        grid=(n // tm,),
        in_specs=[
            pl.BlockSpec((tm, k), lambda i: (i, 0)),
            _const_spec((k, d)),
            pl.BlockSpec((tm, d), lambda i: (i, 0)),
            pl.BlockSpec((None, N_MOD, d), lambda i: (mod_row(i, tm), 0, 0)),
        ],
        out_specs=pl.BlockSpec((tm, d), lambda i: (i, 0)),
        compiler_params=_params(("arbitrary",)),
        name="matmul_residual",
    )(a, w, x, mod)


def _ffn_body(x_ref, mod_ref, g_ref, wg_ref, wv_ref, cwg_ref, cwv_ref, cbg_ref, cbv_ref, wd_ref,
              o_ref, h_ref, *, seq, eps):
    j = pl.program_id(1)

    @pl.when(j == 0)
    def _():
        h = _norm_mod(x_ref[...], g_ref[...], mod_ref, 3, 4, eps)
        h_ref[...] = h.astype(MM_DTYPE)

    h = h_ref[...]
    tm = h.shape[0]
    row = lax.broadcasted_iota(jnp.int32, (tm, 1), 0) % seq
    first = row == 0
    last = row == seq - 1

    def conv(u, cw_ref, cb_ref):
        prev = jnp.where(first, 0.0, pltpu.roll(u, 1, 0))
        nxt = jnp.where(last, 0.0, pltpu.roll(u, tm - 1, 0))
        return cb_ref[...] + prev * cw_ref[0:1, :] + u * cw_ref[1:2, :] + nxt * cw_ref[2:3, :]

    gate = conv(_dot(h, wg_ref[...]), cwg_ref, cbg_ref)
    val = conv(_dot(h, wv_ref[...]), cwv_ref, cbv_ref)
    act = (gate * jax.nn.sigmoid(gate) * val).astype(MM_DTYPE)
    part = _dot(act, wd_ref[...])

    @pl.when(j == 0)
    def _():
        o_ref[...] = part

    @pl.when(j > 0)
    def _():
        o_ref[...] += part

    @pl.when(j == pl.num_programs(1) - 1)
    def _():
        o_ref[...] = x_ref[...] + mod_ref[5:6, :] * o_ref[...]


def _conv_ffn(x, mod, mod_row, g, w_up, w_conv, b_conv, w_down, *, seq, tm, tf, x_buffers, eps):
    n, d = x.shape
    f = w_down.shape[0]
    assert w_conv.shape == (3, 2 * f) and tm % seq == 0 and f % tf == 0
    nf = f // tf
    return pl.pallas_call(
        functools.partial(_ffn_body, seq=seq, eps=eps),
        out_shape=jax.ShapeDtypeStruct((n, d), F32),
        grid=(n // tm, nf),
        in_specs=[
            pl.BlockSpec((tm, d), lambda i, j: (i, 0), pipeline_mode=pl.Buffered(x_buffers)),
            pl.BlockSpec((None, N_MOD, d), lambda i, j: (mod_row(i, tm), 0, 0)),
            _const_spec((1, d)),
            pl.BlockSpec((d, tf), lambda i, j: (0, j)),
            pl.BlockSpec((d, tf), lambda i, j: (0, j + nf)),
            pl.BlockSpec((3, tf), lambda i, j: (0, j)),
            pl.BlockSpec((3, tf), lambda i, j: (0, j + nf)),
            pl.BlockSpec((1, tf), lambda i, j: (0, j)),
            pl.BlockSpec((1, tf), lambda i, j: (0, j + nf)),
            pl.BlockSpec((tf, d), lambda i, j: (j, 0)),
        ],
        out_specs=pl.BlockSpec((tm, d), lambda i, j: (i, 0)),
        scratch_shapes=[pltpu.VMEM((tm, d), MM_DTYPE)],
        compiler_params=_params(("arbitrary", "arbitrary")),
        name="conv_ffn",
    )(x, mod, g, w_up, w_up, w_conv, w_conv, b_conv, b_conv, w_down)


def _mla_mid_body(*refs, cfg, rope):
    if rope:
        (proj_ref, gq_ref, gkv_ref, wqn_ref, wqr_ref, wqs_ref, wuk_ref, wuv_ref, cos_ref, sin_ref,
         qn_ref, qr_ref, kn_ref, v_ref, krp_ref) = refs
    else:
        (proj_ref, gq_ref, gkv_ref, wqn_ref, wqr_ref, wuk_ref, wuv_ref,
         qn_ref, qr_ref, kn_ref, v_ref, krp_ref, ckv_ref, kr_ref) = refs
    ql, kl = cfg.q_lora, cfg.kv_lora
    p = proj_ref[...]
    cqn = _rms(p[:, :ql], gq_ref[...], cfg.eps).astype(MM_DTYPE)
    ckv = _rms(p[:, ql:ql + kl], gkv_ref[...], cfg.eps)
    krp = p[:, ql + kl:ql + kl + LANES]
    qn_ref[...] = _dot(cqn, wqn_ref[...]).astype(MM_DTYPE)
    qr = _dot(cqn, wqr_ref[...])
    if rope:
        krs = p[:, ql + kl + LANES:ql + kl + 2 * LANES]
        qs = _dot(cqn, wqs_ref[...])
        cos = cos_ref[...]
        sin = sin_ref[...]
        for h in range(cfg.n_heads):
            sl = slice(h * LANES, (h + 1) * LANES)
            qr_ref[:, sl] = (qr[:, sl] * cos + qs[:, sl] * sin).astype(MM_DTYPE)
        krp_ref[...] = (krp * cos + krs * sin).astype(MM_DTYPE)
    else:
        qr_ref[...] = qr.astype(MM_DTYPE)
        krp_ref[...] = krp.astype(MM_DTYPE)
        ckv_ref[...] = ckv
        kr_ref[...] = krp[:, :cfg.rope]
    ckv_b = ckv.astype(MM_DTYPE)
    kn_ref[...] = _dot(ckv_b, wuk_ref[...]).astype(MM_DTYPE)
    v_ref[...] = _dot(ckv_b, wuv_ref[...]).astype(MM_DTYPE)


def _mla_mid(proj, g_q, g_kv, w_qn, w_qr, w_qs, w_uk, w_uv, cos, sin, *, cfg, seq, tm):
    n, pw = proj.shape
    rope = cos is not None
    hw = cfg.n_heads * LANES
    row = lambda i: (i, 0)
    ins = [proj, g_q, g_kv, w_qn, w_qr] + ([w_qs] if rope else []) + [w_uk, w_uv]
    in_specs = [pl.BlockSpec((tm, pw), row), _const_spec(g_q.shape), _const_spec(g_kv.shape),
                _const_spec(w_qn.shape), _const_spec(w_qr.shape)]
    in_specs += ([_const_spec(w_qs.shape)] if rope else []) + [_const_spec(w_uk.shape), _const_spec(w_uv.shape)]
    out_shape = [jax.ShapeDtypeStruct((n, hw), MM_DTYPE)] * 4 + [jax.ShapeDtypeStruct((n, LANES), MM_DTYPE)]
    out_specs = [pl.BlockSpec((tm, hw), row)] * 4 + [pl.BlockSpec((tm, LANES), row)]
    if rope:
        bps = seq // tm
        ins += [cos, sin]
        in_specs += [pl.BlockSpec((tm, LANES), lambda i: (i % bps, 0))] * 2
    else:
        out_shape += [jax.ShapeDtypeStruct((n, cfg.kv_lora), F32), jax.ShapeDtypeStruct((n, cfg.rope), F32)]
        out_specs += [pl.BlockSpec((tm, cfg.kv_lora), row), pl.BlockSpec((tm, cfg.rope), row)]
    return pl.pallas_call(
        functools.partial(_mla_mid_body, cfg=cfg, rope=rope),
        out_shape=out_shape,
        grid=(n // tm,),
        in_specs=in_specs,
        out_specs=out_specs,
        compiler_params=_params(("arbitrary",)),
        name="mla_mid_rope" if rope else "mla_mid",
    )(*ins)


def _cache_expand_body(ckv_ref, kr_ref, wuk_ref, wuv_ref, kn_ref, v_ref, krp_ref, *, rope):
    c = ckv_ref[...].astype(MM_DTYPE)
    kn_ref[...] = _dot(c, wuk_ref[...]).astype(MM_DTYPE)
    v_ref[...] = _dot(c, wuv_ref[...]).astype(MM_DTYPE)
    krp_ref[:, :rope] = kr_ref[...].astype(MM_DTYPE)
    krp_ref[:, rope:] = jnp.zeros((krp_ref.shape[0], LANES - rope), MM_DTYPE)


def _cache_expand(cache_ckv, cache_krope, layer, w_uk, w_uv, *, cfg):
    b, _, p, kl = cache_ckv.shape
    hw = cfg.n_heads * LANES
    row = lambda i: (i, 0)
    return pl.pallas_call(
        functools.partial(_cache_expand_body, rope=cfg.rope),
        out_shape=[jax.ShapeDtypeStruct((b * p, hw), MM_DTYPE)] * 2 + [jax.ShapeDtypeStruct((b * p, LANES), MM_DTYPE)],
        grid=(b,),
        in_specs=[
            pl.BlockSpec((None, None, p, kl), lambda i: (i, layer, 0, 0)),
            pl.BlockSpec((None, None, p, cfg.rope), lambda i: (i, layer, 0, 0)),
            _const_spec(w_uk.shape),
            _const_spec(w_uv.shape),
        ],
        out_specs=[pl.BlockSpec((p, hw), row)] * 2 + [pl.BlockSpec((p, LANES), row)],
        compiler_params=_params(("arbitrary",)),
        name="cache_expand",
    )(cache_ckv, cache_krope, w_uk, w_uv)


def _attn_body(*refs, hb, scale, cached):
    if cached:
        qn_ref, qr_ref, kn_ref, kr_ref, v_ref, knc_ref, krc_ref, vc_ref, o_ref = refs
    else:
        qn_ref, qr_ref, kn_ref, kr_ref, v_ref, o_ref = refs
    for h in range(hb):
        sl = slice(h * LANES, (h + 1) * LANES)
        qn = qn_ref[:, sl]
        qr = qr_ref[:, sl]
        s = (_dot_nt(qn, kn_ref[:, sl]) + _dot_nt(qr, kr_ref[...])) * scale
        m = jnp.max(s, axis=-1, keepdims=True)
        if cached:
            sc = (_dot_nt(qn, knc_ref[:, sl]) + _dot_nt(qr, krc_ref[...])) * scale
            m = jnp.maximum(m, jnp.max(sc, axis=-1, keepdims=True))
            pc = jnp.exp(sc - m)
        p = jnp.exp(s - m)
        den = jnp.sum(p, axis=-1, keepdims=True)
        o = _dot(p.astype(MM_DTYPE), v_ref[:, sl])
        if cached:
            den = den + jnp.sum(pc, axis=-1, keepdims=True)
            o = o + _dot(pc.astype(MM_DTYPE), vc_ref[:, sl])
        o_ref[:, sl] = (o / den).astype(MM_DTYPE)


def _attention(qn, qr, kn, krp, v, cache, *, cfg, seq, hb):
    n, hw = qn.shape
    cached = cache is not None
    blk = lambda rows: pl.BlockSpec((rows, hb * LANES), lambda b, g: (b, g))
    shared = lambda rows: pl.BlockSpec((rows, LANES), lambda b, g: (b, 0))
    ins = [qn, qr, kn, krp, v]
    in_specs = [blk(seq), blk(seq), blk(seq), shared(seq), blk(seq)]
    if cached:
        knc, vc, krc = cache
        past = knc.shape[0] // (n // seq)
        ins += [knc, krc, vc]
        in_specs += [blk(past), shared(past), blk(past)]
    return pl.pallas_call(
        functools.partial(_attn_body, hb=hb, scale=float(cfg.nope + cfg.rope) ** -0.5, cached=cached),
        out_shape=jax.ShapeDtypeStruct((n, hw), MM_DTYPE),
        grid=(n // seq, cfg.n_heads // hb),
        in_specs=in_specs,
        out_specs=blk(seq),
        compiler_params=_params(("arbitrary", "arbitrary")),
        name="attention_cached" if cached else "attention",
    )(*ins)


def _rec_scan_body(*refs, seq, cfg, has_state):
    if has_state:
        (y_ref, xb_ref, cw_ref, cb_ref, wg_ref, bg_ref, lam_ref, h0f_ref, h0b_ref,
         o_ref, af_ref, uf_ref, ab_ref, ub_ref) = refs
    else:
        (y_ref, xb_ref, cw_ref, cb_ref, wg_ref, bg_ref, lam_ref,
         o_ref, sf_ref, sb_ref, af_ref, uf_ref, ab_ref, ub_ref) = refs
    nseq = SUBLANES
    pitch = seq + 4
    t_idx = lax.broadcasted_iota(jnp.int32, (seq, 1), 0)
    lam = lam_ref[...]
    sp = jnp.maximum(-lam, 0.0) + jnp.log1p(jnp.exp(-jnp.abs(lam)))
    wg = wg_ref[...]
    bg = bg_ref[...]
    for b in range(nseq):
        x = xb_ref[b * seq:(b + 1) * seq, :]
        xc = cb_ref[...] + jnp.where(t_idx < 2, 0.0, pltpu.roll(x, 2, 0)) * cw_ref[0:1, :]
        xc = xc + jnp.where(t_idx < 1, 0.0, pltpu.roll(x, 1, 0)) * cw_ref[1:2, :]
        xc = xc + x * cw_ref[2:3, :]
        xc = xc + jnp.where(t_idx > seq - 2, 0.0, pltpu.roll(x, seq - 1, 0)) * cw_ref[3:4, :]
        gates = jax.nn.sigmoid(_dot(xc.astype(MM_DTYPE), wg) + bg)
        for d, (a_ref, u_ref) in enumerate(((af_ref, uf_ref), (ab_ref, ub_ref))):
            gx = gates[:, (2 * d) * LANES:(2 * d + 1) * LANES]
            ga = gates[:, (2 * d + 1) * LANES:(2 * d + 2) * LANES]
            log_a = -cfg.lru_c * ga * sp[d:d + 1, :]
            a = jnp.exp(log_a)
            u = jnp.sqrt(1.0 - a * a) * (gx * xc)
            a_ref[b * pitch:b * pitch + seq, :] = a
            u_ref[b * pitch:b * pitch + seq, :] = u

    if has_state:
        hf0 = h0f_ref[...]
        hb0 = h0b_ref[...]
    else:
        hf0 = jnp.zeros((nseq, LANES), F32)
        hb0 = hf0

    def step(t, carry):
        hf, hb = carry
        fwd = pl.ds(t, nseq, stride=pitch)
        hf = af_ref[fwd, :] * hf + uf_ref[fwd, :]
        uf_ref[fwd, :] = hf
        bwd = pl.ds(seq - 1 - t, nseq, stride=pitch)
        hb = ab_ref[bwd, :] * hb + ub_ref[bwd, :]
        ub_ref[bwd, :] = hb
        return hf, hb

    hf, hb = lax.fori_loop(0, seq, step, (hf0, hb0), unroll=8)
    if not has_state:
        sf_ref[...] = hf
        sb_ref[...] = hb
    for b in range(nseq):
        hsum = uf_ref[b * pitch:b * pitch + seq, :] + ub_ref[b * pitch:b * pitch + seq, :]
        y = y_ref[b * seq:(b + 1) * seq, :]
        o_ref[b * seq:(b + 1) * seq, :] = (hsum * jax.nn.gelu(y)).astype(MM_DTYPE)


def _rec_scan(proj, w_conv, b_conv, w_gates, b_gates, lam, state, *, cfg, seq):
    n, w2 = proj.shape
    w = w2 // 2
    nb = w // LANES
    assert nb == cfg.lru_blocks and w_conv.shape[0] == 4
    rows = SUBLANES * seq
    has_state = state is not None
    col = lambda g, c: (g, c)
    ins = [proj, proj, w_conv, b_conv, w_gates, b_gates, lam]
    in_specs = [
        pl.BlockSpec((rows, LANES), col),
        pl.BlockSpec((rows, LANES), lambda g, c: (g, c + nb)),
        pl.BlockSpec((4, LANES), lambda g, c: (0, c)),
        pl.BlockSpec((1, LANES), lambda g, c: (0, c)),
        pl.BlockSpec((None, LANES, 4 * LANES), lambda g, c: (c, 0, 0)),
        pl.BlockSpec((None, 1, 4 * LANES), lambda g, c: (c, 0, 0)),
        pl.BlockSpec((2, LANES), lambda g, c: (0, c)),
    ]
    out_shape = [jax.ShapeDtypeStruct((n, w), MM_DTYPE)]
    out_specs = [pl.BlockSpec((rows, LANES), col)]
    if has_state:
        ins += list(state)
        in_specs += [pl.BlockSpec((SUBLANES, LANES), col)] * 2
    else:
        nstate = n // seq
        out_shape += [jax.ShapeDtypeStruct((nstate, w), F32)] * 2
        out_specs += [pl.BlockSpec((SUBLANES, LANES), col)] * 2
    scratch = [pltpu.VMEM((SUBLANES * (seq + 4), LANES), F32)] * 4
    return pl.pallas_call(
        functools.partial(_rec_scan_body, seq=seq, cfg=cfg, has_state=has_state),
        out_shape=out_shape,
        grid=(n // rows, nb),
        in_specs=in_specs,
        out_specs=out_specs,
        scratch_shapes=scratch,
        compiler_params=_params(("arbitrary", "arbitrary")),
        name="rec_scan_state" if has_state else "rec_scan",
    )(*ins)


def _final_norm_body(x_ref, g_ref, o_ref, *, eps):
    o_ref[...] = _rms(x_ref[...], g_ref[...], eps)


def _final_norm(x, g, *, tm, eps):
    n, d = x.shape
    return pl.pallas_call(
        functools.partial(_final_norm_body, eps=eps),
        out_shape=jax.ShapeDtypeStruct((n, d), F32),
        grid=(n // tm,),
        in_specs=[pl.BlockSpec((tm, d), lambda i: (i, 0)), _const_spec((1, d))],
        out_specs=pl.BlockSpec((tm, d), lambda i: (i, 0)),
        compiler_params=_params(("arbitrary",)),
        name="final_norm",
    )(x, g)


def _swap_halves(w, rope):
    q = rope // 4
    return jnp.concatenate([w[..., q:2 * q], w[..., :q], w[..., 3 * q:], w[..., 2 * q:3 * q]], axis=-1)


def _pad_lanes(w):
    return jnp.pad(w, [(0, 0)] * (w.ndim - 1) + [(0, LANES - w.shape[-1])])


def _rope_tables(seq, cfg):
    rows = seq // cfg.grid_w
    row = jnp.repeat(jnp.arange(rows), cfg.grid_w).astype(F32)
    col = jnp.tile(jnp.arange(cfg.grid_w), rows).astype(F32)
    half = cfg.rope // 2
    inv = 1.0 / (cfg.rope_base ** (jnp.arange(0, half, 2, dtype=F32) / half))
    ar, ac = row[:, None] * inv, col[:, None] * inv
    cos = jnp.concatenate([jnp.cos(ar), jnp.cos(ar), jnp.cos(ac), jnp.cos(ac)], axis=-1)
    sin = jnp.concatenate([-jnp.sin(ar), jnp.sin(ar), -jnp.sin(ac), jnp.sin(ac)], axis=-1)
    return _pad_lanes(cos), _pad_lanes(sin)


def _pick(n, pref):
    return pref if n % pref == 0 else n


def _forward(cfg, x_prompt, x_sample, cache_ckv, cache_krope, state_lru, c, c_ctx,
             g_mix, g_ffn, g_final, w_ada, b_ada,
             w_mla_in, g_mla_q, g_mla_kv, w_mla_uq, w_mla_uk, w_mla_uv, w_mla_o,
             w_rec_in, w_rec_conv, b_rec_conv, w_rec_gx, b_rec_gx, w_rec_ga, b_rec_ga,
             rec_lambda, w_rec_out,
             w_ffn_up, w_ffn_conv, b_ffn_conv, w_ffn_down):
    depth, d = g_mix.shape
    bp, sp, _ = x_prompt.shape
    bs, ss, _ = x_sample.shape
    assert cfg.nope == LANES and cfg.v_dim == LANES and cfg.rope <= LANES and bs + 1 <= MOD_ROWS
    assert bp % SUBLANES == 0 and bs % SUBLANES == 0
    eps = cfg.eps
    cast = lambda w: w.astype(MM_DTYPE)

    cond = jnp.zeros((MOD_ROWS, d), F32).at[0].set(c_ctx).at[1:1 + bs].set(c)
    mod = _adaln(cond, w_ada, b_ada, _pick(w_ada.shape[-1], 1024)).reshape(depth, MOD_ROWS, N_MOD, d)

    ql, kl, r, nh = cfg.q_lora, cfg.kv_lora, cfg.rope, cfg.n_heads
    w_kr = w_mla_in[..., ql + kl:]
    w_in_p = cast(jnp.concatenate(
        [w_mla_in[..., :ql + kl], _pad_lanes(w_kr), _pad_lanes(_swap_halves(w_kr, r))], axis=-1))
    uq = w_mla_uq.reshape(w_mla_uq.shape[0], ql, nh, cfg.nope + r)
    flat = lambda w: cast(w.reshape(w.shape[0], ql, nh * LANES))
    w_qn, w_qr, w_qs = flat(uq[..., :cfg.nope]), flat(_pad_lanes(uq[..., cfg.nope:])), flat(
        _pad_lanes(_swap_halves(uq[..., cfg.nope:], r)))
    w_uk, w_uv, w_o = cast(w_mla_uk), cast(w_mla_uv), cast(w_mla_o)
    cos, sin = _rope_tables(ss, cfg)

    w_gates = cast(jnp.concatenate([w_rec_gx[:, 0], w_rec_ga[:, 0], w_rec_gx[:, 1], w_rec_ga[:, 1]], axis=-1))
    nb = cfg.lru_blocks
    blk = lambda b: b.reshape(b.shape[0], nb, 1, LANES)
    b_gates = jnp.concatenate([blk(b_rec_gx[:, 0]), blk(b_rec_ga[:, 0]), blk(b_rec_gx[:, 1]), blk(b_rec_ga[:, 1])],
                              axis=-1)
    w_rin, w_rout = cast(w_rec_in), cast(w_rec_out)
    w_up, w_down = cast(w_ffn_up), cast(w_ffn_down)

    ctx_row = lambda i, tm: 0
    f = w_down.shape[1]
    tf = _pick(f, 512)

    def run_group(x, seq, mod_row, sample):
        n = x.shape[0]
        tm_ffn = seq if sample else max(seq, _pick(n, 512))
        tm = _pick(seq, 256)
        ckv_new, kr_new, lru_new = [], [], []
        for layer in range(depth):
            m = mod[layer]
            j = layer // 2
            gm = g_mix[layer][None]
            if layer % 2 == 0:
                proj = _norm_mod_matmul(x, m, mod_row, gm, w_in_p[j], tm=tm, tn=w_in_p.shape[-1], eps=eps)
                if sample:
                    qn, qr, kn, v, krp = _mla_mid(proj, g_mla_q[j][None], g_mla_kv[j][None], w_qn[j], w_qr[j],
                                                  w_qs[j], w_uk[j], w_uv[j], cos, sin, cfg=cfg, seq=seq, tm=tm)
                    cache = _cache_expand(cache_ckv, cache_krope, j, w_uk[j], w_uv[j], cfg=cfg)
                    att = _attention(qn, qr, kn, krp, v, cache, cfg=cfg, seq=seq, hb=min(2, nh))
                else:
                    qn, qr, kn, v, krp, ckv, kr = _mla_mid(proj, g_mla_q[j][None], g_mla_kv[j][None], w_qn[j],
                                                           w_qr[j], None, w_uk[j], w_uv[j], None, None,
                                                           cfg=cfg, seq=seq, tm=tm)
                    ckv_new.append(ckv)
                    kr_new.append(kr)
                    att = _attention(qn, qr, kn, krp, v, None, cfg=cfg, seq=seq, hb=nh)
                x = _matmul_residual(att, w_o[j], x, m, mod_row, tm=tm)
            else:
                proj = _norm_mod_matmul(x, m, mod_row, gm, w_rin[j], tm=tm, tn=_pick(w_rin.shape[-1], 1024), eps=eps)
                outs = _rec_scan(proj, w_rec_conv[j], b_rec_conv[j][None], w_gates[j], b_gates[j], rec_lambda[j],
                                 (state_lru[:, j, 0], state_lru[:, j, 1]) if sample else None, cfg=cfg, seq=seq)
                if not sample:
                    lru_new.append(jnp.stack(outs[1:], axis=1))
                x = _matmul_residual(outs[0], w_rout[j], x, m, mod_row, tm=tm)
            x = _conv_ffn(x, m, mod_row, g_ffn[layer][None], w_up[layer], w_ffn_conv[layer], b_ffn_conv[layer][None],
                          w_down[layer], seq=seq, tm=tm_ffn, tf=tf, x_buffers=1 if tm_ffn > 512 else 2, eps=eps)
        y = _final_norm(x, g_final[None], tm=tm, eps=eps)
        return y, ckv_new, kr_new, lru_new

    y_p, ckv_new, kr_new, lru_new = run_group(x_prompt.reshape(bp * sp, d), sp, ctx_row, False)
    y_s, _, _, _ = run_group(x_sample.reshape(bs * ss, d), ss, lambda i, tm: 1 + (i * tm) // ss, True)
    new_ckv = jnp.stack([t.reshape(bp, sp, kl) for t in ckv_new], axis=1)
    new_kr = jnp.stack([t.reshape(bp, sp, r) for t in kr_new], axis=1)
    new_lru = jnp.stack(lru_new, axis=1)
    return y_p.reshape(bp, sp, d), y_s.reshape(bs, ss, d), new_ckv, new_kr, new_lru


CFG = Cfg(n_heads=16, q_lora=512, kv_lora=512, nope=128, rope=64, v_dim=128, grid_w=64, rope_base=10000.0,
          lru_blocks=16, lru_c=8.0, eps=1e-6)


def kernel(x_prompt, x_sample, cache_ckv, cache_krope, state_lru, c, c_ctx, g_mix, g_ffn, g_final, w_ada, b_ada, w_mla_in, g_mla_q, g_mla_kv, w_mla_uq, w_mla_uk, w_mla_uv, w_mla_o, w_rec_in, w_rec_conv, b_rec_conv, w_rec_gx, b_rec_gx, w_rec_ga, b_rec_ga, rec_lambda, w_rec_out, w_ffn_up, w_ffn_conv, b_ffn_conv, w_ffn_down):
    return _forward(CFG, x_prompt, x_sample, cache_ckv, cache_krope, state_lru, c, c_ctx, g_mix, g_ffn, g_final,
                    w_ada, b_ada, w_mla_in, g_mla_q, g_mla_kv, w_mla_uq, w_mla_uk, w_mla_uv, w_mla_o,
                    w_rec_in, w_rec_conv, b_rec_conv, w_rec_gx, b_rec_gx, w_rec_ga, b_rec_ga, rec_lambda, w_rec_out,
                    w_ffn_up, w_ffn_conv, b_ffn_conv, w_ffn_down)
```

```python
import functools
import math
from typing import NamedTuple

import jax
import jax.numpy as jnp
from jax import lax
from jax.experimental import pallas as pl
from jax.experimental.pallas import tpu as pltpu

F32 = jnp.float32
MM_DTYPE = jnp.bfloat16

LANES = 128
SUBLANES = 8
VMEM_LIMIT_BYTES = 60 * 1024 * 1024
N_MOD = 6
MOD_ROWS = 16


class Cfg(NamedTuple):
    n_heads: int
    q_lora: int
    kv_lora: int
    nope: int
    rope: int
    v_dim: int
    grid_w: int
    rope_base: float
    lru_blocks: int
    lru_c: float
    eps: float


def _params(sem):
    return pltpu.CompilerParams(dimension_semantics=sem, vmem_limit_bytes=VMEM_LIMIT_BYTES)


def _dot(a, b):
    return jnp.dot(a, b, preferred_element_type=F32)


def _dot_nt(a, b):
    return lax.dot_general(a, b, (((1,), (1,)), ((), ())), preferred_element_type=F32)


def _const_spec(shape):
    nd = len(shape)
    return pl.BlockSpec(shape, lambda *_: (0,) * nd)


def _layer_spec(arr, layer):
    nd = arr.ndim
    return pl.BlockSpec((None,) + arr.shape[1:], lambda *_: (layer,) + (0,) * (nd - 1))


def _mod_spec(d, layer, mod_row, tm):
    return pl.BlockSpec((None, None, N_MOD, d), lambda i, *_: (layer, mod_row(i, tm), 0, 0))


def _rms(x, g, eps):
    return x * lax.rsqrt(jnp.mean(x * x, axis=-1, keepdims=True) + eps) * g


def _norm_mod(x, g, mod_ref, shift_idx, scale_idx, eps):
    y = _rms(x, g, eps)
    return y * (1.0 + mod_ref[scale_idx:scale_idx + 1, :]) + mod_ref[shift_idx:shift_idx + 1, :]


def _adaln_body(cond_ref, w_ref, b_ref, o_ref):
    c = cond_ref[...]
    s = (c * jax.nn.sigmoid(c)).astype(MM_DTYPE)
    o_ref[...] = _dot(s, w_ref[...].astype(MM_DTYPE)) + b_ref[...]


def _adaln(cond, w_ada, b_ada, tn):
    n_layers, d, n_out = w_ada.shape
    return pl.pallas_call(
        _adaln_body,
        out_shape=jax.ShapeDtypeStruct((n_layers, MOD_ROWS, n_out), F32),
        grid=(n_layers, n_out // tn),
        in_specs=[
            _const_spec((MOD_ROWS, d)),
            pl.BlockSpec((None, d, tn), lambda l, n: (l, 0, n)),
            pl.BlockSpec((None, 1, tn), lambda l, n: (l, 0, n)),
        ],
        out_specs=pl.BlockSpec((None, MOD_ROWS, tn), lambda l, n: (l, 0, n)),
        compiler_params=_params(("arbitrary", "arbitrary")),
        name="adaln",
    )(cond, w_ada, b_ada.reshape(n_layers, 1, n_out))


def _nmm_body(x_ref, mod_ref, g_ref, w_ref, o_ref, h_ref, *, eps):
    @pl.when(pl.program_id(1) == 0)
    def _():
        h = _norm_mod(x_ref[...], g_ref[...], mod_ref, 0, 1, eps)
        h_ref[...] = h.astype(MM_DTYPE)

    o_ref[...] = _dot(h_ref[...], w_ref[...])


def _norm_mod_matmul(x, mod, mod_row, g, w, layer, wl, *, tm, tn, eps):
    n, d = x.shape
    m = w.shape[2]
    return pl.pallas_call(
        functools.partial(_nmm_body, eps=eps),
        out_shape=jax.ShapeDtypeStruct((n, m), F32),
        grid=(n // tm, m // tn),
        in_specs=[
            pl.BlockSpec((tm, d), lambda i, j: (i, 0)),
            _mod_spec(d, layer, mod_row, tm),
            _layer_spec(g, layer),
            pl.BlockSpec((None, d, tn), lambda i, j: (wl, 0, j)),
        ],
        out_specs=pl.BlockSpec((tm, tn), lambda i, j: (i, j)),
        scratch_shapes=[pltpu.VMEM((tm, d), MM_DTYPE)],
        compiler_params=_params(("arbitrary", "arbitrary")),
        name="norm_mod_matmul",
    )(x, mod, g, w)


def _mmres_body(a_ref, w_ref, x_ref, mod_ref, o_ref):
    o_ref[...] = x_ref[...] + mod_ref[2:3, :] * _dot(a_ref[...], w_ref[...])


def _matmul_residual(a, w, wl, x, mod, layer, mod_row, *, tm):
    n, k = a.shape
    d = w.shape[2]
    return pl.pallas_call(
        _mmres_body,
        out_shape=jax.ShapeDtypeStruct((n, d), F32),
        grid=(n // tm,),
        in_specs=[
            pl.BlockSpec((tm, k), lambda i: (i, 0)),
            _layer_spec(w, wl),
            pl.BlockSpec((tm, d), lambda i: (i, 0)),
            _mod_spec(d, layer, mod_row, tm),
        ],
        out_specs=pl.BlockSpec((tm, d), lambda i: (i, 0)),
        compiler_params=_params(("arbitrary",)),
        name="matmul_residual",
    )(a, w, x, mod)


def _ffn_body(*refs, seq, eps, n_sub, final):
    if final:
        (x_ref, mod_ref, g_ref, wg_ref, wv_ref, cwg_ref, cwv_ref, cbg_ref, cbv_ref, wd_ref, gf_ref,
         o_ref, h_ref) = refs
    else:
        (x_ref, mod_ref, g_ref, wg_ref, wv_ref, cwg_ref, cwv_ref, cbg_ref, cbv_ref, wd_ref,
         o_ref, h_ref) = refs
    j = pl.program_id(1)

    @pl.when(j == 0)
    def _():
        h = _norm_mod(x_ref[...], g_ref[...], mod_ref, 3, 4, eps)
        h_ref[...] = h.astype(MM_DTYPE)
        o_ref[...] = jnp.zeros_like(o_ref)

    h = h_ref[...]
    tm = h.shape[0]
    row = lax.broadcasted_iota(jnp.int32, (tm, 1), 0) % seq
    first = row == 0
    last = row == seq - 1

    def conv(u, cw, cb):
        prev = jnp.where(first, 0.0, pltpu.roll(u, 1, 0))
        nxt = jnp.where(last, 0.0, pltpu.roll(u, tm - 1, 0))
        return cb + prev * cw[0:1, :] + u * cw[1:2, :] + nxt * cw[2:3, :]

    ts = wg_ref.shape[1] // n_sub
    part = None
    for s in range(n_sub):
        sl = slice(s * ts, (s + 1) * ts)
        gate = conv(_dot(h, wg_ref[:, sl]), cwg_ref[:, sl], cbg_ref[:, sl])
        val = conv(_dot(h, wv_ref[:, sl]), cwv_ref[:, sl], cbv_ref[:, sl])
        act = (gate * jax.nn.sigmoid(gate) * val).astype(MM_DTYPE)
        p = _dot(act, wd_ref[sl, :])
        part = p if part is None else part + p
    o_ref[...] += part

    @pl.when(j == pl.num_programs(1) - 1)
    def _():
        y = x_ref[...] + mod_ref[5:6, :] * o_ref[...]
        if final:
            y = _rms(y, gf_ref[...], eps)
        o_ref[...] = y


def _conv_ffn(x, mod, layer, mod_row, g, w_up, w_conv, b_conv, w_down, g_final, *, seq, tm, tf, n_sub, x_buffers,
              eps):
    n, d = x.shape
    f = w_down.shape[1]
    assert w_conv.shape[1:] == (3, 2 * f) and tm % seq == 0 and f % tf == 0 and tf % (n_sub * LANES) == 0
    nf = f // tf
    final = g_final is not None
    lo = lambda rows: pl.BlockSpec((None, rows, tf), lambda i, j: (layer, 0, j))
    hi = lambda rows: pl.BlockSpec((None, rows, tf), lambda i, j: (layer, 0, j + nf))
    ins = [x, mod, g, w_up, w_up, w_conv, w_conv, b_conv, b_conv, w_down] + ([g_final] if final else [])
    in_specs = [
        pl.BlockSpec((tm, d), lambda i, j: (i, 0), pipeline_mode=pl.Buffered(x_buffers)),
        _mod_spec(d, layer, mod_row, tm),
        _layer_spec(g, layer),
        lo(d), hi(d), lo(3), hi(3), lo(1), hi(1),
        pl.BlockSpec((None, tf, d), lambda i, j: (layer, j, 0)),
    ] + ([_const_spec(g_final.shape)] if final else [])
    return pl.pallas_call(
        functools.partial(_ffn_body, seq=seq, eps=eps, n_sub=n_sub, final=final),
        out_shape=jax.ShapeDtypeStruct((n, d), F32),
        grid=(n // tm, nf),
        in_specs=in_specs,
        out_specs=pl.BlockSpec((tm, d), lambda i, j: (i, 0)),
        scratch_shapes=[pltpu.VMEM((tm, d), MM_DTYPE)],
        compiler_params=_params(("arbitrary", "arbitrary")),
        name="conv_ffn_final" if final else "conv_ffn",
    )(*ins)


def _mla_mid_body(*refs, cfg, rope):
    if rope:
        (proj_ref, gq_ref, gkv_ref, wqn_ref, wqr_ref, wqs_ref, wuk_ref, wuv_ref, cos_ref, sin_ref,
         qn_ref, qr_ref, kn_ref, v_ref, krp_ref) = refs
    else:
        (proj_ref, gq_ref, gkv_ref, wqn_ref, wqr_ref, wuk_ref, wuv_ref,
         qn_ref, qr_ref, kn_ref, v_ref, krp_ref, ckv_ref, kr_ref) = refs
    ql, kl = cfg.q_lora, cfg.kv_lora
    p = proj_ref[...]
    cqn = _rms(p[:, :ql], gq_ref[...], cfg.eps).astype(MM_DTYPE)
    ckv = _rms(p[:, ql:ql + kl], gkv_ref[...], cfg.eps)
    krp = p[:, ql + kl:ql + kl + LANES]
    qn_ref[...] = _dot(cqn, wqn_ref[...]).astype(MM_DTYPE)
    qr = _dot(cqn, wqr_ref[...])
    if rope:
        krs = p[:, ql + kl + LANES:ql + kl + 2 * LANES]
        qs = _dot(cqn, wqs_ref[...])
        cos = cos_ref[...]
        sin = sin_ref[...]
        for h in range(cfg.n_heads):
            sl = slice(h * LANES, (h + 1) * LANES)
            qr_ref[:, sl] = (qr[:, sl] * cos + qs[:, sl] * sin).astype(MM_DTYPE)
        krp_ref[...] = (krp * cos + krs * sin).astype(MM_DTYPE)
    else:
        qr_ref[...] = qr.astype(MM_DTYPE)
        krp_ref[...] = krp.astype(MM_DTYPE)
        ckv_ref[...] = ckv
        kr_ref[...] = krp[:, :cfg.rope]
    ckv_b = ckv.astype(MM_DTYPE)
    kn_ref[...] = _dot(ckv_b, wuk_ref[...]).astype(MM_DTYPE)
    v_ref[...] = _dot(ckv_b, wuv_ref[...]).astype(MM_DTYPE)


def _mla_mid(proj, g_q, g_kv, w_qn, w_qr, w_qs, w_uk, w_uv, wl, cos, sin, *, cfg, seq, tm):
    n, pw = proj.shape
    rope = cos is not None
    hw = cfg.n_heads * LANES
    row = lambda i: (i, 0)
    ws = [g_q, g_kv, w_qn, w_qr] + ([w_qs] if rope else []) + [w_uk, w_uv]
    ins = [proj] + ws
    in_specs = [pl.BlockSpec((tm, pw), row)] + [_layer_spec(w, wl) for w in ws]
    out_shape = [jax.ShapeDtypeStruct((n, hw), MM_DTYPE)] * 4 + [jax.ShapeDtypeStruct((n, LANES), MM_DTYPE)]
    out_specs = [pl.BlockSpec((tm, hw), row)] * 4 + [pl.BlockSpec((tm, LANES), row)]
    if rope:
        bps = seq // tm
        ins += [cos, sin]
        in_specs += [pl.BlockSpec((tm, LANES), lambda i: (i % bps, 0))] * 2
    else:
        out_shape += [jax.ShapeDtypeStruct((n, cfg.kv_lora), F32), jax.ShapeDtypeStruct((n, cfg.rope), F32)]
        out_specs += [pl.BlockSpec((tm, cfg.kv_lora), row), pl.BlockSpec((tm, cfg.rope), row)]
    return pl.pallas_call(
        functools.partial(_mla_mid_body, cfg=cfg, rope=rope),
        out_shape=out_shape,
        grid=(n // tm,),
        in_specs=in_specs,
        out_specs=out_specs,
        compiler_params=_params(("arbitrary",)),
        name="mla_mid_rope" if rope else "mla_mid",
    )(*ins)


def _cache_expand_body(ckv_ref, kr_ref, wuk_ref, wuv_ref, kn_ref, v_ref, krp_ref, *, rope):
    c = ckv_ref[...].astype(MM_DTYPE)
    kn_ref[...] = _dot(c, wuk_ref[...]).astype(MM_DTYPE)
    v_ref[...] = _dot(c, wuv_ref[...]).astype(MM_DTYPE)
    krp_ref[:, :rope] = kr_ref[...].astype(MM_DTYPE)
    krp_ref[:, rope:] = jnp.zeros((krp_ref.shape[0], LANES - rope), MM_DTYPE)


def _cache_expand(cache_ckv, cache_krope, wl, w_uk, w_uv, *, cfg):
    b, _, p, kl = cache_ckv.shape
    hw = cfg.n_heads * LANES
    row = lambda i: (i, 0)
    return pl.pallas_call(
        functools.partial(_cache_expand_body, rope=cfg.rope),
        out_shape=[jax.ShapeDtypeStruct((b * p, hw), MM_DTYPE)] * 2 + [jax.ShapeDtypeStruct((b * p, LANES), MM_DTYPE)],
        grid=(b,),
        in_specs=[
            pl.BlockSpec((None, None, p, kl), lambda i: (i, wl, 0, 0)),
            pl.BlockSpec((None, None, p, cfg.rope), lambda i: (i, wl, 0, 0)),
            _layer_spec(w_uk, wl),
            _layer_spec(w_uv, wl),
        ],
        out_specs=[pl.BlockSpec((p, hw), row)] * 2 + [pl.BlockSpec((p, LANES), row)],
        compiler_params=_params(("arbitrary",)),
        name="cache_expand",
    )(cache_ckv, cache_krope, w_uk, w_uv)


def _attn_body(*refs, hb, scale, cached):
    if cached:
        qn_ref, qr_ref, kn_ref, kr_ref, v_ref, knc_ref, krc_ref, vc_ref, o_ref = refs
    else:
        qn_ref, qr_ref, kn_ref, kr_ref, v_ref, o_ref = refs
    for h in range(hb):
        sl = slice(h * LANES, (h + 1) * LANES)
        q = jnp.concatenate([qn_ref[:, sl], qr_ref[:, sl]], axis=1)
        s = _dot_nt(q, jnp.concatenate([kn_ref[:, sl], kr_ref[...]], axis=1)) * scale
        m = jnp.max(s, axis=-1, keepdims=True)
        if cached:
            sc = _dot_nt(q, jnp.concatenate([knc_ref[:, sl], krc_ref[...]], axis=1)) * scale
            m = jnp.maximum(m, jnp.max(sc, axis=-1, keepdims=True))
            pc = jnp.exp(sc - m)
        p = jnp.exp(s - m)
        den = jnp.sum(p, axis=-1, keepdims=True)
        o = _dot(p.astype(MM_DTYPE), v_ref[:, sl])
        if cached:
            den = den + jnp.sum(pc, axis=-1, keepdims=True)
            o = o + _dot(pc.astype(MM_DTYPE), vc_ref[:, sl])
        o_ref[:, sl] = (o / den).astype(MM_DTYPE)


def _attention(qn, qr, kn, krp, v, cache, *, cfg, seq, hb):
    n, hw = qn.shape
    cached = cache is not None
    blk = lambda rows: pl.BlockSpec((rows, hb * LANES), lambda b, g: (b, g))
    shared = lambda rows: pl.BlockSpec((rows, LANES), lambda b, g: (b, 0))
    ins = [qn, qr, kn, krp, v]
    in_specs = [blk(seq), blk(seq), blk(seq), shared(seq), blk(seq)]
    if cached:
        knc, vc, krc = cache
        past = knc.shape[0] // (n // seq)
        ins += [knc, krc, vc]
        in_specs += [blk(past), shared(past), blk(past)]
    return pl.pallas_call(
        functools.partial(_attn_body, hb=hb, scale=float(cfg.nope + cfg.rope) ** -0.5, cached=cached),
        out_shape=jax.ShapeDtypeStruct((n, hw), MM_DTYPE),
        grid=(n // seq, cfg.n_heads // hb),
        in_specs=in_specs,
        out_specs=blk(seq),
        compiler_params=_params(("arbitrary", "arbitrary")),
        name="attention_cached" if cached else "attention",
    )(*ins)


def _rec_scan_body(*refs, seq, cfg, has_state):
    if has_state:
        (y_ref, xb_ref, cw_ref, cb_ref, wg_ref, bg_ref, lam_ref, h0f_ref, h0b_ref,
         o_ref, af_ref, uf_ref, ab_ref, ub_ref) = refs
    else:
        (y_ref, xb_ref, cw_ref, cb_ref, wg_ref, bg_ref, lam_ref,
         o_ref, sf_ref, sb_ref, af_ref, uf_ref, ab_ref, ub_ref) = refs
    nseq = SUBLANES
    pitch = seq + 4
    t_idx = lax.broadcasted_iota(jnp.int32, (seq, 1), 0)
    lam = lam_ref[...]
    softplus = jnp.maximum(-lam, 0.0) + jnp.log1p(jnp.exp(-jnp.abs(lam)))
    decay = (-cfg.lru_c * math.log2(math.e)) * softplus
    wg = wg_ref[...]
    bg = bg_ref[...]
    for b in range(nseq):
        x = xb_ref[b * seq:(b + 1) * seq, :]
        xc = cb_ref[...] + jnp.where(t_idx < 2, 0.0, pltpu.roll(x, 2, 0)) * cw_ref[0:1, :]
        xc = xc + jnp.where(t_idx < 1, 0.0, pltpu.roll(x, 1, 0)) * cw_ref[1:2, :]
        xc = xc + x * cw_ref[2:3, :]
        xc = xc + jnp.where(t_idx > seq - 2, 0.0, pltpu.roll(x, seq - 1, 0)) * cw_ref[3:4, :]
        gates = jax.nn.sigmoid(_dot(xc.astype(MM_DTYPE), wg) + bg)
        for d, (a_ref, u_ref) in enumerate(((af_ref, uf_ref), (ab_ref, ub_ref))):
            gx = gates[:, (2 * d) * LANES:(2 * d + 1) * LANES]
            ga = gates[:, (2 * d + 1) * LANES:(2 * d + 2) * LANES]
            a = jnp.exp2(ga * decay[d:d + 1, :])
            z = 1.0 - a * a
            u = jnp.where(z > 0.0, z * lax.rsqrt(z), 0.0) * (gx * xc)
            a_ref[b * pitch:b * pitch + seq, :] = a
            u_ref[b * pitch:b * pitch + seq, :] = u

    if has_state:
        hf0 = h0f_ref[...]
        hb0 = h0b_ref[...]
    else:
        hf0 = jnp.zeros((nseq, LANES), F32)
        hb0 = hf0

    def step(t, carry):
        hf, hb = carry
        fwd = pl.ds(t, nseq, stride=pitch)
        hf = af_ref[fwd, :] * hf + uf_ref[fwd, :]
        uf_ref[fwd, :] = hf
        bwd = pl.ds(seq - 1 - t, nseq, stride=pitch)
        hb = ab_ref[bwd, :] * hb + ub_ref[bwd, :]
        ub_ref[bwd, :] = hb
        return hf, hb

    hf, hb = lax.fori_loop(0, seq, step, (hf0, hb0), unroll=8)
    if not has_state:
        sf_ref[...] = hf
        sb_ref[...] = hb
    for b in range(nseq):
        hsum = uf_ref[b * pitch:b * pitch + seq, :] + ub_ref[b * pitch:b * pitch + seq, :]
        y = y_ref[b * seq:(b + 1) * seq, :]
        o_ref[b * seq:(b + 1) * seq, :] = (hsum * jax.nn.gelu(y)).astype(MM_DTYPE)


def _rec_scan(proj, w_conv, b_conv, w_gates, b_gates, lam, wl, state, *, cfg, seq):
    n, w2 = proj.shape
    w = w2 // 2
    nb = w // LANES
    assert nb == cfg.lru_blocks and w_conv.shape[1] == 4
    rows = SUBLANES * seq
    has_state = state is not None
    col = lambda g, c: (g, c)
    ins = [proj, proj, w_conv, b_conv, w_gates, b_gates, lam]
    in_specs = [
        pl.BlockSpec((rows, LANES), col),
        pl.BlockSpec((rows, LANES), lambda g, c: (g, c + nb)),
        pl.BlockSpec((None, 4, LANES), lambda g, c: (wl, 0, c)),
        pl.BlockSpec((None, 1, LANES), lambda g, c: (wl, 0, c)),
        pl.BlockSpec((None, None, LANES, 4 * LANES), lambda g, c: (wl, c, 0, 0)),
        pl.BlockSpec((None, None, 1, 4 * LANES), lambda g, c: (wl, c, 0, 0)),
        pl.BlockSpec((None, 2, LANES), lambda g, c: (wl, 0, c)),
    ]
    out_shape = [jax.ShapeDtypeStruct((n, w), MM_DTYPE)]
    out_specs = [pl.BlockSpec((rows, LANES), col)]
    if has_state:
        ins += list(state)
        in_specs += [pl.BlockSpec((SUBLANES, LANES), col)] * 2
    else:
        nstate = n // seq
        out_shape += [jax.ShapeDtypeStruct((nstate, w), F32)] * 2
        out_specs += [pl.BlockSpec((SUBLANES, LANES), col)] * 2
    scratch = [pltpu.VMEM((SUBLANES * (seq + 4), LANES), F32)] * 4
    return pl.pallas_call(
        functools.partial(_rec_scan_body, seq=seq, cfg=cfg, has_state=has_state),
        out_shape=out_shape,
        grid=(n // rows, nb),
        in_specs=in_specs,
        out_specs=out_specs,
        scratch_shapes=scratch,
        compiler_params=_params(("arbitrary", "arbitrary")),
        name="rec_scan_state" if has_state else "rec_scan",
    )(*ins)


def _swap_halves(w, rope):
    q = rope // 4
    return jnp.concatenate([w[..., q:2 * q], w[..., :q], w[..., 3 * q:], w[..., 2 * q:3 * q]], axis=-1)


def _pad_lanes(w):
    return jnp.pad(w, [(0, 0)] * (w.ndim - 1) + [(0, LANES - w.shape[-1])])


def _rope_tables(seq, cfg):
    rows = seq // cfg.grid_w
    row = jnp.repeat(jnp.arange(rows), cfg.grid_w).astype(F32)
    col = jnp.tile(jnp.arange(cfg.grid_w), rows).astype(F32)
    half = cfg.rope // 2
    inv = 1.0 / (cfg.rope_base ** (jnp.arange(0, half, 2, dtype=F32) / half))
    ar, ac = row[:, None] * inv, col[:, None] * inv
    cos = jnp.concatenate([jnp.cos(ar), jnp.cos(ar), jnp.cos(ac), jnp.cos(ac)], axis=-1)
    sin = jnp.concatenate([-jnp.sin(ar), jnp.sin(ar), -jnp.sin(ac), jnp.sin(ac)], axis=-1)
    return _pad_lanes(cos), _pad_lanes(sin)


def _pick(n, pref):
    return pref if n % pref == 0 else n


def _forward(cfg, x_prompt, x_sample, cache_ckv, cache_krope, state_lru, c, c_ctx,
             g_mix, g_ffn, g_final, w_ada, b_ada,
             w_mla_in, g_mla_q, g_mla_kv, w_mla_uq, w_mla_uk, w_mla_uv, w_mla_o,
             w_rec_in, w_rec_conv, b_rec_conv, w_rec_gx, b_rec_gx, w_rec_ga, b_rec_ga,
             rec_lambda, w_rec_out,
             w_ffn_up, w_ffn_conv, b_ffn_conv, w_ffn_down):
    depth, d = g_mix.shape
    bp, sp, _ = x_prompt.shape
    bs, ss, _ = x_sample.shape
    assert cfg.nope == LANES and cfg.v_dim == LANES and cfg.rope <= LANES and bs + 1 <= MOD_ROWS
    assert bp % SUBLANES == 0 and bs % SUBLANES == 0
    eps = cfg.eps
    cast = lambda w: w.astype(MM_DTYPE)
    row3 = lambda v: v[:, None, :]

    cond = jnp.zeros((MOD_ROWS, d), F32).at[0].set(c_ctx).at[1:1 + bs].set(c)
    mod = _adaln(cond, w_ada, b_ada, _pick(w_ada.shape[-1], 1024)).reshape(depth, MOD_ROWS, N_MOD, d)

    ql, kl, r, nh = cfg.q_lora, cfg.kv_lora, cfg.rope, cfg.n_heads
    w_kr = w_mla_in[..., ql + kl:]
    w_in_p = cast(jnp.concatenate(
        [w_mla_in[..., :ql + kl], _pad_lanes(w_kr), _pad_lanes(_swap_halves(w_kr, r))], axis=-1))
    uq = w_mla_uq.reshape(w_mla_uq.shape[0], ql, nh, cfg.nope + r)
    flat = lambda w: cast(w.reshape(w.shape[0], ql, nh * LANES))
    w_qn, w_qr, w_qs = flat(uq[..., :cfg.nope]), flat(_pad_lanes(uq[..., cfg.nope:])), flat(
        _pad_lanes(_swap_halves(uq[..., cfg.nope:], r)))
    w_uk, w_uv, w_o = cast(w_mla_uk), cast(w_mla_uv), cast(w_mla_o)
    g_q, g_kv = row3(g_mla_q), row3(g_mla_kv)
    cos, sin = _rope_tables(ss, cfg)

    w_gates = cast(jnp.concatenate([w_rec_gx[:, 0], w_rec_ga[:, 0], w_rec_gx[:, 1], w_rec_ga[:, 1]], axis=-1))
    nb = cfg.lru_blocks
    blk = lambda b: b.reshape(b.shape[0], nb, 1, LANES)
    b_gates = jnp.concatenate([blk(b_rec_gx[:, 0]), blk(b_rec_ga[:, 0]), blk(b_rec_gx[:, 1]), blk(b_rec_ga[:, 1])],
                              axis=-1)
    w_rin, w_rout = cast(w_rec_in), cast(w_rec_out)
    w_up, w_down = cast(w_ffn_up), cast(w_ffn_down)
    g_mix3, g_ffn3, b_rconv, b_fconv = row3(g_mix), row3(g_ffn), row3(b_rec_conv), row3(b_ffn_conv)

    ctx_row = lambda i, tm: 0
    tf = _pick(w_down.shape[1], 512)

    def run_group(x, seq, mod_row, sample):
        n = x.shape[0]
        rows = lambda pref: min(seq, pref) if sample else _pick(n, pref)
        tm_ffn = seq if sample else max(seq, _pick(n, 1024))
        ckv_new, kr_new, lru_new = [], [], []
        for layer in range(depth):
            j = layer // 2
            if layer % 2 == 0:
                proj = _norm_mod_matmul(x, mod, mod_row, g_mix3, w_in_p, layer, j, tm=rows(1024),
                                        tn=w_in_p.shape[-1], eps=eps)
                if sample:
                    qn, qr, kn, v, krp = _mla_mid(proj, g_q, g_kv, w_qn, w_qr, w_qs, w_uk, w_uv, j, cos, sin,
                                                  cfg=cfg, seq=seq, tm=rows(256))
                    cache = _cache_expand(cache_ckv, cache_krope, j, w_uk, w_uv, cfg=cfg)
                    att = _attention(qn, qr, kn, krp, v, cache, cfg=cfg, seq=seq, hb=min(2, nh))
                else:
                    qn, qr, kn, v, krp, ckv, kr = _mla_mid(proj, g_q, g_kv, w_qn, w_qr, None, w_uk, w_uv, j,
                                                           None, None, cfg=cfg, seq=seq, tm=rows(256))
                    ckv_new.append(ckv)
                    kr_new.append(kr)
                    att = _attention(qn, qr, kn, krp, v, None, cfg=cfg, seq=seq, hb=nh)
                x = _matmul_residual(att, w_o, j, x, mod, layer, mod_row, tm=rows(512))
            else:
                proj = _norm_mod_matmul(x, mod, mod_row, g_mix3, w_rin, layer, j, tm=rows(1024),
                                        tn=_pick(w_rin.shape[-1], 1024), eps=eps)
                outs = _rec_scan(proj, w_rec_conv, b_rconv, w_gates, b_gates, rec_lambda, j,
                                 (state_lru[:, j, 0], state_lru[:, j, 1]) if sample else None, cfg=cfg, seq=seq)
                if not sample:
                    lru_new.append(jnp.stack(outs[1:], axis=1))
                x = _matmul_residual(outs[0], w_rout, j, x, mod, layer, mod_row, tm=rows(512))
            x = _conv_ffn(x, mod, layer, mod_row, g_ffn3, w_up, w_ffn_conv, b_fconv, w_down,
                          g_final[None] if layer == depth - 1 else None, seq=seq, tm=tm_ffn, tf=tf,
                          n_sub=2 if tf % (2 * LANES) == 0 else 1, x_buffers=1 if tm_ffn > 512 else 2, eps=eps)
        return x, ckv_new, kr_new, lru_new

    y_p, ckv_new, kr_new, lru_new = run_group(x_prompt.reshape(bp * sp, d), sp, ctx_row, False)
    y_s, _, _, _ = run_group(x_sample.reshape(bs * ss, d), ss, lambda i, tm: 1 + (i * tm) // ss, True)
    new_ckv = jnp.stack([t.reshape(bp, sp, kl) for t in ckv_new], axis=1)
    new_kr = jnp.stack([t.reshape(bp, sp, r) for t in kr_new], axis=1)
    new_lru = jnp.stack(lru_new, axis=1)
    return y_p.reshape(bp, sp, d), y_s.reshape(bs, ss, d), new_ckv, new_kr, new_lru


CFG = Cfg(n_heads=16, q_lora=512, kv_lora=512, nope=128, rope=64, v_dim=128, grid_w=64, rope_base=10000.0,
          lru_blocks=16, lru_c=8.0, eps=1e-6)


def kernel(x_prompt, x_sample, cache_ckv, cache_krope, state_lru, c, c_ctx, g_mix, g_ffn, g_final, w_ada, b_ada, w_mla_in, g_mla_q, g_mla_kv, w_mla_uq, w_mla_uk, w_mla_uv, w_mla_o, w_rec_in, w_rec_conv, b_rec_conv, w_rec_gx, b_rec_gx, w_rec_ga, b_rec_ga, rec_lambda, w_rec_out, w_ffn_up, w_ffn_conv, b_ffn_conv, w_ffn_down):
    return _forward(CFG, x_prompt, x_sample, cache_ckv, cache_krope, state_lru, c, c_ctx, g_mix, g_ffn, g_final,
                    w_ada, b_ada, w_mla_in, g_mla_q, g_mla_kv, w_mla_uq, w_mla_uk, w_mla_uv, w_mla_o,
                    w_rec_in, w_rec_conv, b_rec_conv, w_rec_gx, b_rec_gx, w_rec_ga, b_rec_ga, rec_lambda, w_rec_out,
                    w_ffn_up, w_ffn_conv, b_ffn_conv, w_ffn_down)
```

```python
import functools
import math
from typing import NamedTuple

import jax
import jax.numpy as jnp
from jax import lax
from jax.experimental import pallas as pl
from jax.experimental.pallas import tpu as pltpu

F32 = jnp.float32
MM_DTYPE = jnp.bfloat16

LANES = 128
SUBLANES = 8
VMEM_LIMIT_BYTES = 60 * 1024 * 1024
N_MOD = 6
MOD_ROWS = 16


class Cfg(NamedTuple):
    n_heads: int
    q_lora: int
    kv_lora: int
    nope: int
    rope: int
    v_dim: int
    grid_w: int
    rope_base: float
    lru_blocks: int
    lru_c: float
    eps: float


def _params(sem):
    return pltpu.CompilerParams(dimension_semantics=sem, vmem_limit_bytes=VMEM_LIMIT_BYTES)


def _dot(a, b):
    return jnp.dot(a, b, preferred_element_type=F32)


def _dot_nt(a, b):
    return lax.dot_general(a, b, (((1,), (1,)), ((), ())), preferred_element_type=F32)


def _const_spec(shape):
    nd = len(shape)
    return pl.BlockSpec(shape, lambda *_: (0,) * nd)


def _layer_spec(arr, layer):
    nd = arr.ndim
    return pl.BlockSpec((None,) + arr.shape[1:], lambda *_: (layer,) + (0,) * (nd - 1))


def _mod_spec(d, layer, mod_row, tm):
    return pl.BlockSpec((None, None, N_MOD, d), lambda i, *_: (layer, mod_row(i, tm), 0, 0))


def _rms(x, g, eps):
    return x * lax.rsqrt(jnp.mean(x * x, axis=-1, keepdims=True) + eps) * g


def _norm_mod(x, g, mod_ref, shift_idx, scale_idx, eps):
    y = _rms(x, g, eps)
    return y * (1.0 + mod_ref[scale_idx:scale_idx + 1, :]) + mod_ref[shift_idx:shift_idx + 1, :]


def _adaln_body(cond_ref, w_ref, b_ref, o_ref):
    c = cond_ref[...]
    s = (c * jax.nn.sigmoid(c)).astype(MM_DTYPE)
    o_ref[...] = _dot(s, w_ref[...].astype(MM_DTYPE)) + b_ref[...]


def _adaln(cond, w_ada, b_ada, tn):
    n_layers, d, n_out = w_ada.shape
    return pl.pallas_call(
        _adaln_body,
        out_shape=jax.ShapeDtypeStruct((n_layers, MOD_ROWS, n_out), F32),
        grid=(n_layers, n_out // tn),
        in_specs=[
            _const_spec((MOD_ROWS, d)),
            pl.BlockSpec((None, d, tn), lambda l, n: (l, 0, n)),
            pl.BlockSpec((None, 1, tn), lambda l, n: (l, 0, n)),
        ],
        out_specs=pl.BlockSpec((None, MOD_ROWS, tn), lambda l, n: (l, 0, n)),
        compiler_params=_params(("arbitrary", "arbitrary")),
        name="adaln",
    )(cond, w_ada, b_ada.reshape(n_layers, 1, n_out))


def _nmm_body(x_ref, mod_ref, g_ref, w_ref, o_ref, h_ref, *, eps):
    @pl.when(pl.program_id(1) == 0)
    def _():
        h = _norm_mod(x_ref[...], g_ref[...], mod_ref, 0, 1, eps)
        h_ref[...] = h.astype(MM_DTYPE)

    o_ref[...] = _dot(h_ref[...], w_ref[...])


def _norm_mod_matmul(x, mod, mod_row, g, w, layer, wl, *, tm, tn, eps):
    n, d = x.shape
    m = w.shape[2]
    return pl.pallas_call(
        functools.partial(_nmm_body, eps=eps),
        out_shape=jax.ShapeDtypeStruct((n, m), F32),
        grid=(n // tm, m // tn),
        in_specs=[
            pl.BlockSpec((tm, d), lambda i, j: (i, 0)),
            _mod_spec(d, layer, mod_row, tm),
            _layer_spec(g, layer),
            pl.BlockSpec((None, d, tn), lambda i, j: (wl, 0, j)),
        ],
        out_specs=pl.BlockSpec((tm, tn), lambda i, j: (i, j)),
        scratch_shapes=[pltpu.VMEM((tm, d), MM_DTYPE)],
        compiler_params=_params(("arbitrary", "arbitrary")),
        name="norm_mod_matmul",
    )(x, mod, g, w)


def _rec_in_body(x_ref, mod_ref, g_ref, w_ref, cw_ref, cb_ref, o_ref, h_ref, *, seq, n_y, eps):
    j = pl.program_id(1)

    @pl.when(j == 0)
    def _():
        h = _norm_mod(x_ref[...], g_ref[...], mod_ref, 0, 1, eps)
        h_ref[...] = h.astype(MM_DTYPE)

    @pl.when(j < n_y)
    def _():
        o_ref[...] = jax.nn.gelu(_dot(h_ref[...], w_ref[...]))

    @pl.when(j >= n_y)
    def _():
        x = _dot(h_ref[...], w_ref[...])
        tm = x.shape[0]
        t = lax.broadcasted_iota(jnp.int32, (tm, 1), 0) % seq
        xc = cb_ref[...] + jnp.where(t < 2, 0.0, pltpu.roll(x, 2, 0)) * cw_ref[0:1, :]
        xc = xc + jnp.where(t < 1, 0.0, pltpu.roll(x, 1, 0)) * cw_ref[1:2, :]
        xc = xc + x * cw_ref[2:3, :]
        xc = xc + jnp.where(t > seq - 2, 0.0, pltpu.roll(x, tm - 1, 0)) * cw_ref[3:4, :]
        o_ref[...] = xc


def _rec_in(x, mod, mod_row, g, w, w_conv, b_conv, layer, wl, *, seq, tm, tn, eps):
    n, d = x.shape
    m = w.shape[2]
    n_y = m // 2 // tn
    assert tm % seq == 0 and w_conv.shape[1] == 4 and m % (2 * tn) == 0
    conv_col = lambda i, j: (wl, 0, jnp.maximum(j - n_y, 0))
    return pl.pallas_call(
        functools.partial(_rec_in_body, seq=seq, n_y=n_y, eps=eps),
        out_shape=jax.ShapeDtypeStruct((n, m), F32),
        grid=(n // tm, m // tn),
        in_specs=[
            pl.BlockSpec((tm, d), lambda i, j: (i, 0)),
            _mod_spec(d, layer, mod_row, tm),
            _layer_spec(g, layer),
            pl.BlockSpec((None, d, tn), lambda i, j: (wl, 0, j)),
            pl.BlockSpec((None, 4, tn), conv_col),
            pl.BlockSpec((None, 1, tn), conv_col),
        ],
        out_specs=pl.BlockSpec((tm, tn), lambda i, j: (i, j)),
        scratch_shapes=[pltpu.VMEM((tm, d), MM_DTYPE)],
        compiler_params=_params(("arbitrary", "arbitrary")),
        name="rec_in",
    )(x, mod, g, w, w_conv, b_conv)


def _mmres_body(a_ref, w_ref, x_ref, mod_ref, o_ref):
    o_ref[...] = x_ref[...] + mod_ref[2:3, :] * _dot(a_ref[...], w_ref[...])


def _matmul_residual(a, w, wl, x, mod, layer, mod_row, *, tm):
    n, k = a.shape
    d = w.shape[2]
    return pl.pallas_call(
        _mmres_body,
        out_shape=jax.ShapeDtypeStruct((n, d), F32),
        grid=(n // tm,),
        in_specs=[
            pl.BlockSpec((tm, k), lambda i: (i, 0)),
            _layer_spec(w, wl),
            pl.BlockSpec((tm, d), lambda i: (i, 0)),
            _mod_spec(d, layer, mod_row, tm),
        ],
        out_specs=pl.BlockSpec((tm, d), lambda i: (i, 0)),
        compiler_params=_params(("arbitrary",)),
        name="matmul_residual",
    )(a, w, x, mod)


def _ffn_body(*refs, seq, eps, n_sub, final):
    if final:
        (x_ref, mod_ref, g_ref, wg_ref, wv_ref, cwg_ref, cwv_ref, cbg_ref, cbv_ref, wd_ref, gf_ref,
         o_ref, h_ref) = refs
    else:
        (x_ref, mod_ref, g_ref, wg_ref, wv_ref, cwg_ref, cwv_ref, cbg_ref, cbv_ref, wd_ref,
         o_ref, h_ref) = refs
    j = pl.program_id(1)

    @pl.when(j == 0)
    def _():
        h = _norm_mod(x_ref[...], g_ref[...], mod_ref, 3, 4, eps)
        h_ref[...] = h.astype(MM_DTYPE)
        o_ref[...] = jnp.zeros_like(o_ref)

    h = h_ref[...]
    tm = h.shape[0]
    row = lax.broadcasted_iota(jnp.int32, (tm, 1), 0) % seq
    first = row == 0
    last = row == seq - 1

    def conv(u, cw, cb):
        prev = jnp.where(first, 0.0, pltpu.roll(u, 1, 0))
        nxt = jnp.where(last, 0.0, pltpu.roll(u, tm - 1, 0))
        return cb + prev * cw[0:1, :] + u * cw[1:2, :] + nxt * cw[2:3, :]

    ts = wg_ref.shape[1] // n_sub
    subs = [slice(s * ts, (s + 1) * ts) for s in range(n_sub)]
    ups = [(_dot(h, wg_ref[:, sl]), _dot(h, wv_ref[:, sl])) for sl in subs]
    acts = []
    for sl, (ug, uv) in zip(subs, ups):
        gate = conv(ug, cwg_ref[:, sl], cbg_ref[:, sl])
        val = conv(uv, cwv_ref[:, sl], cbv_ref[:, sl])
        acts.append((gate * jax.nn.sigmoid(gate) * val).astype(MM_DTYPE))
    for sl, act in zip(subs, acts):
        o_ref[...] += _dot(act, wd_ref[sl, :])

    @pl.when(j == pl.num_programs(1) - 1)
    def _():
        y = x_ref[...] + mod_ref[5:6, :] * o_ref[...]
        if final:
            y = _rms(y, gf_ref[...], eps)
        o_ref[...] = y


def _conv_ffn(x, mod, layer, mod_row, g, w_up, w_conv, b_conv, w_down, g_final, *, seq, tm, tf, n_sub, x_buffers,
              eps):
    n, d = x.shape
    f = w_down.shape[1]
    assert w_conv.shape[1:] == (3, 2 * f) and tm % seq == 0 and f % tf == 0 and tf % (n_sub * LANES) == 0
    nf = f // tf
    final = g_final is not None
    lo = lambda rows: pl.BlockSpec((None, rows, tf), lambda i, j: (layer, 0, j))
    hi = lambda rows: pl.BlockSpec((None, rows, tf), lambda i, j: (layer, 0, j + nf))
    ins = [x, mod, g, w_up, w_up, w_conv, w_conv, b_conv, b_conv, w_down] + ([g_final] if final else [])
    in_specs = [
        pl.BlockSpec((tm, d), lambda i, j: (i, 0), pipeline_mode=pl.Buffered(x_buffers)),
        _mod_spec(d, layer, mod_row, tm),
        _layer_spec(g, layer),
        lo(d), hi(d), lo(3), hi(3), lo(1), hi(1),
        pl.BlockSpec((None, tf, d), lambda i, j: (layer, j, 0)),
    ] + ([_const_spec(g_final.shape)] if final else [])
    return pl.pallas_call(
        functools.partial(_ffn_body, seq=seq, eps=eps, n_sub=n_sub, final=final),
        out_shape=jax.ShapeDtypeStruct((n, d), F32),
        grid=(n // tm, nf),
        in_specs=in_specs,
        out_specs=pl.BlockSpec((tm, d), lambda i, j: (i, 0)),
        scratch_shapes=[pltpu.VMEM((tm, d), MM_DTYPE)],
        compiler_params=_params(("arbitrary", "arbitrary")),
        name="conv_ffn_final" if final else "conv_ffn",
    )(*ins)


def _mla_mid_body(*refs, cfg, rope):
    if rope:
        (proj_ref, gq_ref, gkv_ref, wqn_ref, wqr_ref, wqs_ref, wuk_ref, wuv_ref, cos_ref, sin_ref,
         qn_ref, qr_ref, kn_ref, v_ref, krp_ref) = refs
    else:
        (proj_ref, gq_ref, gkv_ref, wqn_ref, wqr_ref, wuk_ref, wuv_ref,
         qn_ref, qr_ref, kn_ref, v_ref, krp_ref, ckv_ref, kr_ref) = refs
    ql, kl = cfg.q_lora, cfg.kv_lora
    p = proj_ref[...]
    cqn = _rms(p[:, :ql], gq_ref[...], cfg.eps).astype(MM_DTYPE)
    ckv = _rms(p[:, ql:ql + kl], gkv_ref[...], cfg.eps)
    krp = p[:, ql + kl:ql + kl + LANES]
    qn_ref[...] = _dot(cqn, wqn_ref[...]).astype(MM_DTYPE)
    qr = _dot(cqn, wqr_ref[...])
    if rope:
        krs = p[:, ql + kl + LANES:ql + kl + 2 * LANES]
        qs = _dot(cqn, wqs_ref[...])
        cos = cos_ref[...]
        sin = sin_ref[...]
        for h in range(cfg.n_heads):
            sl = slice(h * LANES, (h + 1) * LANES)
            qr_ref[:, sl] = (qr[:, sl] * cos + qs[:, sl] * sin).astype(MM_DTYPE)
        krp_ref[...] = (krp * cos + krs * sin).astype(MM_DTYPE)
    else:
        qr_ref[...] = qr.astype(MM_DTYPE)
        krp_ref[...] = krp.astype(MM_DTYPE)
        ckv_ref[...] = ckv
        kr_ref[...] = krp[:, :cfg.rope]
    ckv_b = ckv.astype(MM_DTYPE)
    kn_ref[...] = _dot(ckv_b, wuk_ref[...]).astype(MM_DTYPE)
    v_ref[...] = _dot(ckv_b, wuv_ref[...]).astype(MM_DTYPE)


def _mla_mid(proj, g_q, g_kv, w_qn, w_qr, w_qs, w_uk, w_uv, wl, cos, sin, *, cfg, seq, tm):
    n, pw = proj.shape
    rope = cos is not None
    hw = cfg.n_heads * LANES
    row = lambda i: (i, 0)
    ws = [g_q, g_kv, w_qn, w_qr] + ([w_qs] if rope else []) + [w_uk, w_uv]
    ins = [proj] + ws
    in_specs = [pl.BlockSpec((tm, pw), row)] + [_layer_spec(w, wl) for w in ws]
    out_shape = [jax.ShapeDtypeStruct((n, hw), MM_DTYPE)] * 4 + [jax.ShapeDtypeStruct((n, LANES), MM_DTYPE)]
    out_specs = [pl.BlockSpec((tm, hw), row)] * 4 + [pl.BlockSpec((tm, LANES), row)]
    if rope:
        bps = seq // tm
        ins += [cos, sin]
        in_specs += [pl.BlockSpec((tm, LANES), lambda i: (i % bps, 0))] * 2
    else:
        out_shape += [jax.ShapeDtypeStruct((n, cfg.kv_lora), F32), jax.ShapeDtypeStruct((n, cfg.rope), F32)]
        out_specs += [pl.BlockSpec((tm, cfg.kv_lora), row), pl.BlockSpec((tm, cfg.rope), row)]
    return pl.pallas_call(
        functools.partial(_mla_mid_body, cfg=cfg, rope=rope),
        out_shape=out_shape,
        grid=(n // tm,),
        in_specs=in_specs,
        out_specs=out_specs,
        compiler_params=_params(("arbitrary",)),
        name="mla_mid_rope" if rope else "mla_mid",
    )(*ins)


def _cache_expand_body(ckv_ref, kr_ref, wuk_ref, wuv_ref, kn_ref, v_ref, krp_ref, *, rope):
    c = ckv_ref[...].astype(MM_DTYPE)
    kn_ref[...] = _dot(c, wuk_ref[...]).astype(MM_DTYPE)
    v_ref[...] = _dot(c, wuv_ref[...]).astype(MM_DTYPE)
    krp_ref[:, :rope] = kr_ref[...].astype(MM_DTYPE)
    krp_ref[:, rope:] = jnp.zeros((krp_ref.shape[0], LANES - rope), MM_DTYPE)


def _cache_expand(cache_ckv, cache_krope, wl, w_uk, w_uv, *, cfg):
    b, _, p, kl = cache_ckv.shape
    hw = cfg.n_heads * LANES
    row = lambda i: (i, 0)
    return pl.pallas_call(
        functools.partial(_cache_expand_body, rope=cfg.rope),
        out_shape=[jax.ShapeDtypeStruct((b * p, hw), MM_DTYPE)] * 2 + [jax.ShapeDtypeStruct((b * p, LANES), MM_DTYPE)],
        grid=(b,),
        in_specs=[
            pl.BlockSpec((None, None, p, kl), lambda i: (i, wl, 0, 0)),
            pl.BlockSpec((None, None, p, cfg.rope), lambda i: (i, wl, 0, 0)),
            _layer_spec(w_uk, wl),
            _layer_spec(w_uv, wl),
        ],
        out_specs=[pl.BlockSpec((p, hw), row)] * 2 + [pl.BlockSpec((p, LANES), row)],
        compiler_params=_params(("arbitrary",)),
        name="cache_expand",
    )(cache_ckv, cache_krope, w_uk, w_uv)


def _attn_body(*refs, hb, scale, cached):
    if cached:
        qn_ref, qr_ref, kn_ref, kr_ref, v_ref, knc_ref, krc_ref, vc_ref, o_ref = refs
    else:
        qn_ref, qr_ref, kn_ref, kr_ref, v_ref, o_ref = refs
    for h in range(hb):
        sl = slice(h * LANES, (h + 1) * LANES)
        q = jnp.concatenate([qn_ref[:, sl], qr_ref[:, sl]], axis=1)
        s = _dot_nt(q, jnp.concatenate([kn_ref[:, sl], kr_ref[...]], axis=1)) * scale
        m = jnp.max(s, axis=-1, keepdims=True)
        if cached:
            sc = _dot_nt(q, jnp.concatenate([knc_ref[:, sl], krc_ref[...]], axis=1)) * scale
            m = jnp.maximum(m, jnp.max(sc, axis=-1, keepdims=True))
            pc = jnp.exp(sc - m)
        p = jnp.exp(s - m)
        den = jnp.sum(p, axis=-1, keepdims=True)
        o = _dot(p.astype(MM_DTYPE), v_ref[:, sl])
        if cached:
            den = den + jnp.sum(pc, axis=-1, keepdims=True)
            o = o + _dot(pc.astype(MM_DTYPE), vc_ref[:, sl])
        o_ref[:, sl] = (o / den).astype(MM_DTYPE)


def _attention(qn, qr, kn, krp, v, cache, *, cfg, seq, hb):
    n, hw = qn.shape
    cached = cache is not None
    blk = lambda rows: pl.BlockSpec((rows, hb * LANES), lambda b, g: (b, g))
    shared = lambda rows: pl.BlockSpec((rows, LANES), lambda b, g: (b, 0))
    ins = [qn, qr, kn, krp, v]
    in_specs = [blk(seq), blk(seq), blk(seq), shared(seq), blk(seq)]
    if cached:
        knc, vc, krc = cache
        past = knc.shape[0] // (n // seq)
        ins += [knc, krc, vc]
        in_specs += [blk(past), shared(past), blk(past)]
    return pl.pallas_call(
        functools.partial(_attn_body, hb=hb, scale=float(cfg.nope + cfg.rope) ** -0.5, cached=cached),
        out_shape=jax.ShapeDtypeStruct((n, hw), MM_DTYPE),
        grid=(n // seq, cfg.n_heads // hb),
        in_specs=in_specs,
        out_specs=blk(seq),
        compiler_params=_params(("arbitrary", "arbitrary")),
        name="attention_cached" if cached else "attention",
    )(*ins)


def _rec_scan_body(*refs, seq, cfg, has_state):
    if has_state:
        (yg_ref, xc_ref, wg_ref, bg_ref, lam_ref, h0f_ref, h0b_ref,
         o_ref, af_ref, uf_ref, ab_ref, ub_ref, hf_ref, hb_ref) = refs
    else:
        (yg_ref, xc_ref, wg_ref, bg_ref, lam_ref,
         o_ref, sf_ref, sb_ref, af_ref, uf_ref, ab_ref, ub_ref, hf_ref, hb_ref) = refs
    nseq = SUBLANES
    pitch = seq + 4
    lam = lam_ref[...]
    softplus = jnp.maximum(-lam, 0.0) + jnp.log1p(jnp.exp(-jnp.abs(lam)))
    half_decay = (-0.5 * cfg.lru_c * math.log2(math.e)) * softplus
    wg = wg_ref[...]
    bg = bg_ref[...]
    for b in range(nseq):
        xc = xc_ref[b * seq:(b + 1) * seq, :]
        th = jnp.tanh(_dot(xc.astype(MM_DTYPE), wg) + bg)
        xh = 0.5 * xc
        for d, (a_ref, u_ref) in enumerate(((af_ref, uf_ref), (ab_ref, ub_ref))):
            tx = th[:, (2 * d) * LANES:(2 * d + 1) * LANES]
            ta = th[:, (2 * d + 1) * LANES:(2 * d + 2) * LANES]
            hd = half_decay[d:d + 1, :]
            a = jnp.exp2(ta * hd + hd)
            z = 1.0 - a * a
            u = jnp.where(z > 0.0, z * lax.rsqrt(z), 0.0) * ((tx + 1.0) * xh)
            a_ref[b * pitch:b * pitch + seq, :] = a
            u_ref[b * pitch:b * pitch + seq, :] = u

    if has_state:
        hf0 = h0f_ref[...]
        hb0 = h0b_ref[...]
    else:
        hf0 = jnp.zeros((nseq, LANES), F32)
        hb0 = hf0

    def step(t, carry):
        hf, hb = carry
        fwd = pl.ds(t, nseq, stride=pitch)
        hf = af_ref[fwd, :] * hf + uf_ref[fwd, :]
        hf_ref[fwd, :] = hf
        bwd = pl.ds(seq - 1 - t, nseq, stride=pitch)
        hb = ab_ref[bwd, :] * hb + ub_ref[bwd, :]
        hb_ref[bwd, :] = hb
        return hf, hb

    hf, hb = lax.fori_loop(0, seq, step, (hf0, hb0), unroll=8)
    if not has_state:
        sf_ref[...] = hf
        sb_ref[...] = hb
    for b in range(nseq):
        hsum = hf_ref[b * pitch:b * pitch + seq, :] + hb_ref[b * pitch:b * pitch + seq, :]
        o_ref[b * seq:(b + 1) * seq, :] = (hsum * yg_ref[b * seq:(b + 1) * seq, :]).astype(MM_DTYPE)


def _rec_scan(proj, w_gates, b_gates, lam, wl, state, *, cfg, seq):
    n, w2 = proj.shape
    w = w2 // 2
    nb = w // LANES
    assert nb == cfg.lru_blocks
    rows = SUBLANES * seq
    has_state = state is not None
    col = lambda g, c: (g, c)
    ins = [proj, proj, w_gates, b_gates, lam]
    in_specs = [
        pl.BlockSpec((rows, LANES), col),
        pl.BlockSpec((rows, LANES), lambda g, c: (g, c + nb)),
        pl.BlockSpec((None, None, LANES, 4 * LANES), lambda g, c: (wl, c, 0, 0)),
        pl.BlockSpec((None, None, 1, 4 * LANES), lambda g, c: (wl, c, 0, 0)),
        pl.BlockSpec((None, 2, LANES), lambda g, c: (wl, 0, c)),
    ]
    out_shape = [jax.ShapeDtypeStruct((n, w), MM_DTYPE)]
    out_specs = [pl.BlockSpec((rows, LANES), col)]
    if has_state:
        ins += list(state)
        in_specs += [pl.BlockSpec((SUBLANES, LANES), col)] * 2
    else:
        nstate = n // seq
        out_shape += [jax.ShapeDtypeStruct((nstate, w), F32)] * 2
        out_specs += [pl.BlockSpec((SUBLANES, LANES), col)] * 2
    scratch = [pltpu.VMEM((SUBLANES * (seq + 4), LANES), F32)] * 6
    return pl.pallas_call(
        functools.partial(_rec_scan_body, seq=seq, cfg=cfg, has_state=has_state),
        out_shape=out_shape,
        grid=(n // rows, nb),
        in_specs=in_specs,
        out_specs=out_specs,
        scratch_shapes=scratch,
        compiler_params=_params(("arbitrary", "arbitrary")),
        name="rec_scan_state" if has_state else "rec_scan",
    )(*ins)


def _swap_halves(w, rope):
    q = rope // 4
    return jnp.concatenate([w[..., q:2 * q], w[..., :q], w[..., 3 * q:], w[..., 2 * q:3 * q]], axis=-1)


def _pad_lanes(w):
    return jnp.pad(w, [(0, 0)] * (w.ndim - 1) + [(0, LANES - w.shape[-1])])


def _rope_tables(seq, cfg):
    rows = seq // cfg.grid_w
    row = jnp.repeat(jnp.arange(rows), cfg.grid_w).astype(F32)
    col = jnp.tile(jnp.arange(cfg.grid_w), rows).astype(F32)
    half = cfg.rope // 2
    inv = 1.0 / (cfg.rope_base ** (jnp.arange(0, half, 2, dtype=F32) / half))
    ar, ac = row[:, None] * inv, col[:, None] * inv
    cos = jnp.concatenate([jnp.cos(ar), jnp.cos(ar), jnp.cos(ac), jnp.cos(ac)], axis=-1)
    sin = jnp.concatenate([-jnp.sin(ar), jnp.sin(ar), -jnp.sin(ac), jnp.sin(ac)], axis=-1)
    return _pad_lanes(cos), _pad_lanes(sin)


def _pick(n, pref):
    return pref if n % pref == 0 else n


def _forward(cfg, x_prompt, x_sample, cache_ckv, cache_krope, state_lru, c, c_ctx,
             g_mix, g_ffn, g_final, w_ada, b_ada,
             w_mla_in, g_mla_q, g_mla_kv, w_mla_uq, w_mla_uk, w_mla_uv, w_mla_o,
             w_rec_in, w_rec_conv, b_rec_conv, w_rec_gx, b_rec_gx, w_rec_ga, b_rec_ga,
             rec_lambda, w_rec_out,
             w_ffn_up, w_ffn_conv, b_ffn_conv, w_ffn_down):
    depth, d = g_mix.shape
    bp, sp, _ = x_prompt.shape
    bs, ss, _ = x_sample.shape
    assert cfg.nope == LANES and cfg.v_dim == LANES and cfg.rope <= LANES and bs + 1 <= MOD_ROWS
    assert bp % SUBLANES == 0 and bs % SUBLANES == 0
    eps = cfg.eps
    cast = lambda w: w.astype(MM_DTYPE)
    row3 = lambda v: v[:, None, :]

    cond = jnp.zeros((MOD_ROWS, d), F32).at[0].set(c_ctx).at[1:1 + bs].set(c)
    mod = _adaln(cond, w_ada, b_ada, _pick(w_ada.shape[-1], 1024)).reshape(depth, MOD_ROWS, N_MOD, d)

    ql, kl, r, nh = cfg.q_lora, cfg.kv_lora, cfg.rope, cfg.n_heads
    w_kr = w_mla_in[..., ql + kl:]
    w_in_p = cast(jnp.concatenate(
        [w_mla_in[..., :ql + kl], _pad_lanes(w_kr), _pad_lanes(_swap_halves(w_kr, r))], axis=-1))
    uq = w_mla_uq.reshape(w_mla_uq.shape[0], ql, nh, cfg.nope + r)
    flat = lambda w: cast(w.reshape(w.shape[0], ql, nh * LANES))
    w_qn, w_qr, w_qs = flat(uq[..., :cfg.nope]), flat(_pad_lanes(uq[..., cfg.nope:])), flat(
        _pad_lanes(_swap_halves(uq[..., cfg.nope:], r)))
    w_uk, w_uv, w_o = cast(w_mla_uk), cast(w_mla_uv), cast(w_mla_o)
    g_q, g_kv = row3(g_mla_q), row3(g_mla_kv)
    cos, sin = _rope_tables(ss, cfg)

    w_gates = cast(0.5 * jnp.concatenate([w_rec_gx[:, 0], w_rec_ga[:, 0], w_rec_gx[:, 1], w_rec_ga[:, 1]], axis=-1))
    nb = cfg.lru_blocks
    blk = lambda b: b.reshape(b.shape[0], nb, 1, LANES)
    b_gates = 0.5 * jnp.concatenate(
        [blk(b_rec_gx[:, 0]), blk(b_rec_ga[:, 0]), blk(b_rec_gx[:, 1]), blk(b_rec_ga[:, 1])], axis=-1)
    w_rin, w_rout = cast(w_rec_in), cast(w_rec_out)
    w_up, w_down = cast(w_ffn_up), cast(w_ffn_down)
    g_mix3, g_ffn3, b_rconv, b_fconv = row3(g_mix), row3(g_ffn), row3(b_rec_conv), row3(b_ffn_conv)

    ctx_row = lambda i, tm: 0
    tf = _pick(w_down.shape[1], 512)

    def run_group(x, seq, mod_row, sample):
        n = x.shape[0]
        rows = lambda pref: min(seq, pref) if sample else _pick(n, pref)
        tm_ffn = seq if sample else max(seq, _pick(n, 1024))
        ckv_new, kr_new, lru_new = [], [], []
        for layer in range(depth):
            j = layer // 2
            if layer % 2 == 0:
                proj = _norm_mod_matmul(x, mod, mod_row, g_mix3, w_in_p, layer, j, tm=rows(1024),
                                        tn=w_in_p.shape[-1], eps=eps)
                if sample:
                    qn, qr, kn, v, krp = _mla_mid(proj, g_q, g_kv, w_qn, w_qr, w_qs, w_uk, w_uv, j, cos, sin,
                                                  cfg=cfg, seq=seq, tm=rows(256))
                    cache = _cache_expand(cache_ckv, cache_krope, j, w_uk, w_uv, cfg=cfg)
                    att = _attention(qn, qr, kn, krp, v, cache, cfg=cfg, seq=seq, hb=min(2, nh))
                else:
                    qn, qr, kn, v, krp, ckv, kr = _mla_mid(proj, g_q, g_kv, w_qn, w_qr, None, w_uk, w_uv, j,
                                                           None, None, cfg=cfg, seq=seq, tm=rows(256))
                    ckv_new.append(ckv)
                    kr_new.append(kr)
                    att = _attention(qn, qr, kn, krp, v, None, cfg=cfg, seq=seq, hb=nh)
                x = _matmul_residual(att, w_o, j, x, mod, layer, mod_row, tm=rows(512))
            else:
                proj = _rec_in(x, mod, mod_row, g_mix3, w_rin, w_rec_conv, b_rconv, layer, j, seq=seq,
                               tm=max(seq, rows(1024)), tn=_pick(w_rin.shape[-1] // 2, 1024), eps=eps)
                outs = _rec_scan(proj, w_gates, b_gates, rec_lambda, j,
                                 (state_lru[:, j, 0], state_lru[:, j, 1]) if sample else None, cfg=cfg, seq=seq)
                if not sample:
                    lru_new.append(jnp.stack(outs[1:], axis=1))
                x = _matmul_residual(outs[0], w_rout, j, x, mod, layer, mod_row, tm=rows(512))
            x = _conv_ffn(x, mod, layer, mod_row, g_ffn3, w_up, w_ffn_conv, b_fconv, w_down,
                          g_final[None] if layer == depth - 1 else None, seq=seq, tm=tm_ffn, tf=tf,
                          n_sub=2 if tf % (2 * LANES) == 0 else 1, x_buffers=1 if tm_ffn > 512 else 2, eps=eps)
        return x, ckv_new, kr_new, lru_new

    y_p, ckv_new, kr_new, lru_new = run_group(x_prompt.reshape(bp * sp, d), sp, ctx_row, False)
    y_s, _, _, _ = run_group(x_sample.reshape(bs * ss, d), ss, lambda i, tm: 1 + (i * tm) // ss, True)
    new_ckv = jnp.stack([t.reshape(bp, sp, kl) for t in ckv_new], axis=1)
    new_kr = jnp.stack([t.reshape(bp, sp, r) for t in kr_new], axis=1)
    new_lru = jnp.stack(lru_new, axis=1)
    return y_p.reshape(bp, sp, d), y_s.reshape(bs, ss, d), new_ckv, new_kr, new_lru


CFG = Cfg(n_heads=16, q_lora=512, kv_lora=512, nope=128, rope=64, v_dim=128, grid_w=64, rope_base=10000.0,
          lru_blocks=16, lru_c=8.0, eps=1e-6)


def kernel(x_prompt, x_sample, cache_ckv, cache_krope, state_lru, c, c_ctx, g_mix, g_ffn, g_final, w_ada, b_ada, w_mla_in, g_mla_q, g_mla_kv, w_mla_uq, w_mla_uk, w_mla_uv, w_mla_o, w_rec_in, w_rec_conv, b_rec_conv, w_rec_gx, b_rec_gx, w_rec_ga, b_rec_ga, rec_lambda, w_rec_out, w_ffn_up, w_ffn_conv, b_ffn_conv, w_ffn_down):
    return _forward(CFG, x_prompt, x_sample, cache_ckv, cache_krope, state_lru, c, c_ctx, g_mix, g_ffn, g_final,
                    w_ada, b_ada, w_mla_in, g_mla_q, g_mla_kv, w_mla_uq, w_mla_uk, w_mla_uv, w_mla_o,
                    w_rec_in, w_rec_conv, b_rec_conv, w_rec_gx, b_rec_gx, w_rec_ga, b_rec_ga, rec_lambda, w_rec_out,
                    w_ffn_up, w_ffn_conv, b_ffn_conv, w_ffn_down)
```

```python
import functools
import math
from typing import NamedTuple

import jax
import jax.numpy as jnp
from jax import lax
from jax.experimental import pallas as pl
from jax.experimental.pallas import tpu as pltpu

F32 = jnp.float32
MM_DTYPE = jnp.bfloat16

LANES = 128
SUBLANES = 8
VMEM_LIMIT_BYTES = 60 * 1024 * 1024
N_MOD = 6
MOD_ROWS = 16


class Cfg(NamedTuple):
    n_heads: int
    q_lora: int
    kv_lora: int
    nope: int
    rope: int
    v_dim: int
    grid_w: int
    rope_base: float
    lru_blocks: int
    lru_c: float
    eps: float


def _params(sem):
    return pltpu.CompilerParams(dimension_semantics=sem, vmem_limit_bytes=VMEM_LIMIT_BYTES)


def _dot(a, b):
    return jnp.dot(a, b, preferred_element_type=F32)


def _dot_nt(a, b):
    return lax.dot_general(a, b, (((1,), (1,)), ((), ())), preferred_element_type=F32)


def _const_spec(shape):
    nd = len(shape)
    return pl.BlockSpec(shape, lambda *_: (0,) * nd)


def _layer_spec(arr, layer):
    nd = arr.ndim
    return pl.BlockSpec((None,) + arr.shape[1:], lambda *_: (layer,) + (0,) * (nd - 1))


def _mod_spec(d, layer, mod_row, tm):
    return pl.BlockSpec((None, None, N_MOD, d), lambda i, *_: (layer, mod_row(i, tm), 0, 0))


def _rms(x, g, eps):
    return x * lax.rsqrt(jnp.mean(x * x, axis=-1, keepdims=True) + eps) * g


def _norm_mod(x, g, mod_ref, shift_idx, scale_idx, eps):
    y = _rms(x, g, eps)
    return y * (1.0 + mod_ref[scale_idx:scale_idx + 1, :]) + mod_ref[shift_idx:shift_idx + 1, :]


ROW_PAD = SUBLANES


def _zero_row_pads(u_ref, rows):
    lead = (slice(None),) * (len(u_ref.shape) - 2)
    zeros = jnp.zeros(u_ref.shape[:-2] + (ROW_PAD, u_ref.shape[-1]), u_ref.dtype)
    u_ref[lead + (slice(0, ROW_PAD), slice(None))] = zeros
    u_ref[lead + (slice(ROW_PAD + rows, ROW_PAD + rows + ROW_PAD), slice(None))] = zeros


def _store_slabs(u_ref, first, u):
    for k in range(u.shape[1] // LANES):
        u_ref[first + k, ROW_PAD:ROW_PAD + u.shape[0], :] = u[:, k * LANES:(k + 1) * LANES]


def _seq_window(u_ref, off, rows, seq):
    w = u_ref[ROW_PAD + off:ROW_PAD + off + rows, :]
    if off == 0 or seq == rows:
        return w
    sub_row = lax.broadcasted_iota(jnp.int32, (SUBLANES, 1), 0)
    pieces = []
    for base in range(0, rows, seq):
        if off < 0:
            tile, bad = base, sub_row < -off
        else:
            tile, bad = base + seq - SUBLANES, sub_row >= SUBLANES - off
        pieces += [w[base:tile], jnp.where(bad, 0.0, w[tile:tile + SUBLANES]), w[tile + SUBLANES:base + seq]]
    return jnp.concatenate([p for p in pieces if p.shape[0]], axis=0)


def _adaln_body(cond_ref, w_ref, b_ref, o_ref):
    c = cond_ref[...]
    s = (c * jax.nn.sigmoid(c)).astype(MM_DTYPE)
    o_ref[...] = _dot(s, w_ref[...].astype(MM_DTYPE)) + b_ref[...]


def _adaln(cond, w_ada, b_ada, tn):
    n_layers, d, n_out = w_ada.shape
    return pl.pallas_call(
        _adaln_body,
        out_shape=jax.ShapeDtypeStruct((n_layers, MOD_ROWS, n_out), F32),
        grid=(n_layers, n_out // tn),
        in_specs=[
            _const_spec((MOD_ROWS, d)),
            pl.BlockSpec((None, d, tn), lambda l, n: (l, 0, n)),
            pl.BlockSpec((None, 1, tn), lambda l, n: (l, 0, n)),
        ],
        out_specs=pl.BlockSpec((None, MOD_ROWS, tn), lambda l, n: (l, 0, n)),
        compiler_params=_params(("arbitrary", "arbitrary")),
        name="adaln",
    )(cond, w_ada, b_ada.reshape(n_layers, 1, n_out))


def _nmm_body(x_ref, mod_ref, g_ref, w_ref, o_ref, h_ref, *, eps):
    @pl.when(pl.program_id(1) == 0)
    def _():
        h = _norm_mod(x_ref[...], g_ref[...], mod_ref, 0, 1, eps)
        h_ref[...] = h.astype(MM_DTYPE)

    o_ref[...] = _dot(h_ref[...], w_ref[...])


def _norm_mod_matmul(x, mod, mod_row, g, w, layer, wl, *, tm, tn, eps):
    n, d = x.shape
    m = w.shape[2]
    return pl.pallas_call(
        functools.partial(_nmm_body, eps=eps),
        out_shape=jax.ShapeDtypeStruct((n, m), F32),
        grid=(n // tm, m // tn),
        in_specs=[
            pl.BlockSpec((tm, d), lambda i, j: (i, 0)),
            _mod_spec(d, layer, mod_row, tm),
            _layer_spec(g, layer),
            pl.BlockSpec((None, d, tn), lambda i, j: (wl, 0, j)),
        ],
        out_specs=pl.BlockSpec((tm, tn), lambda i, j: (i, j)),
        scratch_shapes=[pltpu.VMEM((tm, d), MM_DTYPE)],
        compiler_params=_params(("arbitrary", "arbitrary")),
        name="norm_mod_matmul",
    )(x, mod, g, w)


def _rec_in_body(x_ref, mod_ref, g_ref, w_ref, cw_ref, cb_ref, o_ref, h_ref, u_ref, *, seq, n_y, eps):
    j = pl.program_id(1)
    tm = h_ref.shape[0]

    @pl.when(j == 0)
    def _():
        h = _norm_mod(x_ref[...], g_ref[...], mod_ref, 0, 1, eps)
        h_ref[...] = h.astype(MM_DTYPE)
        _zero_row_pads(u_ref, tm)

    @pl.when(j < n_y)
    def _():
        o_ref[...] = jax.nn.gelu(_dot(h_ref[...], w_ref[...]))

    @pl.when(j >= n_y)
    def _():
        _store_slabs(u_ref, 0, _dot(h_ref[...], w_ref[...]))
        for c in range(u_ref.shape[0]):
            sl = slice(c * LANES, (c + 1) * LANES)
            xc = cb_ref[:, sl]
            for tap, off in enumerate((-2, -1, 0, 1)):
                xc = xc + _seq_window(u_ref.at[c], off, tm, seq) * cw_ref[tap:tap + 1, sl]
            o_ref[:, sl] = xc


def _rec_in(x, mod, mod_row, g, w, w_conv, b_conv, layer, wl, *, seq, tm, tn, eps):
    n, d = x.shape
    m = w.shape[2]
    n_y = m // 2 // tn
    assert tm % seq == 0 and w_conv.shape[1] == 4 and m % (2 * tn) == 0
    conv_col = lambda i, j: (wl, 0, jnp.maximum(j - n_y, 0))
    return pl.pallas_call(
        functools.partial(_rec_in_body, seq=seq, n_y=n_y, eps=eps),
        out_shape=jax.ShapeDtypeStruct((n, m), F32),
        grid=(n // tm, m // tn),
        in_specs=[
            pl.BlockSpec((tm, d), lambda i, j: (i, 0)),
            _mod_spec(d, layer, mod_row, tm),
            _layer_spec(g, layer),
            pl.BlockSpec((None, d, tn), lambda i, j: (wl, 0, j)),
            pl.BlockSpec((None, 4, tn), conv_col),
            pl.BlockSpec((None, 1, tn), conv_col),
        ],
        out_specs=pl.BlockSpec((tm, tn), lambda i, j: (i, j)),
        scratch_shapes=[pltpu.VMEM((tm, d), MM_DTYPE), pltpu.VMEM((tn // LANES, ROW_PAD + tm + ROW_PAD, LANES), F32)],
        compiler_params=_params(("arbitrary", "arbitrary")),
        name="rec_in",
    )(x, mod, g, w, w_conv, b_conv)


def _mmres_body(a_ref, w_ref, x_ref, mod_ref, o_ref):
    o_ref[...] = x_ref[...] + mod_ref[2:3, :] * _dot(a_ref[...], w_ref[...])


def _matmul_residual(a, w, wl, x, mod, layer, mod_row, *, tm):
    n, k = a.shape
    d = w.shape[2]
    return pl.pallas_call(
        _mmres_body,
        out_shape=jax.ShapeDtypeStruct((n, d), F32),
        grid=(n // tm,),
        in_specs=[
            pl.BlockSpec((tm, k), lambda i: (i, 0)),
            _layer_spec(w, wl),
            pl.BlockSpec((tm, d), lambda i: (i, 0)),
            _mod_spec(d, layer, mod_row, tm),
        ],
        out_specs=pl.BlockSpec((tm, d), lambda i: (i, 0)),
        compiler_params=_params(("arbitrary",)),
        name="matmul_residual",
    )(a, w, x, mod)


def _ffn_body(*refs, seq, eps, n_sub, final):
    if final:
        (x_ref, mod_ref, g_ref, wg_ref, wv_ref, cwg_ref, cwv_ref, cbg_ref, cbv_ref, wd_ref, gf_ref,
         o_ref, h_ref, act_ref, ug_ref, uv_ref) = refs
    else:
        (x_ref, mod_ref, g_ref, wg_ref, wv_ref, cwg_ref, cwv_ref, cbg_ref, cbv_ref, wd_ref,
         o_ref, h_ref, act_ref, ug_ref, uv_ref) = refs
    j = pl.program_id(1)
    n_tiles = pl.num_programs(1) - 1
    tm = h_ref.shape[0]
    ts = wg_ref.shape[1] // n_sub
    subs = [slice(s * ts, (s + 1) * ts) for s in range(n_sub)]

    def conv(u_ref, c, cw, cb):
        win = lambda off: _seq_window(u_ref.at[c], off, tm, seq)
        return cb + win(-1) * cw[0:1, :] + win(0) * cw[1:2, :] + win(1) * cw[2:3, :]

    def up():
        h = h_ref[...]
        for s, sl in enumerate(subs):
            _store_slabs(ug_ref, s * (ts // LANES), _dot(h, wg_ref[:, sl]))
            _store_slabs(uv_ref, s * (ts // LANES), _dot(h, wv_ref[:, sl]))

    def activate():
        for c in range(ug_ref.shape[0]):
            sl = slice(c * LANES, (c + 1) * LANES)
            hg = conv(ug_ref, c, 0.5 * cwg_ref[:, sl], 0.5 * cbg_ref[:, sl])
            val = conv(uv_ref, c, cwv_ref[:, sl], cbv_ref[:, sl])
            act_ref[:, sl] = (hg * (jnp.tanh(hg) + 1.0) * val).astype(MM_DTYPE)

    def down(act):
        o_ref[...] += _dot(act, wd_ref[...])

    @pl.when(j == 0)
    def _():
        h = _norm_mod(x_ref[...], g_ref[...], mod_ref, 3, 4, eps)
        h_ref[...] = h.astype(MM_DTYPE)
        o_ref[...] = jnp.zeros_like(o_ref)
        for u_ref in (ug_ref, uv_ref):
            _zero_row_pads(u_ref, tm)
        up()
        activate()

    @pl.when((j > 0) & (j < n_tiles))
    def _():
        down(act_ref[...])
        up()
        activate()

    @pl.when(j == n_tiles)
    def _():
        down(act_ref[...])
        y = x_ref[...] + mod_ref[5:6, :] * o_ref[...]
        if final:
            y = _rms(y, gf_ref[...], eps)
        o_ref[...] = y


def _conv_ffn(x, mod, layer, mod_row, g, w_up, w_conv, b_conv, w_down, g_final, *, seq, tm, tf, n_sub, x_buffers,
              eps):
    n, d = x.shape
    f = w_down.shape[1]
    assert w_conv.shape[1:] == (3, 2 * f) and tm % seq == 0 and f % tf == 0 and tf % (n_sub * LANES) == 0
    nf = f // tf
    final = g_final is not None
    up_tile = lambda j: jnp.minimum(j, nf - 1)
    lo = lambda rows: pl.BlockSpec((None, rows, tf), lambda i, j: (layer, 0, up_tile(j)))
    hi = lambda rows: pl.BlockSpec((None, rows, tf), lambda i, j: (layer, 0, up_tile(j) + nf))
    ins = [x, mod, g, w_up, w_up, w_conv, w_conv, b_conv, b_conv, w_down] + ([g_final] if final else [])
    in_specs = [
        pl.BlockSpec((tm, d), lambda i, j: (i, 0), pipeline_mode=pl.Buffered(x_buffers)),
        _mod_spec(d, layer, mod_row, tm),
        _layer_spec(g, layer),
        lo(d), hi(d), lo(3), hi(3), lo(1), hi(1),
        pl.BlockSpec((None, tf, d), lambda i, j: (layer, jnp.maximum(j - 1, 0), 0)),
    ] + ([_const_spec(g_final.shape)] if final else [])
    return pl.pallas_call(
        functools.partial(_ffn_body, seq=seq, eps=eps, n_sub=n_sub, final=final),
        out_shape=jax.ShapeDtypeStruct((n, d), F32),
        grid=(n // tm, nf + 1),
        in_specs=in_specs,
        out_specs=pl.BlockSpec((tm, d), lambda i, j: (i, 0)),
        scratch_shapes=[pltpu.VMEM((tm, d), MM_DTYPE), pltpu.VMEM((tm, tf), MM_DTYPE)]
        + [pltpu.VMEM((tf // LANES, ROW_PAD + tm + ROW_PAD, LANES), F32)] * 2,
        compiler_params=_params(("arbitrary", "arbitrary")),
        name="conv_ffn_final" if final else "conv_ffn",
    )(*ins)


def _mla_mid_body(*refs, cfg, rope):
    if rope:
        (proj_ref, gq_ref, gkv_ref, wqn_ref, wqr_ref, wqs_ref, wuk_ref, wuv_ref, cos_ref, sin_ref,
         qn_ref, qr_ref, kn_ref, v_ref, krp_ref) = refs
    else:
        (proj_ref, gq_ref, gkv_ref, wqn_ref, wqr_ref, wuk_ref, wuv_ref,
         qn_ref, qr_ref, kn_ref, v_ref, krp_ref, ckv_ref, kr_ref) = refs
    ql, kl = cfg.q_lora, cfg.kv_lora
    p = proj_ref[...]
    cqn = _rms(p[:, :ql], gq_ref[...], cfg.eps).astype(MM_DTYPE)
    ckv = _rms(p[:, ql:ql + kl], gkv_ref[...], cfg.eps)
    krp = p[:, ql + kl:ql + kl + LANES]
    qn_ref[...] = _dot(cqn, wqn_ref[...]).astype(MM_DTYPE)
    qr = _dot(cqn, wqr_ref[...])
    if rope:
        krs = p[:, ql + kl + LANES:ql + kl + 2 * LANES]
        qs = _dot(cqn, wqs_ref[...])
        cos = cos_ref[...]
        sin = sin_ref[...]
        for h in range(cfg.n_heads):
            sl = slice(h * LANES, (h + 1) * LANES)
            qr_ref[:, sl] = (qr[:, sl] * cos + qs[:, sl] * sin).astype(MM_DTYPE)
        krp_ref[...] = (krp * cos + krs * sin).astype(MM_DTYPE)
    else:
        qr_ref[...] = qr.astype(MM_DTYPE)
        krp_ref[...] = krp.astype(MM_DTYPE)
        ckv_ref[...] = ckv
        kr_ref[...] = krp[:, :cfg.rope]
    ckv_b = ckv.astype(MM_DTYPE)
    kn_ref[...] = _dot(ckv_b, wuk_ref[...]).astype(MM_DTYPE)
    v_ref[...] = _dot(ckv_b, wuv_ref[...]).astype(MM_DTYPE)


def _mla_mid(proj, g_q, g_kv, w_qn, w_qr, w_qs, w_uk, w_uv, wl, cos, sin, *, cfg, seq, tm):
    n, pw = proj.shape
    rope = cos is not None
    hw = cfg.n_heads * LANES
    row = lambda i: (i, 0)
    ws = [g_q, g_kv, w_qn, w_qr] + ([w_qs] if rope else []) + [w_uk, w_uv]
    ins = [proj] + ws
    in_specs = [pl.BlockSpec((tm, pw), row)] + [_layer_spec(w, wl) for w in ws]
    out_shape = [jax.ShapeDtypeStruct((n, hw), MM_DTYPE)] * 4 + [jax.ShapeDtypeStruct((n, LANES), MM_DTYPE)]
    out_specs = [pl.BlockSpec((tm, hw), row)] * 4 + [pl.BlockSpec((tm, LANES), row)]
    if rope:
        bps = seq // tm
        ins += [cos, sin]
        in_specs += [pl.BlockSpec((tm, LANES), lambda i: (i % bps, 0))] * 2
    else:
        out_shape += [jax.ShapeDtypeStruct((n, cfg.kv_lora), F32), jax.ShapeDtypeStruct((n, cfg.rope), F32)]
        out_specs += [pl.BlockSpec((tm, cfg.kv_lora), row), pl.BlockSpec((tm, cfg.rope), row)]
    return pl.pallas_call(
        functools.partial(_mla_mid_body, cfg=cfg, rope=rope),
        out_shape=out_shape,
        grid=(n // tm,),
        in_specs=in_specs,
        out_specs=out_specs,
        compiler_params=_params(("arbitrary",)),
        name="mla_mid_rope" if rope else "mla_mid",
    )(*ins)


def _cache_expand_body(ckv_ref, kr_ref, wuk_ref, wuv_ref, kn_ref, v_ref, krp_ref, *, rope):
    c = ckv_ref[...].astype(MM_DTYPE)
    kn_ref[...] = _dot(c, wuk_ref[...]).astype(MM_DTYPE)
    v_ref[...] = _dot(c, wuv_ref[...]).astype(MM_DTYPE)
    krp_ref[:, :rope] = kr_ref[...].astype(MM_DTYPE)
    krp_ref[:, rope:] = jnp.zeros((krp_ref.shape[0], LANES - rope), MM_DTYPE)


def _cache_expand(cache_ckv, cache_krope, wl, w_uk, w_uv, *, cfg):
    b, _, p, kl = cache_ckv.shape
    hw = cfg.n_heads * LANES
    row = lambda i: (i, 0)
    return pl.pallas_call(
        functools.partial(_cache_expand_body, rope=cfg.rope),
        out_shape=[jax.ShapeDtypeStruct((b * p, hw), MM_DTYPE)] * 2 + [jax.ShapeDtypeStruct((b * p, LANES), MM_DTYPE)],
        grid=(b,),
        in_specs=[
            pl.BlockSpec((None, None, p, kl), lambda i: (i, wl, 0, 0)),
            pl.BlockSpec((None, None, p, cfg.rope), lambda i: (i, wl, 0, 0)),
            _layer_spec(w_uk, wl),
            _layer_spec(w_uv, wl),
        ],
        out_specs=[pl.BlockSpec((p, hw), row)] * 2 + [pl.BlockSpec((p, LANES), row)],
        compiler_params=_params(("arbitrary",)),
        name="cache_expand",
    )(cache_ckv, cache_krope, w_uk, w_uv)


def _attn_body(*refs, hb, scale, cached):
    if cached:
        qn_ref, qr_ref, kn_ref, kr_ref, v_ref, knc_ref, krc_ref, vc_ref, o_ref = refs
    else:
        qn_ref, qr_ref, kn_ref, kr_ref, v_ref, o_ref = refs
    for h in range(hb):
        sl = slice(h * LANES, (h + 1) * LANES)
        q = jnp.concatenate([qn_ref[:, sl], qr_ref[:, sl]], axis=1)
        s = _dot_nt(q, jnp.concatenate([kn_ref[:, sl], kr_ref[...]], axis=1)) * scale
        m = jnp.max(s, axis=-1, keepdims=True)
        if cached:
            sc = _dot_nt(q, jnp.concatenate([knc_ref[:, sl], krc_ref[...]], axis=1)) * scale
            m = jnp.maximum(m, jnp.max(sc, axis=-1, keepdims=True))
            pc = jnp.exp(sc - m)
        p = jnp.exp(s - m)
        den = jnp.sum(p, axis=-1, keepdims=True)
        o = _dot(p.astype(MM_DTYPE), v_ref[:, sl])
        if cached:
            den = den + jnp.sum(pc, axis=-1, keepdims=True)
            o = o + _dot(pc.astype(MM_DTYPE), vc_ref[:, sl])
        o_ref[:, sl] = (o / den).astype(MM_DTYPE)


def _attention(qn, qr, kn, krp, v, cache, *, cfg, seq, hb):
    n, hw = qn.shape
    cached = cache is not None
    blk = lambda rows: pl.BlockSpec((rows, hb * LANES), lambda b, g: (b, g))
    shared = lambda rows: pl.BlockSpec((rows, LANES), lambda b, g: (b, 0))
    ins = [qn, qr, kn, krp, v]
    in_specs = [blk(seq), blk(seq), blk(seq), shared(seq), blk(seq)]
    if cached:
        knc, vc, krc = cache
        past = knc.shape[0] // (n // seq)
        ins += [knc, krc, vc]
        in_specs += [blk(past), shared(past), blk(past)]
    return pl.pallas_call(
        functools.partial(_attn_body, hb=hb, scale=float(cfg.nope + cfg.rope) ** -0.5, cached=cached),
        out_shape=jax.ShapeDtypeStruct((n, hw), MM_DTYPE),
        grid=(n // seq, cfg.n_heads // hb),
        in_specs=in_specs,
        out_specs=blk(seq),
        compiler_params=_params(("arbitrary", "arbitrary")),
        name="attention_cached" if cached else "attention",
    )(*ins)


def _rec_scan_body(*refs, seq, cfg, has_state):
    if has_state:
        (yg_ref, xc_ref, wg_ref, bg_ref, lam_ref, h0f_ref, h0b_ref,
         o_ref, af_ref, uf_ref, ab_ref, ub_ref, hf_ref, hb_ref) = refs
    else:
        (yg_ref, xc_ref, wg_ref, bg_ref, lam_ref,
         o_ref, sf_ref, sb_ref, af_ref, uf_ref, ab_ref, ub_ref, hf_ref, hb_ref) = refs
    nseq = SUBLANES
    pitch = seq + 4
    lam = lam_ref[...]
    softplus = jnp.maximum(-lam, 0.0) + jnp.log1p(jnp.exp(-jnp.abs(lam)))
    half_decay = (-0.5 * cfg.lru_c * math.log2(math.e)) * softplus
    wg = wg_ref[...]
    bg = bg_ref[...]
    for b in range(nseq):
        xc = xc_ref[b * seq:(b + 1) * seq, :]
        th = jnp.tanh(_dot(xc.astype(MM_DTYPE), wg) + bg)
        xh = 0.5 * xc
        for d, (a_ref, u_ref) in enumerate(((af_ref, uf_ref), (ab_ref, ub_ref))):
            tx = th[:, (2 * d) * LANES:(2 * d + 1) * LANES]
            ta = th[:, (2 * d + 1) * LANES:(2 * d + 2) * LANES]
            hd = half_decay[d:d + 1, :]
            a = jnp.exp2(ta * hd + hd)
            z = 1.0 - a * a
            u = jnp.where(z > 0.0, z * lax.rsqrt(z), 0.0) * ((tx + 1.0) * xh)
            a_ref[b * pitch:b * pitch + seq, :] = a
            u_ref[b * pitch:b * pitch + seq, :] = u

    if has_state:
        hf0 = h0f_ref[...]
        hb0 = h0b_ref[...]
    else:
        hf0 = jnp.zeros((nseq, LANES), F32)
        hb0 = hf0

    def step(t, carry):
        hf, hb = carry
        fwd = pl.ds(t, nseq, stride=pitch)
        hf = af_ref[fwd, :] * hf + uf_ref[fwd, :]
        hf_ref[fwd, :] = hf
        bwd = pl.ds(seq - 1 - t, nseq, stride=pitch)
        hb = ab_ref[bwd, :] * hb + ub_ref[bwd, :]
        hb_ref[bwd, :] = hb
        return hf, hb

    hf, hb = lax.fori_loop(0, seq, step, (hf0, hb0), unroll=8)
    if not has_state:
        sf_ref[...] = hf
        sb_ref[...] = hb
    for b in range(nseq):
        hsum = hf_ref[b * pitch:b * pitch + seq, :] + hb_ref[b * pitch:b * pitch + seq, :]
        o_ref[b * seq:(b + 1) * seq, :] = (hsum * yg_ref[b * seq:(b + 1) * seq, :]).astype(MM_DTYPE)


def _rec_scan(proj, w_gates, b_gates, lam, wl, state, *, cfg, seq):
    n, w2 = proj.shape
    w = w2 // 2
    nb = w // LANES
    assert nb == cfg.lru_blocks
    rows = SUBLANES * seq
    has_state = state is not None
    col = lambda g, c: (g, c)
    ins = [proj, proj, w_gates, b_gates, lam]
    in_specs = [
        pl.BlockSpec((rows, LANES), col),
        pl.BlockSpec((rows, LANES), lambda g, c: (g, c + nb)),
        pl.BlockSpec((None, None, LANES, 4 * LANES), lambda g, c: (wl, c, 0, 0)),
        pl.BlockSpec((None, None, 1, 4 * LANES), lambda g, c: (wl, c, 0, 0)),
        pl.BlockSpec((None, 2, LANES), lambda g, c: (wl, 0, c)),
    ]
    out_shape = [jax.ShapeDtypeStruct((n, w), MM_DTYPE)]
    out_specs = [pl.BlockSpec((rows, LANES), col)]
    if has_state:
        ins += list(state)
        in_specs += [pl.BlockSpec((SUBLANES, LANES), col)] * 2
    else:
        nstate = n // seq
        out_shape += [jax.ShapeDtypeStruct((nstate, w), F32)] * 2
        out_specs += [pl.BlockSpec((SUBLANES, LANES), col)] * 2
    scratch = [pltpu.VMEM((SUBLANES * (seq + 4), LANES), F32)] * 6
    return pl.pallas_call(
        functools.partial(_rec_scan_body, seq=seq, cfg=cfg, has_state=has_state),
        out_shape=out_shape,
        grid=(n // rows, nb),
        in_specs=in_specs,
        out_specs=out_specs,
        scratch_shapes=scratch,
        compiler_params=_params(("arbitrary", "arbitrary")),
        name="rec_scan_state" if has_state else "rec_scan",
    )(*ins)


def _swap_halves(w, rope):
    q = rope // 4
    return jnp.concatenate([w[..., q:2 * q], w[..., :q], w[..., 3 * q:], w[..., 2 * q:3 * q]], axis=-1)


def _pad_lanes(w):
    return jnp.pad(w, [(0, 0)] * (w.ndim - 1) + [(0, LANES - w.shape[-1])])


def _rope_tables(seq, cfg):
    rows = seq // cfg.grid_w
    row = jnp.repeat(jnp.arange(rows), cfg.grid_w).astype(F32)
    col = jnp.tile(jnp.arange(cfg.grid_w), rows).astype(F32)
    half = cfg.rope // 2
    inv = 1.0 / (cfg.rope_base ** (jnp.arange(0, half, 2, dtype=F32) / half))
    ar, ac = row[:, None] * inv, col[:, None] * inv
    cos = jnp.concatenate([jnp.cos(ar), jnp.cos(ar), jnp.cos(ac), jnp.cos(ac)], axis=-1)
    sin = jnp.concatenate([-jnp.sin(ar), jnp.sin(ar), -jnp.sin(ac), jnp.sin(ac)], axis=-1)
    return _pad_lanes(cos), _pad_lanes(sin)


def _pick(n, pref):
    return pref if n % pref == 0 else n


def _forward(cfg, x_prompt, x_sample, cache_ckv, cache_krope, state_lru, c, c_ctx,
             g_mix, g_ffn, g_final, w_ada, b_ada,
             w_mla_in, g_mla_q, g_mla_kv, w_mla_uq, w_mla_uk, w_mla_uv, w_mla_o,
             w_rec_in, w_rec_conv, b_rec_conv, w_rec_gx, b_rec_gx, w_rec_ga, b_rec_ga,
             rec_lambda, w_rec_out,
             w_ffn_up, w_ffn_conv, b_ffn_conv, w_ffn_down):
    depth, d = g_mix.shape
    bp, sp, _ = x_prompt.shape
    bs, ss, _ = x_sample.shape
    assert cfg.nope == LANES and cfg.v_dim == LANES and cfg.rope <= LANES and bs + 1 <= MOD_ROWS
    assert bp % SUBLANES == 0 and bs % SUBLANES == 0
    eps = cfg.eps
    cast = lambda w: w.astype(MM_DTYPE)
    row3 = lambda v: v[:, None, :]

    cond = jnp.zeros((MOD_ROWS, d), F32).at[0].set(c_ctx).at[1:1 + bs].set(c)
    mod = _adaln(cond, w_ada, b_ada, _pick(w_ada.shape[-1], 1024)).reshape(depth, MOD_ROWS, N_MOD, d)

    ql, kl, r, nh = cfg.q_lora, cfg.kv_lora, cfg.rope, cfg.n_heads
    w_kr = w_mla_in[..., ql + kl:]
    w_in_p = cast(jnp.concatenate(
        [w_mla_in[..., :ql + kl], _pad_lanes(w_kr), _pad_lanes(_swap_halves(w_kr, r))], axis=-1))
    uq = w_mla_uq.reshape(w_mla_uq.shape[0], ql, nh, cfg.nope + r)
    flat = lambda w: cast(w.reshape(w.shape[0], ql, nh * LANES))
    w_qn, w_qr, w_qs = flat(uq[..., :cfg.nope]), flat(_pad_lanes(uq[..., cfg.nope:])), flat(
        _pad_lanes(_swap_halves(uq[..., cfg.nope:], r)))
    w_uk, w_uv, w_o = cast(w_mla_uk), cast(w_mla_uv), cast(w_mla_o)
    g_q, g_kv = row3(g_mla_q), row3(g_mla_kv)
    cos, sin = _rope_tables(ss, cfg)

    w_gates = cast(0.5 * jnp.concatenate([w_rec_gx[:, 0], w_rec_ga[:, 0], w_rec_gx[:, 1], w_rec_ga[:, 1]], axis=-1))
    nb = cfg.lru_blocks
    blk = lambda b: b.reshape(b.shape[0], nb, 1, LANES)
    b_gates = 0.5 * jnp.concatenate(
        [blk(b_rec_gx[:, 0]), blk(b_rec_ga[:, 0]), blk(b_rec_gx[:, 1]), blk(b_rec_ga[:, 1])], axis=-1)
    w_rin, w_rout = cast(w_rec_in), cast(w_rec_out)
    w_up, w_down = cast(w_ffn_up), cast(w_ffn_down)
    g_mix3, g_ffn3, b_rconv, b_fconv = row3(g_mix), row3(g_ffn), row3(b_rec_conv), row3(b_ffn_conv)

    ctx_row = lambda i, tm: 0
    tf = _pick(w_down.shape[1], 512)

    def run_group(x, seq, mod_row, sample):
        n = x.shape[0]
        rows = lambda pref: min(seq, pref) if sample else _pick(n, pref)
        tm_ffn = seq if sample else max(seq, _pick(n, 1024))
        ckv_new, kr_new, lru_new = [], [], []
        for layer in range(depth):
            j = layer // 2
            if layer % 2 == 0:
                proj = _norm_mod_matmul(x, mod, mod_row, g_mix3, w_in_p, layer, j, tm=rows(1024),
                                        tn=w_in_p.shape[-1], eps=eps)
                if sample:
                    qn, qr, kn, v, krp = _mla_mid(proj, g_q, g_kv, w_qn, w_qr, w_qs, w_uk, w_uv, j, cos, sin,
                                                  cfg=cfg, seq=seq, tm=rows(256))
                    cache = _cache_expand(cache_ckv, cache_krope, j, w_uk, w_uv, cfg=cfg)
                    att = _attention(qn, qr, kn, krp, v, cache, cfg=cfg, seq=seq, hb=min(2, nh))
                else:
                    qn, qr, kn, v, krp, ckv, kr = _mla_mid(proj, g_q, g_kv, w_qn, w_qr, None, w_uk, w_uv, j,
                                                           None, None, cfg=cfg, seq=seq, tm=rows(256))
                    ckv_new.append(ckv)
                    kr_new.append(kr)
                    att = _attention(qn, qr, kn, krp, v, None, cfg=cfg, seq=seq, hb=nh)
                x = _matmul_residual(att, w_o, j, x, mod, layer, mod_row, tm=rows(512))
            else:
                proj = _rec_in(x, mod, mod_row, g_mix3, w_rin, w_rec_conv, b_rconv, layer, j, seq=seq,
                               tm=max(seq, rows(1024)), tn=_pick(w_rin.shape[-1] // 2, 1024), eps=eps)
                outs = _rec_scan(proj, w_gates, b_gates, rec_lambda, j,
                                 (state_lru[:, j, 0], state_lru[:, j, 1]) if sample else None, cfg=cfg, seq=seq)
                if not sample:
                    lru_new.append(jnp.stack(outs[1:], axis=1))
                x = _matmul_residual(outs[0], w_rout, j, x, mod, layer, mod_row, tm=rows(512))
            x = _conv_ffn(x, mod, layer, mod_row, g_ffn3, w_up, w_ffn_conv, b_fconv, w_down,
                          g_final[None] if layer == depth - 1 else None, seq=seq, tm=tm_ffn, tf=tf,
                          n_sub=2 if tf % (2 * LANES) == 0 else 1, x_buffers=1 if tm_ffn > 512 else 2, eps=eps)
        return x, ckv_new, kr_new, lru_new

    y_p, ckv_new, kr_new, lru_new = run_group(x_prompt.reshape(bp * sp, d), sp, ctx_row, False)
    y_s, _, _, _ = run_group(x_sample.reshape(bs * ss, d), ss, lambda i, tm: 1 + (i * tm) // ss, True)
    new_ckv = jnp.stack([t.reshape(bp, sp, kl) for t in ckv_new], axis=1)
    new_kr = jnp.stack([t.reshape(bp, sp, r) for t in kr_new], axis=1)
    new_lru = jnp.stack(lru_new, axis=1)
    return y_p.reshape(bp, sp, d), y_s.reshape(bs, ss, d), new_ckv, new_kr, new_lru


CFG = Cfg(n_heads=16, q_lora=512, kv_lora=512, nope=128, rope=64, v_dim=128, grid_w=64, rope_base=10000.0,
          lru_blocks=16, lru_c=8.0, eps=1e-6)


def kernel(x_prompt, x_sample, cache_ckv, cache_krope, state_lru, c, c_ctx, g_mix, g_ffn, g_final, w_ada, b_ada, w_mla_in, g_mla_q, g_mla_kv, w_mla_uq, w_mla_uk, w_mla_uv, w_mla_o, w_rec_in, w_rec_conv, b_rec_conv, w_rec_gx, b_rec_gx, w_rec_ga, b_rec_ga, rec_lambda, w_rec_out, w_ffn_up, w_ffn_conv, b_ffn_conv, w_ffn_down):
    return _forward(CFG, x_prompt, x_sample, cache_ckv, cache_krope, state_lru, c, c_ctx, g_mix, g_ffn, g_final,
                    w_ada, b_ada, w_mla_in, g_mla_q, g_mla_kv, w_mla_uq, w_mla_uk, w_mla_uv, w_mla_o,
                    w_rec_in, w_rec_conv, b_rec_conv, w_rec_gx, b_rec_gx, w_rec_ga, b_rec_ga, rec_lambda, w_rec_out,
                    w_ffn_up, w_ffn_conv, b_ffn_conv, w_ffn_down)
```

```python
import functools
import math
from typing import NamedTuple

import jax
import jax.numpy as jnp
from jax import lax
from jax.experimental import pallas as pl
from jax.experimental.pallas import tpu as pltpu

F32 = jnp.float32
MM_DTYPE = jnp.bfloat16

LANES = 128
SUBLANES = 8
VMEM_LIMIT_BYTES = 60 * 1024 * 1024
N_MOD = 6
MOD_ROWS = 16


class Cfg(NamedTuple):
    n_heads: int
    q_lora: int
    kv_lora: int
    nope: int
    rope: int
    v_dim: int
    grid_w: int
    rope_base: float
    lru_blocks: int
    lru_c: float
    eps: float


def _params(sem):
    return pltpu.CompilerParams(dimension_semantics=sem, vmem_limit_bytes=VMEM_LIMIT_BYTES)


def _dot(a, b):
    return jnp.dot(a, b, preferred_element_type=F32)


def _dot_nt(a, b):
    return lax.dot_general(a, b, (((1,), (1,)), ((), ())), preferred_element_type=F32)


def _const_spec(shape):
    nd = len(shape)
    return pl.BlockSpec(shape, lambda *_: (0,) * nd)


def _layer_spec(arr, layer):
    nd = arr.ndim
    return pl.BlockSpec((None,) + arr.shape[1:], lambda *_: (layer,) + (0,) * (nd - 1))


def _mod_spec(d, layer, mod_row, tm):
    return pl.BlockSpec((None, None, N_MOD, d), lambda i, *_: (layer, mod_row(i, tm), 0, 0))


def _rms(x, g, eps):
    return x * lax.rsqrt(jnp.mean(x * x, axis=-1, keepdims=True) + eps) * g


def _norm_mod(x, g, mod_ref, shift_idx, scale_idx, eps):
    y = _rms(x, g, eps)
    return y * (1.0 + mod_ref[scale_idx:scale_idx + 1, :]) + mod_ref[shift_idx:shift_idx + 1, :]


ROW_PAD = SUBLANES


def _zero_row_pads(u_ref, rows):
    lead = (slice(None),) * (len(u_ref.shape) - 2)
    zeros = jnp.zeros(u_ref.shape[:-2] + (ROW_PAD, u_ref.shape[-1]), u_ref.dtype)
    u_ref[lead + (slice(0, ROW_PAD), slice(None))] = zeros
    u_ref[lead + (slice(ROW_PAD + rows, ROW_PAD + rows + ROW_PAD), slice(None))] = zeros


def _store_slabs(u_ref, first, u):
    for k in range(u.shape[1] // LANES):
        u_ref[first + k, ROW_PAD:ROW_PAD + u.shape[0], :] = u[:, k * LANES:(k + 1) * LANES]


def _seq_window(u_ref, off, rows, seq):
    w = u_ref[ROW_PAD + off:ROW_PAD + off + rows, :]
    if off == 0 or seq == rows:
        return w
    sub_row = lax.broadcasted_iota(jnp.int32, (SUBLANES, 1), 0)
    pieces = []
    for base in range(0, rows, seq):
        if off < 0:
            tile, bad = base, sub_row < -off
        else:
            tile, bad = base + seq - SUBLANES, sub_row >= SUBLANES - off
        pieces += [w[base:tile], jnp.where(bad, 0.0, w[tile:tile + SUBLANES]), w[tile + SUBLANES:base + seq]]
    return jnp.concatenate([p for p in pieces if p.shape[0]], axis=0)


def _adaln_body(cond_ref, w_ref, b_ref, o_ref):
    c = cond_ref[...]
    s = (c * jax.nn.sigmoid(c)).astype(MM_DTYPE)
    o_ref[...] = _dot(s, w_ref[...].astype(MM_DTYPE)) + b_ref[...]


def _adaln(cond, w_ada, b_ada, tn):
    n_layers, d, n_out = w_ada.shape
    return pl.pallas_call(
        _adaln_body,
        out_shape=jax.ShapeDtypeStruct((n_layers, MOD_ROWS, n_out), F32),
        grid=(n_layers, n_out // tn),
        in_specs=[
            _const_spec((MOD_ROWS, d)),
            pl.BlockSpec((None, d, tn), lambda l, n: (l, 0, n)),
            pl.BlockSpec((None, 1, tn), lambda l, n: (l, 0, n)),
        ],
        out_specs=pl.BlockSpec((None, MOD_ROWS, tn), lambda l, n: (l, 0, n)),
        compiler_params=_params(("arbitrary", "arbitrary")),
        name="adaln",
    )(cond, w_ada, b_ada.reshape(n_layers, 1, n_out))


def _nmm_body(x_ref, mod_ref, g_ref, w_ref, o_ref, h_ref, *, eps):
    @pl.when(pl.program_id(1) == 0)
    def _():
        h = _norm_mod(x_ref[...], g_ref[...], mod_ref, 0, 1, eps)
        h_ref[...] = h.astype(MM_DTYPE)

    o_ref[...] = _dot(h_ref[...], w_ref[...])


def _norm_mod_matmul(x, mod, mod_row, g, w, layer, wl, *, tm, tn, eps):
    n, d = x.shape
    m = w.shape[2]
    return pl.pallas_call(
        functools.partial(_nmm_body, eps=eps),
        out_shape=jax.ShapeDtypeStruct((n, m), F32),
        grid=(n // tm, m // tn),
        in_specs=[
            pl.BlockSpec((tm, d), lambda i, j: (i, 0)),
            _mod_spec(d, layer, mod_row, tm),
            _layer_spec(g, layer),
            pl.BlockSpec((None, d, tn), lambda i, j: (wl, 0, j)),
        ],
        out_specs=pl.BlockSpec((tm, tn), lambda i, j: (i, j)),
        scratch_shapes=[pltpu.VMEM((tm, d), MM_DTYPE)],
        compiler_params=_params(("arbitrary", "arbitrary")),
        name="norm_mod_matmul",
    )(x, mod, g, w)


def _rec_in_body(x_ref, mod_ref, g_ref, w_ref, cw_ref, cb_ref, o_ref, h_ref, u_ref, *, seq, n_y, eps):
    j = pl.program_id(1)
    tm = h_ref.shape[0]

    @pl.when(j == 0)
    def _():
        h = _norm_mod(x_ref[...], g_ref[...], mod_ref, 0, 1, eps)
        h_ref[...] = h.astype(MM_DTYPE)
        _zero_row_pads(u_ref, tm)

    @pl.when(j < n_y)
    def _():
        y = jax.nn.gelu(_dot(h_ref[...], w_ref[...]))
        for c in range(o_ref.shape[0]):
            o_ref[c] = y[:, c * LANES:(c + 1) * LANES]

    @pl.when(j >= n_y)
    def _():
        _store_slabs(u_ref, 0, _dot(h_ref[...], w_ref[...]))
        for c in range(u_ref.shape[0]):
            sl = slice(c * LANES, (c + 1) * LANES)
            xc = cb_ref[:, sl]
            for tap, off in enumerate((-2, -1, 0, 1)):
                xc = xc + _seq_window(u_ref.at[c], off, tm, seq) * cw_ref[tap:tap + 1, sl]
            o_ref[c] = xc


def _col_tiles(w, tn):
    layers, d, m = w.shape
    return w.reshape(layers, d, m // tn, tn).transpose(0, 2, 1, 3)


def _rec_in(x, mod, mod_row, g, w, w_conv, b_conv, layer, wl, *, seq, tm, eps):
    n, d = x.shape
    tn = w.shape[3]
    m = w.shape[1] * tn
    n_y = m // 2 // tn
    assert tm % seq == 0 and w_conv.shape[1] == 4 and m % (2 * tn) == 0
    conv_col = lambda i, j: (wl, 0, jnp.maximum(j - n_y, 0))
    return pl.pallas_call(
        functools.partial(_rec_in_body, seq=seq, n_y=n_y, eps=eps),
        out_shape=jax.ShapeDtypeStruct((m // LANES, n, LANES), F32),
        grid=(n // tm, m // tn),
        in_specs=[
            pl.BlockSpec((tm, d), lambda i, j: (i, 0)),
            _mod_spec(d, layer, mod_row, tm),
            _layer_spec(g, layer),
            pl.BlockSpec((None, None, d, tn), lambda i, j: (wl, j, 0, 0)),
            pl.BlockSpec((None, 4, tn), conv_col),
            pl.BlockSpec((None, 1, tn), conv_col),
        ],
        out_specs=pl.BlockSpec((tn // LANES, tm, LANES), lambda i, j: (j, i, 0)),
        scratch_shapes=[pltpu.VMEM((tm, d), MM_DTYPE), pltpu.VMEM((tn // LANES, ROW_PAD + tm + ROW_PAD, LANES), F32)],
        compiler_params=_params(("arbitrary", "arbitrary")),
        name="rec_in",
    )(x, mod, g, w, w_conv, b_conv)


def _mmres_body(a_ref, w_ref, x_ref, mod_ref, o_ref):
    o_ref[...] = x_ref[...] + mod_ref[2:3, :] * _dot(a_ref[...], w_ref[...])


def _matmul_residual(a, w, wl, x, mod, layer, mod_row, *, tm):
    n, k = a.shape
    d = w.shape[2]
    return pl.pallas_call(
        _mmres_body,
        out_shape=jax.ShapeDtypeStruct((n, d), F32),
        grid=(n // tm,),
        in_specs=[
            pl.BlockSpec((tm, k), lambda i: (i, 0)),
            _layer_spec(w, wl),
            pl.BlockSpec((tm, d), lambda i: (i, 0)),
            _mod_spec(d, layer, mod_row, tm),
        ],
        out_specs=pl.BlockSpec((tm, d), lambda i: (i, 0)),
        compiler_params=_params(("arbitrary",)),
        name="matmul_residual",
    )(a, w, x, mod)


def _ffn_body(*refs, seq, eps, n_sub, final):
    if final:
        (x_ref, mod_ref, g_ref, wg_ref, wv_ref, cwg_ref, cwv_ref, cbg_ref, cbv_ref, wd_ref, gf_ref,
         o_ref, h_ref, act_ref, ug_ref, uv_ref) = refs
    else:
        (x_ref, mod_ref, g_ref, wg_ref, wv_ref, cwg_ref, cwv_ref, cbg_ref, cbv_ref, wd_ref,
         o_ref, h_ref, act_ref, ug_ref, uv_ref) = refs
    j = pl.program_id(1)
    n_tiles = pl.num_programs(1) - 1
    tm = h_ref.shape[0]
    ts = wg_ref.shape[1] // n_sub
    subs = [slice(s * ts, (s + 1) * ts) for s in range(n_sub)]

    def conv(u_ref, c, cw, cb):
        win = lambda off: _seq_window(u_ref.at[c], off, tm, seq)
        return cb + win(-1) * cw[0:1, :] + win(0) * cw[1:2, :] + win(1) * cw[2:3, :]

    def up():
        h = h_ref[...]
        for s, sl in enumerate(subs):
            _store_slabs(ug_ref, s * (ts // LANES), _dot(h, wg_ref[:, sl]))
            _store_slabs(uv_ref, s * (ts // LANES), _dot(h, wv_ref[:, sl]))

    def activate():
        for c in range(ug_ref.shape[0]):
            sl = slice(c * LANES, (c + 1) * LANES)
            hg = conv(ug_ref, c, 0.5 * cwg_ref[:, sl], 0.5 * cbg_ref[:, sl])
            val = conv(uv_ref, c, cwv_ref[:, sl], cbv_ref[:, sl])
            act_ref[:, sl] = (hg * (jnp.tanh(hg) + 1.0) * val).astype(MM_DTYPE)

    def down(act):
        o_ref[...] += _dot(act, wd_ref[...])

    @pl.when(j == 0)
    def _():
        h = _norm_mod(x_ref[...], g_ref[...], mod_ref, 3, 4, eps)
        h_ref[...] = h.astype(MM_DTYPE)
        o_ref[...] = jnp.zeros_like(o_ref)
        for u_ref in (ug_ref, uv_ref):
            _zero_row_pads(u_ref, tm)
        up()
        activate()

    @pl.when((j > 0) & (j < n_tiles))
    def _():
        down(act_ref[...])
        up()
        activate()

    @pl.when(j == n_tiles)
    def _():
        down(act_ref[...])
        y = x_ref[...] + mod_ref[5:6, :] * o_ref[...]
        if final:
            y = _rms(y, gf_ref[...], eps)
        o_ref[...] = y


def _conv_ffn(x, mod, layer, mod_row, g, w_up, w_conv, b_conv, w_down, g_final, *, seq, tm, n_sub, x_buffers, eps):
    n, d = x.shape
    f = w_down.shape[1]
    tf = w_up.shape[3]
    assert w_conv.shape[1:] == (3, 2 * f) and tm % seq == 0 and f % tf == 0 and tf % (n_sub * LANES) == 0
    nf = f // tf
    final = g_final is not None
    up_tile = lambda j: jnp.minimum(j, nf - 1)
    lo = lambda rows: pl.BlockSpec((None, rows, tf), lambda i, j: (layer, 0, up_tile(j)))
    hi = lambda rows: pl.BlockSpec((None, rows, tf), lambda i, j: (layer, 0, up_tile(j) + nf))
    ins = [x, mod, g, w_up, w_up, w_conv, w_conv, b_conv, b_conv, w_down] + ([g_final] if final else [])
    in_specs = [
        pl.BlockSpec((tm, d), lambda i, j: (i, 0), pipeline_mode=pl.Buffered(x_buffers)),
        _mod_spec(d, layer, mod_row, tm),
        _layer_spec(g, layer),
        pl.BlockSpec((None, None, d, tf), lambda i, j: (layer, up_tile(j), 0, 0)),
        pl.BlockSpec((None, None, d, tf), lambda i, j: (layer, up_tile(j) + nf, 0, 0)),
        lo(3), hi(3), lo(1), hi(1),
        pl.BlockSpec((None, tf, d), lambda i, j: (layer, jnp.maximum(j - 1, 0), 0)),
    ] + ([_const_spec(g_final.shape)] if final else [])
    return pl.pallas_call(
        functools.partial(_ffn_body, seq=seq, eps=eps, n_sub=n_sub, final=final),
        out_shape=jax.ShapeDtypeStruct((n, d), F32),
        grid=(n // tm, nf + 1),
        in_specs=in_specs,
        out_specs=pl.BlockSpec((tm, d), lambda i, j: (i, 0)),
        scratch_shapes=[pltpu.VMEM((tm, d), MM_DTYPE), pltpu.VMEM((tm, tf), MM_DTYPE)]
        + [pltpu.VMEM((tf // LANES, ROW_PAD + tm + ROW_PAD, LANES), F32)] * 2,
        compiler_params=_params(("arbitrary", "arbitrary")),
        name="conv_ffn_final" if final else "conv_ffn",
    )(*ins)


def _mla_mid_body(*refs, cfg, rope):
    if rope:
        (proj_ref, gq_ref, gkv_ref, wqn_ref, wqr_ref, wqs_ref, wuk_ref, wuv_ref, cos_ref, sin_ref,
         qn_ref, qr_ref, kn_ref, v_ref, krp_ref) = refs
    else:
        (proj_ref, gq_ref, gkv_ref, wqn_ref, wqr_ref, wuk_ref, wuv_ref,
         qn_ref, qr_ref, kn_ref, v_ref, krp_ref, ckv_ref, kr_ref) = refs
    ql, kl = cfg.q_lora, cfg.kv_lora
    p = proj_ref[...]
    cqn = _rms(p[:, :ql], gq_ref[...], cfg.eps).astype(MM_DTYPE)
    ckv = _rms(p[:, ql:ql + kl], gkv_ref[...], cfg.eps)
    krp = p[:, ql + kl:ql + kl + LANES]
    qn_ref[...] = _dot(cqn, wqn_ref[...]).astype(MM_DTYPE)
    qr = _dot(cqn, wqr_ref[...])
    if rope:
        krs = p[:, ql + kl + LANES:ql + kl + 2 * LANES]
        qs = _dot(cqn, wqs_ref[...])
        cos = cos_ref[...]
        sin = sin_ref[...]
        for h in range(cfg.n_heads):
            sl = slice(h * LANES, (h + 1) * LANES)
            qr_ref[:, sl] = (qr[:, sl] * cos + qs[:, sl] * sin).astype(MM_DTYPE)
        krp_ref[...] = (krp * cos + krs * sin).astype(MM_DTYPE)
    else:
        qr_ref[...] = qr.astype(MM_DTYPE)
        krp_ref[...] = krp.astype(MM_DTYPE)
        ckv_ref[...] = ckv
        kr_ref[...] = krp[:, :cfg.rope]
    ckv_b = ckv.astype(MM_DTYPE)
    kn_ref[...] = _dot(ckv_b, wuk_ref[...]).astype(MM_DTYPE)
    v_ref[...] = _dot(ckv_b, wuv_ref[...]).astype(MM_DTYPE)


def _mla_mid(proj, g_q, g_kv, w_qn, w_qr, w_qs, w_uk, w_uv, wl, cos, sin, *, cfg, seq, tm):
    n, pw = proj.shape
    rope = cos is not None
    hw = cfg.n_heads * LANES
    row = lambda i: (i, 0)
    ws = [g_q, g_kv, w_qn, w_qr] + ([w_qs] if rope else []) + [w_uk, w_uv]
    ins = [proj] + ws
    in_specs = [pl.BlockSpec((tm, pw), row)] + [_layer_spec(w, wl) for w in ws]
    out_shape = [jax.ShapeDtypeStruct((n, hw), MM_DTYPE)] * 4 + [jax.ShapeDtypeStruct((n, LANES), MM_DTYPE)]
    out_specs = [pl.BlockSpec((tm, hw), row)] * 4 + [pl.BlockSpec((tm, LANES), row)]
    if rope:
        bps = seq // tm
        ins += [cos, sin]
        in_specs += [pl.BlockSpec((tm, LANES), lambda i: (i % bps, 0))] * 2
    else:
        out_shape += [jax.ShapeDtypeStruct((n, cfg.kv_lora), F32), jax.ShapeDtypeStruct((n, cfg.rope), F32)]
        out_specs += [pl.BlockSpec((tm, cfg.kv_lora), row), pl.BlockSpec((tm, cfg.rope), row)]
    return pl.pallas_call(
        functools.partial(_mla_mid_body, cfg=cfg, rope=rope),
        out_shape=out_shape,
        grid=(n // tm,),
        in_specs=in_specs,
        out_specs=out_specs,
        compiler_params=_params(("arbitrary",)),
        name="mla_mid_rope" if rope else "mla_mid",
    )(*ins)


def _cache_expand_body(ckv_ref, kr_ref, wuk_ref, wuv_ref, kn_ref, v_ref, krp_ref, *, rope):
    c = ckv_ref[...].astype(MM_DTYPE)
    kn_ref[...] = _dot(c, wuk_ref[...]).astype(MM_DTYPE)
    v_ref[...] = _dot(c, wuv_ref[...]).astype(MM_DTYPE)
    krp_ref[:, :rope] = kr_ref[...].astype(MM_DTYPE)
    krp_ref[:, rope:] = jnp.zeros((krp_ref.shape[0], LANES - rope), MM_DTYPE)


def _cache_expand(cache_ckv, cache_krope, wl, w_uk, w_uv, *, cfg):
    b, _, p, kl = cache_ckv.shape
    hw = cfg.n_heads * LANES
    row = lambda i: (i, 0)
    return pl.pallas_call(
        functools.partial(_cache_expand_body, rope=cfg.rope),
        out_shape=[jax.ShapeDtypeStruct((b * p, hw), MM_DTYPE)] * 2 + [jax.ShapeDtypeStruct((b * p, LANES), MM_DTYPE)],
        grid=(b,),
        in_specs=[
            pl.BlockSpec((None, None, p, kl), lambda i: (i, wl, 0, 0)),
            pl.BlockSpec((None, None, p, cfg.rope), lambda i: (i, wl, 0, 0)),
            _layer_spec(w_uk, wl),
            _layer_spec(w_uv, wl),
        ],
        out_specs=[pl.BlockSpec((p, hw), row)] * 2 + [pl.BlockSpec((p, LANES), row)],
        compiler_params=_params(("arbitrary",)),
        name="cache_expand",
    )(cache_ckv, cache_krope, w_uk, w_uv)


def _attn_body(*refs, hb, scale, cached):
    if cached:
        qn_ref, qr_ref, kn_ref, kr_ref, v_ref, knc_ref, krc_ref, vc_ref, o_ref = refs
    else:
        qn_ref, qr_ref, kn_ref, kr_ref, v_ref, o_ref = refs
    for h in range(hb):
        sl = slice(h * LANES, (h + 1) * LANES)
        q = jnp.concatenate([qn_ref[:, sl], qr_ref[:, sl]], axis=1)
        s = _dot_nt(q, jnp.concatenate([kn_ref[:, sl], kr_ref[...]], axis=1)) * scale
        m = jnp.max(s, axis=-1, keepdims=True)
        if cached:
            sc = _dot_nt(q, jnp.concatenate([knc_ref[:, sl], krc_ref[...]], axis=1)) * scale
            m = jnp.maximum(m, jnp.max(sc, axis=-1, keepdims=True))
            pc = jnp.exp(sc - m)
        p = jnp.exp(s - m)
        den = jnp.sum(p, axis=-1, keepdims=True)
        o = _dot(p.astype(MM_DTYPE), v_ref[:, sl])
        if cached:
            den = den + jnp.sum(pc, axis=-1, keepdims=True)
            o = o + _dot(pc.astype(MM_DTYPE), vc_ref[:, sl])
        o_ref[:, sl] = (o / den).astype(MM_DTYPE)


def _attention(qn, qr, kn, krp, v, cache, *, cfg, seq, hb):
    n, hw = qn.shape
    cached = cache is not None
    blk = lambda rows: pl.BlockSpec((rows, hb * LANES), lambda b, g: (b, g))
    shared = lambda rows: pl.BlockSpec((rows, LANES), lambda b, g: (b, 0))
    ins = [qn, qr, kn, krp, v]
    in_specs = [blk(seq), blk(seq), blk(seq), shared(seq), blk(seq)]
    if cached:
        knc, vc, krc = cache
        past = knc.shape[0] // (n // seq)
        ins += [knc, krc, vc]
        in_specs += [blk(past), shared(past), blk(past)]
    return pl.pallas_call(
        functools.partial(_attn_body, hb=hb, scale=float(cfg.nope + cfg.rope) ** -0.5, cached=cached),
        out_shape=jax.ShapeDtypeStruct((n, hw), MM_DTYPE),
        grid=(n // seq, cfg.n_heads // hb),
        in_specs=in_specs,
        out_specs=blk(seq),
        compiler_params=_params(("arbitrary", "arbitrary")),
        name="attention_cached" if cached else "attention",
    )(*ins)


def _rec_scan_body(*refs, seq, cfg, has_state):
    if has_state:
        (yg_ref, xc_ref, wg_ref, bg_ref, lam_ref, h0f_ref, h0b_ref,
         o_ref, af_ref, uf_ref, ab_ref, ub_ref, hf_ref, hb_ref) = refs
    else:
        (yg_ref, xc_ref, wg_ref, bg_ref, lam_ref,
         o_ref, sf_ref, sb_ref, af_ref, uf_ref, ab_ref, ub_ref, hf_ref, hb_ref) = refs
    nseq = SUBLANES
    pitch = seq + 4
    lam = lam_ref[...]
    softplus = jnp.maximum(-lam, 0.0) + jnp.log1p(jnp.exp(-jnp.abs(lam)))
    half_decay = (-0.5 * cfg.lru_c * math.log2(math.e)) * softplus
    wg = wg_ref[...]
    bg = bg_ref[...]
    for b in range(nseq):
        xc = xc_ref[b * seq:(b + 1) * seq, :]
        th = jnp.tanh(_dot(xc.astype(MM_DTYPE), wg) + bg)
        xh = 0.5 * xc
        for d, (a_ref, u_ref) in enumerate(((af_ref, uf_ref), (ab_ref, ub_ref))):
            tx = th[:, (2 * d) * LANES:(2 * d + 1) * LANES]
            ta = th[:, (2 * d + 1) * LANES:(2 * d + 2) * LANES]
            hd = half_decay[d:d + 1, :]
            a = jnp.exp2(ta * hd + hd)
            z = 1.0 - a * a
            u = jnp.where(z > 0.0, z * lax.rsqrt(z), 0.0) * ((tx + 1.0) * xh)
            a_ref[b * pitch:b * pitch + seq, :] = a
            u_ref[b * pitch:b * pitch + seq, :] = u

    if has_state:
        hf0 = h0f_ref[...]
        hb0 = h0b_ref[...]
    else:
        hf0 = jnp.zeros((nseq, LANES), F32)
        hb0 = hf0

    def step(t, carry):
        hf, hb = carry
        fwd = pl.ds(t, nseq, stride=pitch)
        hf = af_ref[fwd, :] * hf + uf_ref[fwd, :]
        hf_ref[fwd, :] = hf
        bwd = pl.ds(seq - 1 - t, nseq, stride=pitch)
        hb = ab_ref[bwd, :] * hb + ub_ref[bwd, :]
        hb_ref[bwd, :] = hb
        return hf, hb

    hf, hb = lax.fori_loop(0, seq, step, (hf0, hb0), unroll=8)
    if not has_state:
        sf_ref[...] = hf
        sb_ref[...] = hb
    for b in range(nseq):
        hsum = hf_ref[b * pitch:b * pitch + seq, :] + hb_ref[b * pitch:b * pitch + seq, :]
        o_ref[b * seq:(b + 1) * seq, :] = (hsum * yg_ref[b * seq:(b + 1) * seq, :]).astype(MM_DTYPE)


def _rec_scan(proj, w_gates, b_gates, lam, wl, state, *, cfg, seq):
    nb2, n, _ = proj.shape
    nb = nb2 // 2
    w = nb * LANES
    assert nb == cfg.lru_blocks
    rows = SUBLANES * seq
    has_state = state is not None
    col = lambda g, c: (g, c)
    ins = [proj, proj, w_gates, b_gates, lam]
    in_specs = [
        pl.BlockSpec((None, rows, LANES), lambda g, c: (c, g, 0)),
        pl.BlockSpec((None, rows, LANES), lambda g, c: (c + nb, g, 0)),
        pl.BlockSpec((None, None, LANES, 4 * LANES), lambda g, c: (wl, c, 0, 0)),
        pl.BlockSpec((None, None, 1, 4 * LANES), lambda g, c: (wl, c, 0, 0)),
        pl.BlockSpec((None, 2, LANES), lambda g, c: (wl, 0, c)),
    ]
    out_shape = [jax.ShapeDtypeStruct((n, w), MM_DTYPE)]
    out_specs = [pl.BlockSpec((rows, LANES), col)]
    if has_state:
        ins += list(state)
        in_specs += [pl.BlockSpec((SUBLANES, LANES), col)] * 2
    else:
        nstate = n // seq
        out_shape += [jax.ShapeDtypeStruct((nstate, w), F32)] * 2
        out_specs += [pl.BlockSpec((SUBLANES, LANES), col)] * 2
    scratch = [pltpu.VMEM((SUBLANES * (seq + 4), LANES), F32)] * 6
    return pl.pallas_call(
        functools.partial(_rec_scan_body, seq=seq, cfg=cfg, has_state=has_state),
        out_shape=out_shape,
        grid=(n // rows, nb),
        in_specs=in_specs,
        out_specs=out_specs,
        scratch_shapes=scratch,
        compiler_params=_params(("arbitrary", "arbitrary")),
        name="rec_scan_state" if has_state else "rec_scan",
    )(*ins)


def _swap_halves(w, rope):
    q = rope // 4
    return jnp.concatenate([w[..., q:2 * q], w[..., :q], w[..., 3 * q:], w[..., 2 * q:3 * q]], axis=-1)


def _pad_lanes(w):
    return jnp.pad(w, [(0, 0)] * (w.ndim - 1) + [(0, LANES - w.shape[-1])])


def _rope_tables(seq, cfg):
    rows = seq // cfg.grid_w
    row = jnp.repeat(jnp.arange(rows), cfg.grid_w).astype(F32)
    col = jnp.tile(jnp.arange(cfg.grid_w), rows).astype(F32)
    half = cfg.rope // 2
    inv = 1.0 / (cfg.rope_base ** (jnp.arange(0, half, 2, dtype=F32) / half))
    ar, ac = row[:, None] * inv, col[:, None] * inv
    cos = jnp.concatenate([jnp.cos(ar), jnp.cos(ar), jnp.cos(ac), jnp.cos(ac)], axis=-1)
    sin = jnp.concatenate([-jnp.sin(ar), jnp.sin(ar), -jnp.sin(ac), jnp.sin(ac)], axis=-1)
    return _pad_lanes(cos), _pad_lanes(sin)


def _pick(n, pref):
    return pref if n % pref == 0 else n


def _forward(cfg, x_prompt, x_sample, cache_ckv, cache_krope, state_lru, c, c_ctx,
             g_mix, g_ffn, g_final, w_ada, b_ada,
             w_mla_in, g_mla_q, g_mla_kv, w_mla_uq, w_mla_uk, w_mla_uv, w_mla_o,
             w_rec_in, w_rec_conv, b_rec_conv, w_rec_gx, b_rec_gx, w_rec_ga, b_rec_ga,
             rec_lambda, w_rec_out,
             w_ffn_up, w_ffn_conv, b_ffn_conv, w_ffn_down):
    depth, d = g_mix.shape
    bp, sp, _ = x_prompt.shape
    bs, ss, _ = x_sample.shape
    assert cfg.nope == LANES and cfg.v_dim == LANES and cfg.rope <= LANES and bs + 1 <= MOD_ROWS
    assert bp % SUBLANES == 0 and bs % SUBLANES == 0
    eps = cfg.eps
    cast = lambda w: w.astype(MM_DTYPE)
    row3 = lambda v: v[:, None, :]

    cond = jnp.zeros((MOD_ROWS, d), F32).at[0].set(c_ctx).at[1:1 + bs].set(c)
    mod = _adaln(cond, w_ada, b_ada, _pick(w_ada.shape[-1], 1024)).reshape(depth, MOD_ROWS, N_MOD, d)

    ql, kl, r, nh = cfg.q_lora, cfg.kv_lora, cfg.rope, cfg.n_heads
    w_kr = w_mla_in[..., ql + kl:]
    w_in_p = cast(jnp.concatenate(
        [w_mla_in[..., :ql + kl], _pad_lanes(w_kr), _pad_lanes(_swap_halves(w_kr, r))], axis=-1))
    uq = w_mla_uq.reshape(w_mla_uq.shape[0], ql, nh, cfg.nope + r)
    flat = lambda w: cast(w.reshape(w.shape[0], ql, nh * LANES))
    w_qn, w_qr, w_qs = flat(uq[..., :cfg.nope]), flat(_pad_lanes(uq[..., cfg.nope:])), flat(
        _pad_lanes(_swap_halves(uq[..., cfg.nope:], r)))
    w_uk, w_uv, w_o = cast(w_mla_uk), cast(w_mla_uv), cast(w_mla_o)
    g_q, g_kv = row3(g_mla_q), row3(g_mla_kv)
    cos, sin = _rope_tables(ss, cfg)

    w_gates = cast(0.5 * jnp.concatenate([w_rec_gx[:, 0], w_rec_ga[:, 0], w_rec_gx[:, 1], w_rec_ga[:, 1]], axis=-1))
    nb = cfg.lru_blocks
    blk = lambda b: b.reshape(b.shape[0], nb, 1, LANES)
    b_gates = 0.5 * jnp.concatenate(
        [blk(b_rec_gx[:, 0]), blk(b_rec_ga[:, 0]), blk(b_rec_gx[:, 1]), blk(b_rec_ga[:, 1])], axis=-1)
    tf = _pick(w_ffn_down.shape[1], 512)
    w_rin = cast(_col_tiles(w_rec_in, _pick(w_rec_in.shape[-1] // 2, 1024)))
    w_up = cast(_col_tiles(w_ffn_up, tf))
    w_rout, w_down = cast(w_rec_out), cast(w_ffn_down)
    g_mix3, g_ffn3, b_rconv, b_fconv = row3(g_mix), row3(g_ffn), row3(b_rec_conv), row3(b_ffn_conv)

    ctx_row = lambda i, tm: 0

    def run_group(x, seq, mod_row, sample):
        n = x.shape[0]
        rows = lambda pref: min(seq, pref) if sample else _pick(n, pref)
        tm_ffn = seq if sample else max(seq, _pick(n, 1024))
        ckv_new, kr_new, lru_new = [], [], []
        for layer in range(depth):
            j = layer // 2
            if layer % 2 == 0:
                proj = _norm_mod_matmul(x, mod, mod_row, g_mix3, w_in_p, layer, j, tm=rows(1024),
                                        tn=w_in_p.shape[-1], eps=eps)
                if sample:
                    qn, qr, kn, v, krp = _mla_mid(proj, g_q, g_kv, w_qn, w_qr, w_qs, w_uk, w_uv, j, cos, sin,
                                                  cfg=cfg, seq=seq, tm=rows(256))
                    cache = _cache_expand(cache_ckv, cache_krope, j, w_uk, w_uv, cfg=cfg)
                    att = _attention(qn, qr, kn, krp, v, cache, cfg=cfg, seq=seq, hb=min(2, nh))
                else:
                    qn, qr, kn, v, krp, ckv, kr = _mla_mid(proj, g_q, g_kv, w_qn, w_qr, None, w_uk, w_uv, j,
                                                           None, None, cfg=cfg, seq=seq, tm=rows(256))
                    ckv_new.append(ckv)
                    kr_new.append(kr)
                    att = _attention(qn, qr, kn, krp, v, None, cfg=cfg, seq=seq, hb=nh)
                x = _matmul_residual(att, w_o, j, x, mod, layer, mod_row, tm=rows(512))
            else:
                proj = _rec_in(x, mod, mod_row, g_mix3, w_rin, w_rec_conv, b_rconv, layer, j, seq=seq,
                               tm=max(seq, rows(1024)), eps=eps)
                outs = _rec_scan(proj, w_gates, b_gates, rec_lambda, j,
                                 (state_lru[:, j, 0], state_lru[:, j, 1]) if sample else None, cfg=cfg, seq=seq)
                if not sample:
                    lru_new.append(jnp.stack(outs[1:], axis=1))
                x = _matmul_residual(outs[0], w_rout, j, x, mod, layer, mod_row, tm=rows(512))
            x = _conv_ffn(x, mod, layer, mod_row, g_ffn3, w_up, w_ffn_conv, b_fconv, w_down,
                          g_final[None] if layer == depth - 1 else None, seq=seq, tm=tm_ffn,
                          n_sub=2 if tf % (2 * LANES) == 0 else 1, x_buffers=1 if tm_ffn > 512 else 2, eps=eps)
        return x, ckv_new, kr_new, lru_new

    y_p, ckv_new, kr_new, lru_new = run_group(x_prompt.reshape(bp * sp, d), sp, ctx_row, False)
    y_s, _, _, _ = run_group(x_sample.reshape(bs * ss, d), ss, lambda i, tm: 1 + (i * tm) // ss, True)
    new_ckv = jnp.stack([t.reshape(bp, sp, kl) for t in ckv_new], axis=1)
    new_kr = jnp.stack([t.reshape(bp, sp, r) for t in kr_new], axis=1)
    new_lru = jnp.stack(lru_new, axis=1)
    return y_p.reshape(bp, sp, d), y_s.reshape(bs, ss, d), new_ckv, new_kr, new_lru


CFG = Cfg(n_heads=16, q_lora=512, kv_lora=512, nope=128, rope=64, v_dim=128, grid_w=64, rope_base=10000.0,
          lru_blocks=16, lru_c=8.0, eps=1e-6)


def kernel(x_prompt, x_sample, cache_ckv, cache_krope, state_lru, c, c_ctx, g_mix, g_ffn, g_final, w_ada, b_ada, w_mla_in, g_mla_q, g_mla_kv, w_mla_uq, w_mla_uk, w_mla_uv, w_mla_o, w_rec_in, w_rec_conv, b_rec_conv, w_rec_gx, b_rec_gx, w_rec_ga, b_rec_ga, rec_lambda, w_rec_out, w_ffn_up, w_ffn_conv, b_ffn_conv, w_ffn_down):
    return _forward(CFG, x_prompt, x_sample, cache_ckv, cache_krope, state_lru, c, c_ctx, g_mix, g_ffn, g_final,
                    w_ada, b_ada, w_mla_in, g_mla_q, g_mla_kv, w_mla_uq, w_mla_uk, w_mla_uv, w_mla_o,
                    w_rec_in, w_rec_conv, b_rec_conv, w_rec_gx, b_rec_gx, w_rec_ga, b_rec_ga, rec_lambda, w_rec_out,
                    w_ffn_up, w_ffn_conv, b_ffn_conv, w_ffn_down)
```

```python
import functools
import math
from typing import NamedTuple

import jax
import jax.numpy as jnp
from jax import lax
from jax.experimental import pallas as pl
from jax.experimental.pallas import tpu as pltpu

F32 = jnp.float32
MM_DTYPE = jnp.bfloat16

LANES = 128
SUBLANES = 8
VMEM_LIMIT_BYTES = 60 * 1024 * 1024
N_MOD = 6
MOD_ROWS = 16


class Cfg(NamedTuple):
    n_heads: int
    q_lora: int
    kv_lora: int
    nope: int
    rope: int
    v_dim: int
    grid_w: int
    rope_base: float
    lru_blocks: int
    lru_c: float
    eps: float


def _params(sem):
    return pltpu.CompilerParams(dimension_semantics=sem, vmem_limit_bytes=VMEM_LIMIT_BYTES)


def _dot(a, b):
    return jnp.dot(a, b, preferred_element_type=F32)


def _dot_nt(a, b):
    return lax.dot_general(a, b, (((1,), (1,)), ((), ())), preferred_element_type=F32)


def _const_spec(shape):
    nd = len(shape)
    return pl.BlockSpec(shape, lambda *_: (0,) * nd)


def _layer_spec(arr, layer):
    nd = arr.ndim
    return pl.BlockSpec((None,) + arr.shape[1:], lambda *_: (layer,) + (0,) * (nd - 1))


def _mod_spec(d, layer, mod_row, tm):
    return pl.BlockSpec((None, None, N_MOD, d), lambda i, *_: (layer, mod_row(i, tm), 0, 0))


def _rms(x, g, eps):
    return x * lax.rsqrt(jnp.mean(x * x, axis=-1, keepdims=True) + eps) * g


def _norm_mod(x, g, mod_ref, shift_idx, scale_idx, eps):
    y = _rms(x, g, eps)
    return y * (1.0 + mod_ref[scale_idx:scale_idx + 1, :]) + mod_ref[shift_idx:shift_idx + 1, :]


ROW_PAD = SUBLANES


def _zero_row_pads(u_ref, rows):
    lead = (slice(None),) * (len(u_ref.shape) - 2)
    zeros = jnp.zeros(u_ref.shape[:-2] + (ROW_PAD, u_ref.shape[-1]), u_ref.dtype)
    u_ref[lead + (slice(0, ROW_PAD), slice(None))] = zeros
    u_ref[lead + (slice(ROW_PAD + rows, ROW_PAD + rows + ROW_PAD), slice(None))] = zeros


def _store_slabs(u_ref, first, u):
    for k in range(u.shape[1] // LANES):
        u_ref[first + k, ROW_PAD:ROW_PAD + u.shape[0], :] = u[:, k * LANES:(k + 1) * LANES]


def _seq_window(u_ref, off, rows, seq):
    w = u_ref[ROW_PAD + off:ROW_PAD + off + rows, :]
    if off == 0 or seq == rows:
        return w
    sub_row = lax.broadcasted_iota(jnp.int32, (SUBLANES, 1), 0)
    pieces = []
    for base in range(0, rows, seq):
        if off < 0:
            tile, bad = base, sub_row < -off
        else:
            tile, bad = base + seq - SUBLANES, sub_row >= SUBLANES - off
        pieces += [w[base:tile], jnp.where(bad, 0.0, w[tile:tile + SUBLANES]), w[tile + SUBLANES:base + seq]]
    return jnp.concatenate([p for p in pieces if p.shape[0]], axis=0)


def _adaln_body(cond_ref, w_ref, b_ref, o_ref):
    c = cond_ref[...]
    s = (c * jax.nn.sigmoid(c)).astype(MM_DTYPE)
    o_ref[...] = _dot(s, w_ref[...].astype(MM_DTYPE)) + b_ref[...]


def _adaln(cond, w_ada, b_ada, tn):
    n_layers, d, n_out = w_ada.shape
    return pl.pallas_call(
        _adaln_body,
        out_shape=jax.ShapeDtypeStruct((n_layers, MOD_ROWS, n_out), F32),
        grid=(n_layers, n_out // tn),
        in_specs=[
            _const_spec((MOD_ROWS, d)),
            pl.BlockSpec((None, d, tn), lambda l, n: (l, 0, n)),
            pl.BlockSpec((None, 1, tn), lambda l, n: (l, 0, n)),
        ],
        out_specs=pl.BlockSpec((None, MOD_ROWS, tn), lambda l, n: (l, 0, n)),
        compiler_params=_params(("arbitrary", "arbitrary")),
        name="adaln",
    )(cond, w_ada, b_ada.reshape(n_layers, 1, n_out))


def _nmm_body(x_ref, mod_ref, g_ref, w_ref, o_ref, h_ref, *, eps):
    @pl.when(pl.program_id(1) == 0)
    def _():
        h = _norm_mod(x_ref[...], g_ref[...], mod_ref, 0, 1, eps)
        h_ref[...] = h.astype(MM_DTYPE)

    o_ref[...] = _dot(h_ref[...], w_ref[...])


def _norm_mod_matmul(x, mod, mod_row, g, w, layer, wl, *, tm, tn, eps):
    n, d = x.shape
    m = w.shape[2]
    return pl.pallas_call(
        functools.partial(_nmm_body, eps=eps),
        out_shape=jax.ShapeDtypeStruct((n, m), F32),
        grid=(n // tm, m // tn),
        in_specs=[
            pl.BlockSpec((tm, d), lambda i, j: (i, 0)),
            _mod_spec(d, layer, mod_row, tm),
            _layer_spec(g, layer),
            pl.BlockSpec((None, d, tn), lambda i, j: (wl, 0, j)),
        ],
        out_specs=pl.BlockSpec((tm, tn), lambda i, j: (i, j)),
        scratch_shapes=[pltpu.VMEM((tm, d), MM_DTYPE)],
        compiler_params=_params(("arbitrary", "arbitrary")),
        name="norm_mod_matmul",
    )(x, mod, g, w)


def _rec_in_body(x_ref, mod_ref, g_ref, w_ref, cw_ref, cb_ref, o_ref, h_ref, u_ref, *, seq, n_y, eps):
    j = pl.program_id(1)
    tm = h_ref.shape[0]

    @pl.when(j == 0)
    def _():
        h = _norm_mod(x_ref[...], g_ref[...], mod_ref, 0, 1, eps)
        h_ref[...] = h.astype(MM_DTYPE)
        _zero_row_pads(u_ref, tm)

    @pl.when(j < n_y)
    def _():
        y = jax.nn.gelu(_dot(h_ref[...], w_ref[...]))
        for c in range(o_ref.shape[0]):
            o_ref[c] = y[:, c * LANES:(c + 1) * LANES]

    @pl.when(j >= n_y)
    def _():
        _store_slabs(u_ref, 0, _dot(h_ref[...], w_ref[...]))
        for c in range(u_ref.shape[0]):
            sl = slice(c * LANES, (c + 1) * LANES)
            xc = cb_ref[:, sl]
            for tap, off in enumerate((-2, -1, 0, 1)):
                xc = xc + _seq_window(u_ref.at[c], off, tm, seq) * cw_ref[tap:tap + 1, sl]
            o_ref[c] = xc


def _rec_in(x, mod, mod_row, g, w, w_conv, b_conv, layer, wl, *, seq, tm, tn, eps):
    n, d = x.shape
    m = w.shape[2]
    n_y = m // 2 // tn
    assert tm % seq == 0 and w_conv.shape[1] == 4 and m % (2 * tn) == 0
    conv_col = lambda i, j: (wl, 0, jnp.maximum(j - n_y, 0))
    return pl.pallas_call(
        functools.partial(_rec_in_body, seq=seq, n_y=n_y, eps=eps),
        out_shape=jax.ShapeDtypeStruct((m // LANES, n, LANES), F32),
        grid=(n // tm, m // tn),
        in_specs=[
            pl.BlockSpec((tm, d), lambda i, j: (i, 0)),
            _mod_spec(d, layer, mod_row, tm),
            _layer_spec(g, layer),
            pl.BlockSpec((None, d, tn), lambda i, j: (wl, 0, j)),
            pl.BlockSpec((None, 4, tn), conv_col),
            pl.BlockSpec((None, 1, tn), conv_col),
        ],
        out_specs=pl.BlockSpec((tn // LANES, tm, LANES), lambda i, j: (j, i, 0)),
        scratch_shapes=[pltpu.VMEM((tm, d), MM_DTYPE), pltpu.VMEM((tn // LANES, ROW_PAD + tm + ROW_PAD, LANES), F32)],
        compiler_params=_params(("arbitrary", "arbitrary")),
        name="rec_in",
    )(x, mod, g, w, w_conv, b_conv)


def _mmres_body(a_ref, w_ref, x_ref, mod_ref, o_ref):
    o_ref[...] = x_ref[...] + mod_ref[2:3, :] * _dot(a_ref[...], w_ref[...])


def _matmul_residual(a, w, wl, x, mod, layer, mod_row, *, tm):
    n, k = a.shape
    d = w.shape[2]
    return pl.pallas_call(
        _mmres_body,
        out_shape=jax.ShapeDtypeStruct((n, d), F32),
        grid=(n // tm,),
        in_specs=[
            pl.BlockSpec((tm, k), lambda i: (i, 0)),
            _layer_spec(w, wl),
            pl.BlockSpec((tm, d), lambda i: (i, 0)),
            _mod_spec(d, layer, mod_row, tm),
        ],
        out_specs=pl.BlockSpec((tm, d), lambda i: (i, 0)),
        compiler_params=_params(("arbitrary",)),
        name="matmul_residual",
    )(a, w, x, mod)


def _ffn_body(*refs, seq, eps, n_sub, final):
    if final:
        (x_ref, mod_ref, g_ref, wg_ref, wv_ref, cwg_ref, cwv_ref, cbg_ref, cbv_ref, wd_ref, gf_ref,
         o_ref, h_ref, act_ref, ug_ref, uv_ref) = refs
    else:
        (x_ref, mod_ref, g_ref, wg_ref, wv_ref, cwg_ref, cwv_ref, cbg_ref, cbv_ref, wd_ref,
         o_ref, h_ref, act_ref, ug_ref, uv_ref) = refs
    j = pl.program_id(1)
    n_tiles = pl.num_programs(1) - 1
    tm = h_ref.shape[0]
    ts = wg_ref.shape[1] // n_sub
    subs = [slice(s * ts, (s + 1) * ts) for s in range(n_sub)]

    def conv(u_ref, c, cw, cb):
        win = lambda off: _seq_window(u_ref.at[c], off, tm, seq)
        return cb + win(-1) * cw[0:1, :] + win(0) * cw[1:2, :] + win(1) * cw[2:3, :]

    def up():
        h = h_ref[...]
        for s, sl in enumerate(subs):
            _store_slabs(ug_ref, s * (ts // LANES), _dot(h, wg_ref[:, sl]))
            _store_slabs(uv_ref, s * (ts // LANES), _dot(h, wv_ref[:, sl]))

    def activate():
        for c in range(ug_ref.shape[0]):
            sl = slice(c * LANES, (c + 1) * LANES)
            hg = conv(ug_ref, c, 0.5 * cwg_ref[:, sl], 0.5 * cbg_ref[:, sl])
            val = conv(uv_ref, c, cwv_ref[:, sl], cbv_ref[:, sl])
            act_ref[:, sl] = (hg * (jnp.tanh(hg) + 1.0) * val).astype(MM_DTYPE)

    def down(act):
        o_ref[...] += _dot(act, wd_ref[...])

    @pl.when(j == 0)
    def _():
        h = _norm_mod(x_ref[...], g_ref[...], mod_ref, 3, 4, eps)
        h_ref[...] = h.astype(MM_DTYPE)
        o_ref[...] = jnp.zeros_like(o_ref)
        for u_ref in (ug_ref, uv_ref):
            _zero_row_pads(u_ref, tm)
        up()
        activate()

    @pl.when((j > 0) & (j < n_tiles))
    def _():
        down(act_ref[...])
        up()
        activate()

    @pl.when(j == n_tiles)
    def _():
        down(act_ref[...])
        y = x_ref[...] + mod_ref[5:6, :] * o_ref[...]
        if final:
            y = _rms(y, gf_ref[...], eps)
        o_ref[...] = y


def _conv_ffn(x, mod, layer, mod_row, g, w_up, w_conv, b_conv, w_down, g_final, *, seq, tm, tf, n_sub, x_buffers,
              eps):
    n, d = x.shape
    f = w_down.shape[1]
    assert w_conv.shape[1:] == (3, 2 * f) and tm % seq == 0 and f % tf == 0 and tf % (n_sub * LANES) == 0
    nf = f // tf
    final = g_final is not None
    up_tile = lambda j: jnp.minimum(j, nf - 1)
    lo = lambda rows: pl.BlockSpec((None, rows, tf), lambda i, j: (layer, 0, up_tile(j)))
    hi = lambda rows: pl.BlockSpec((None, rows, tf), lambda i, j: (layer, 0, up_tile(j) + nf))
    ins = [x, mod, g, w_up, w_up, w_conv, w_conv, b_conv, b_conv, w_down] + ([g_final] if final else [])
    in_specs = [
        pl.BlockSpec((tm, d), lambda i, j: (i, 0), pipeline_mode=pl.Buffered(x_buffers)),
        _mod_spec(d, layer, mod_row, tm),
        _layer_spec(g, layer),
        lo(d), hi(d), lo(3), hi(3), lo(1), hi(1),
        pl.BlockSpec((None, tf, d), lambda i, j: (layer, jnp.maximum(j - 1, 0), 0)),
    ] + ([_const_spec(g_final.shape)] if final else [])
    return pl.pallas_call(
        functools.partial(_ffn_body, seq=seq, eps=eps, n_sub=n_sub, final=final),
        out_shape=jax.ShapeDtypeStruct((n, d), F32),
        grid=(n // tm, nf + 1),
        in_specs=in_specs,
        out_specs=pl.BlockSpec((tm, d), lambda i, j: (i, 0)),
        scratch_shapes=[pltpu.VMEM((tm, d), MM_DTYPE), pltpu.VMEM((tm, tf), MM_DTYPE)]
        + [pltpu.VMEM((tf // LANES, ROW_PAD + tm + ROW_PAD, LANES), F32)] * 2,
        compiler_params=_params(("arbitrary", "arbitrary")),
        name="conv_ffn_final" if final else "conv_ffn",
    )(*ins)


def _mla_mid_body(*refs, cfg, rope):
    if rope:
        (proj_ref, gq_ref, gkv_ref, wqn_ref, wqr_ref, wqs_ref, wuk_ref, wuv_ref, cos_ref, sin_ref,
         qn_ref, qr_ref, kn_ref, v_ref, krp_ref) = refs
    else:
        (proj_ref, gq_ref, gkv_ref, wqn_ref, wqr_ref, wuk_ref, wuv_ref,
         qn_ref, qr_ref, kn_ref, v_ref, krp_ref, ckv_ref, kr_ref) = refs
    ql, kl = cfg.q_lora, cfg.kv_lora
    p = proj_ref[...]
    cqn = _rms(p[:, :ql], gq_ref[...], cfg.eps).astype(MM_DTYPE)
    ckv = _rms(p[:, ql:ql + kl], gkv_ref[...], cfg.eps)
    krp = p[:, ql + kl:ql + kl + LANES]
    qn_ref[...] = _dot(cqn, wqn_ref[...]).astype(MM_DTYPE)
    qr = _dot(cqn, wqr_ref[...])
    if rope:
        krs = p[:, ql + kl + LANES:ql + kl + 2 * LANES]
        qs = _dot(cqn, wqs_ref[...])
        cos = cos_ref[...]
        sin = sin_ref[...]
        for h in range(cfg.n_heads):
            sl = slice(h * LANES, (h + 1) * LANES)
            qr_ref[:, sl] = (qr[:, sl] * cos + qs[:, sl] * sin).astype(MM_DTYPE)
        krp_ref[...] = (krp * cos + krs * sin).astype(MM_DTYPE)
    else:
        qr_ref[...] = qr.astype(MM_DTYPE)
        krp_ref[...] = krp.astype(MM_DTYPE)
        ckv_ref[...] = ckv
        kr_ref[...] = krp[:, :cfg.rope]
    ckv_b = ckv.astype(MM_DTYPE)
    kn_ref[...] = _dot(ckv_b, wuk_ref[...]).astype(MM_DTYPE)
    v_ref[...] = _dot(ckv_b, wuv_ref[...]).astype(MM_DTYPE)


def _mla_mid(proj, g_q, g_kv, w_qn, w_qr, w_qs, w_uk, w_uv, wl, cos, sin, *, cfg, seq, tm):
    n, pw = proj.shape
    rope = cos is not None
    hw = cfg.n_heads * LANES
    row = lambda i: (i, 0)
    ws = [g_q, g_kv, w_qn, w_qr] + ([w_qs] if rope else []) + [w_uk, w_uv]
    ins = [proj] + ws
    in_specs = [pl.BlockSpec((tm, pw), row)] + [_layer_spec(w, wl) for w in ws]
    out_shape = [jax.ShapeDtypeStruct((n, hw), MM_DTYPE)] * 4 + [jax.ShapeDtypeStruct((n, LANES), MM_DTYPE)]
    out_specs = [pl.BlockSpec((tm, hw), row)] * 4 + [pl.BlockSpec((tm, LANES), row)]
    if rope:
        bps = seq // tm
        ins += [cos, sin]
        in_specs += [pl.BlockSpec((tm, LANES), lambda i: (i % bps, 0))] * 2
    else:
        out_shape += [jax.ShapeDtypeStruct((n, cfg.kv_lora), F32), jax.ShapeDtypeStruct((n, cfg.rope), F32)]
        out_specs += [pl.BlockSpec((tm, cfg.kv_lora), row), pl.BlockSpec((tm, cfg.rope), row)]
    return pl.pallas_call(
        functools.partial(_mla_mid_body, cfg=cfg, rope=rope),
        out_shape=out_shape,
        grid=(n // tm,),
        in_specs=in_specs,
        out_specs=out_specs,
        compiler_params=_params(("arbitrary",)),
        name="mla_mid_rope" if rope else "mla_mid",
    )(*ins)


def _cache_expand_body(ckv_ref, kr_ref, wuk_ref, wuv_ref, kn_ref, v_ref, krp_ref, *, rope):
    c = ckv_ref[...].astype(MM_DTYPE)
    kn_ref[...] = _dot(c, wuk_ref[...]).astype(MM_DTYPE)
    v_ref[...] = _dot(c, wuv_ref[...]).astype(MM_DTYPE)
    krp_ref[:, :rope] = kr_ref[...].astype(MM_DTYPE)
    krp_ref[:, rope:] = jnp.zeros((krp_ref.shape[0], LANES - rope), MM_DTYPE)


def _cache_expand(cache_ckv, cache_krope, wl, w_uk, w_uv, *, cfg):
    b, _, p, kl = cache_ckv.shape
    hw = cfg.n_heads * LANES
    row = lambda i: (i, 0)
    return pl.pallas_call(
        functools.partial(_cache_expand_body, rope=cfg.rope),
        out_shape=[jax.ShapeDtypeStruct((b * p, hw), MM_DTYPE)] * 2 + [jax.ShapeDtypeStruct((b * p, LANES), MM_DTYPE)],
        grid=(b,),
        in_specs=[
            pl.BlockSpec((None, None, p, kl), lambda i: (i, wl, 0, 0)),
            pl.BlockSpec((None, None, p, cfg.rope), lambda i: (i, wl, 0, 0)),
            _layer_spec(w_uk, wl),
            _layer_spec(w_uv, wl),
        ],
        out_specs=[pl.BlockSpec((p, hw), row)] * 2 + [pl.BlockSpec((p, LANES), row)],
        compiler_params=_params(("arbitrary",)),
        name="cache_expand",
    )(cache_ckv, cache_krope, w_uk, w_uv)


def _attn_body(*refs, hb, scale, cached, q_chunk):
    if cached:
        qn_ref, qr_ref, kn_ref, kr_ref, v_ref, knc_ref, krc_ref, vc_ref, o_ref = refs
    else:
        qn_ref, qr_ref, kn_ref, kr_ref, v_ref, o_ref = refs
    sq = qn_ref.shape[0]
    qc = min(sq, q_chunk)
    for h in range(hb):
        sl = slice(h * LANES, (h + 1) * LANES)
        k = jnp.concatenate([kn_ref[:, sl], kr_ref[...]], axis=1)
        if cached:
            kc = jnp.concatenate([knc_ref[:, sl], krc_ref[...]], axis=1)
        for r0 in range(0, sq, qc):
            rows = slice(r0, r0 + qc)
            q = jnp.concatenate([qn_ref[rows, sl], qr_ref[rows, sl]], axis=1)
            s = _dot_nt(q, k) * scale
            m = jnp.max(s, axis=-1, keepdims=True)
            if cached:
                sc = _dot_nt(q, kc) * scale
                m = jnp.maximum(m, jnp.max(sc, axis=-1, keepdims=True))
                pc = jnp.exp(sc - m)
            p = jnp.exp(s - m)
            den = jnp.sum(p, axis=-1, keepdims=True)
            o = _dot(p.astype(MM_DTYPE), v_ref[:, sl])
            if cached:
                den = den + jnp.sum(pc, axis=-1, keepdims=True)
                o = o + _dot(pc.astype(MM_DTYPE), vc_ref[:, sl])
            o_ref[rows, sl] = (o / den).astype(MM_DTYPE)


def _attention(qn, qr, kn, krp, v, cache, *, cfg, seq, hb):
    n, hw = qn.shape
    cached = cache is not None
    blk = lambda rows: pl.BlockSpec((rows, hb * LANES), lambda b, g: (b, g))
    shared = lambda rows: pl.BlockSpec((rows, LANES), lambda b, g: (b, 0))
    ins = [qn, qr, kn, krp, v]
    in_specs = [blk(seq), blk(seq), blk(seq), shared(seq), blk(seq)]
    if cached:
        knc, vc, krc = cache
        past = knc.shape[0] // (n // seq)
        ins += [knc, krc, vc]
        in_specs += [blk(past), shared(past), blk(past)]
    return pl.pallas_call(
        functools.partial(_attn_body, hb=hb, scale=float(cfg.nope + cfg.rope) ** -0.5, cached=cached, q_chunk=512),
        out_shape=jax.ShapeDtypeStruct((n, hw), MM_DTYPE),
        grid=(n // seq, cfg.n_heads // hb),
        in_specs=in_specs,
        out_specs=blk(seq),
        compiler_params=_params(("arbitrary", "arbitrary")),
        name="attention_cached" if cached else "attention",
    )(*ins)


def _rec_scan_body(*refs, seq, cfg, has_state):
    if has_state:
        (yg_ref, xc_ref, wg_ref, bg_ref, lam_ref, h0f_ref, h0b_ref,
         o_ref, af_ref, uf_ref, ab_ref, ub_ref, hf_ref, hb_ref) = refs
    else:
        (yg_ref, xc_ref, wg_ref, bg_ref, lam_ref,
         o_ref, sf_ref, sb_ref, af_ref, uf_ref, ab_ref, ub_ref, hf_ref, hb_ref) = refs
    nseq = SUBLANES
    pitch = seq + 4
    lam = lam_ref[...]
    softplus = jnp.maximum(-lam, 0.0) + jnp.log1p(jnp.exp(-jnp.abs(lam)))
    half_decay = (-0.5 * cfg.lru_c * math.log2(math.e)) * softplus
    wg = wg_ref[...]
    bg = bg_ref[...]
    for b in range(nseq):
        xc = xc_ref[b * seq:(b + 1) * seq, :]
        th = jnp.tanh(_dot(xc.astype(MM_DTYPE), wg) + bg)
        xh = 0.5 * xc
        for d, (a_ref, u_ref) in enumerate(((af_ref, uf_ref), (ab_ref, ub_ref))):
            tx = th[:, (2 * d) * LANES:(2 * d + 1) * LANES]
            ta = th[:, (2 * d + 1) * LANES:(2 * d + 2) * LANES]
            hd = half_decay[d:d + 1, :]
            a = jnp.exp2(ta * hd + hd)
            z = 1.0 - a * a
            u = jnp.where(z > 0.0, z * lax.rsqrt(z), 0.0) * ((tx + 1.0) * xh)
            a_ref[b * pitch:b * pitch + seq, :] = a
            u_ref[b * pitch:b * pitch + seq, :] = u

    if has_state:
        hf0 = h0f_ref[...]
        hb0 = h0b_ref[...]
    else:
        hf0 = jnp.zeros((nseq, LANES), F32)
        hb0 = hf0

    def step(t, carry):
        hf, hb = carry
        fwd = pl.ds(t, nseq, stride=pitch)
        hf = af_ref[fwd, :] * hf + uf_ref[fwd, :]
        hf_ref[fwd, :] = hf
        bwd = pl.ds(seq - 1 - t, nseq, stride=pitch)
        hb = ab_ref[bwd, :] * hb + ub_ref[bwd, :]
        hb_ref[bwd, :] = hb
        return hf, hb

    hf, hb = lax.fori_loop(0, seq, step, (hf0, hb0), unroll=8)
    if not has_state:
        sf_ref[...] = hf
        sb_ref[...] = hb
    for b in range(nseq):
        hsum = hf_ref[b * pitch:b * pitch + seq, :] + hb_ref[b * pitch:b * pitch + seq, :]
        o_ref[b * seq:(b + 1) * seq, :] = (hsum * yg_ref[b * seq:(b + 1) * seq, :]).astype(MM_DTYPE)


def _rec_scan(proj, w_gates, b_gates, lam, wl, state, *, cfg, seq):
    nb2, n, _ = proj.shape
    nb = nb2 // 2
    w = nb * LANES
    assert nb == cfg.lru_blocks
    rows = SUBLANES * seq
    has_state = state is not None
    col = lambda g, c: (g, c)
    ins = [proj, proj, w_gates, b_gates, lam]
    in_specs = [
        pl.BlockSpec((None, rows, LANES), lambda g, c: (c, g, 0)),
        pl.BlockSpec((None, rows, LANES), lambda g, c: (c + nb, g, 0)),
        pl.BlockSpec((None, None, LANES, 4 * LANES), lambda g, c: (wl, c, 0, 0)),
        pl.BlockSpec((None, None, 1, 4 * LANES), lambda g, c: (wl, c, 0, 0)),
        pl.BlockSpec((None, 2, LANES), lambda g, c: (wl, 0, c)),
    ]
    out_shape = [jax.ShapeDtypeStruct((n, w), MM_DTYPE)]
    out_specs = [pl.BlockSpec((rows, LANES), col)]
    if has_state:
        ins += list(state)
        in_specs += [pl.BlockSpec((SUBLANES, LANES), col)] * 2
    else:
        nstate = n // seq
        out_shape += [jax.ShapeDtypeStruct((nstate, w), F32)] * 2
        out_specs += [pl.BlockSpec((SUBLANES, LANES), col)] * 2
    scratch = [pltpu.VMEM((SUBLANES * (seq + 4), LANES), F32)] * 6
    return pl.pallas_call(
        functools.partial(_rec_scan_body, seq=seq, cfg=cfg, has_state=has_state),
        out_shape=out_shape,
        grid=(n // rows, nb),
        in_specs=in_specs,
        out_specs=out_specs,
        scratch_shapes=scratch,
        compiler_params=_params(("arbitrary", "arbitrary")),
        name="rec_scan_state" if has_state else "rec_scan",
    )(*ins)


def _swap_halves(w, rope):
    q = rope // 4
    return jnp.concatenate([w[..., q:2 * q], w[..., :q], w[..., 3 * q:], w[..., 2 * q:3 * q]], axis=-1)


def _pad_lanes(w):
    return jnp.pad(w, [(0, 0)] * (w.ndim - 1) + [(0, LANES - w.shape[-1])])


def _rope_tables(seq, cfg):
    rows = seq // cfg.grid_w
    row = jnp.repeat(jnp.arange(rows), cfg.grid_w).astype(F32)
    col = jnp.tile(jnp.arange(cfg.grid_w), rows).astype(F32)
    half = cfg.rope // 2
    inv = 1.0 / (cfg.rope_base ** (jnp.arange(0, half, 2, dtype=F32) / half))
    ar, ac = row[:, None] * inv, col[:, None] * inv
    cos = jnp.concatenate([jnp.cos(ar), jnp.cos(ar), jnp.cos(ac), jnp.cos(ac)], axis=-1)
    sin = jnp.concatenate([-jnp.sin(ar), jnp.sin(ar), -jnp.sin(ac), jnp.sin(ac)], axis=-1)
    return _pad_lanes(cos), _pad_lanes(sin)


def _pick(n, pref):
    return pref if n % pref == 0 else n


def _forward(cfg, x_prompt, x_sample, cache_ckv, cache_krope, state_lru, c, c_ctx,
             g_mix, g_ffn, g_final, w_ada, b_ada,
             w_mla_in, g_mla_q, g_mla_kv, w_mla_uq, w_mla_uk, w_mla_uv, w_mla_o,
             w_rec_in, w_rec_conv, b_rec_conv, w_rec_gx, b_rec_gx, w_rec_ga, b_rec_ga,
             rec_lambda, w_rec_out,
             w_ffn_up, w_ffn_conv, b_ffn_conv, w_ffn_down):
    depth, d = g_mix.shape
    bp, sp, _ = x_prompt.shape
    bs, ss, _ = x_sample.shape
    assert cfg.nope == LANES and cfg.v_dim == LANES and cfg.rope <= LANES and bs + 1 <= MOD_ROWS
    assert bp % SUBLANES == 0 and bs % SUBLANES == 0
    eps = cfg.eps
    cast = lambda w: w.astype(MM_DTYPE)
    row3 = lambda v: v[:, None, :]

    cond = jnp.zeros((MOD_ROWS, d), F32).at[0].set(c_ctx).at[1:1 + bs].set(c)
    mod = _adaln(cond, w_ada, b_ada, _pick(w_ada.shape[-1], 1024)).reshape(depth, MOD_ROWS, N_MOD, d)

    ql, kl, r, nh = cfg.q_lora, cfg.kv_lora, cfg.rope, cfg.n_heads
    w_kr = w_mla_in[..., ql + kl:]
    w_in_p = cast(jnp.concatenate(
        [w_mla_in[..., :ql + kl], _pad_lanes(w_kr), _pad_lanes(_swap_halves(w_kr, r))], axis=-1))
    uq = w_mla_uq.reshape(w_mla_uq.shape[0], ql, nh, cfg.nope + r)
    flat = lambda w: cast(w.reshape(w.shape[0], ql, nh * LANES))
    w_qn, w_qr, w_qs = flat(uq[..., :cfg.nope]), flat(_pad_lanes(uq[..., cfg.nope:])), flat(
        _pad_lanes(_swap_halves(uq[..., cfg.nope:], r)))
    w_uk, w_uv, w_o = cast(w_mla_uk), cast(w_mla_uv), cast(w_mla_o)
    g_q, g_kv = row3(g_mla_q), row3(g_mla_kv)
    cos, sin = _rope_tables(ss, cfg)

    w_gates = cast(0.5 * jnp.concatenate([w_rec_gx[:, 0], w_rec_ga[:, 0], w_rec_gx[:, 1], w_rec_ga[:, 1]], axis=-1))
    nb = cfg.lru_blocks
    blk = lambda b: b.reshape(b.shape[0], nb, 1, LANES)
    b_gates = 0.5 * jnp.concatenate(
        [blk(b_rec_gx[:, 0]), blk(b_rec_ga[:, 0]), blk(b_rec_gx[:, 1]), blk(b_rec_ga[:, 1])], axis=-1)
    tf = _pick(w_ffn_down.shape[1], 512)
    w_rin, w_rout = cast(w_rec_in), cast(w_rec_out)
    w_up, w_down = cast(w_ffn_up), cast(w_ffn_down)
    g_mix3, g_ffn3, b_rconv, b_fconv = row3(g_mix), row3(g_ffn), row3(b_rec_conv), row3(b_ffn_conv)

    ctx_row = lambda i, tm: 0

    def run_group(x, seq, mod_row, sample):
        n = x.shape[0]
        rows = lambda pref: min(seq, pref) if sample else _pick(n, pref)
        tm_ffn = seq if sample else max(seq, _pick(n, 1024))
        ckv_new, kr_new, lru_new = [], [], []
        for layer in range(depth):
            j = layer // 2
            if layer % 2 == 0:
                proj = _norm_mod_matmul(x, mod, mod_row, g_mix3, w_in_p, layer, j, tm=rows(1024),
                                        tn=w_in_p.shape[-1], eps=eps)
                if sample:
                    qn, qr, kn, v, krp = _mla_mid(proj, g_q, g_kv, w_qn, w_qr, w_qs, w_uk, w_uv, j, cos, sin,
                                                  cfg=cfg, seq=seq, tm=rows(256))
                    cache = _cache_expand(cache_ckv, cache_krope, j, w_uk, w_uv, cfg=cfg)
                    att = _attention(qn, qr, kn, krp, v, cache, cfg=cfg, seq=seq, hb=min(4, nh))
                else:
                    qn, qr, kn, v, krp, ckv, kr = _mla_mid(proj, g_q, g_kv, w_qn, w_qr, None, w_uk, w_uv, j,
                                                           None, None, cfg=cfg, seq=seq, tm=rows(256))
                    ckv_new.append(ckv)
                    kr_new.append(kr)
                    att = _attention(qn, qr, kn, krp, v, None, cfg=cfg, seq=seq, hb=nh)
                x = _matmul_residual(att, w_o, j, x, mod, layer, mod_row, tm=rows(512))
            else:
                proj = _rec_in(x, mod, mod_row, g_mix3, w_rin, w_rec_conv, b_rconv, layer, j, seq=seq,
                               tm=max(seq, rows(1024)), tn=_pick(w_rin.shape[-1] // 2, 1024), eps=eps)
                outs = _rec_scan(proj, w_gates, b_gates, rec_lambda, j,
                                 (state_lru[:, j, 0], state_lru[:, j, 1]) if sample else None, cfg=cfg, seq=seq)
                if not sample:
                    lru_new.append(jnp.stack(outs[1:], axis=1))
                x = _matmul_residual(outs[0], w_rout, j, x, mod, layer, mod_row, tm=rows(512))
            x = _conv_ffn(x, mod, layer, mod_row, g_ffn3, w_up, w_ffn_conv, b_fconv, w_down,
                          g_final[None] if layer == depth - 1 else None, seq=seq, tm=tm_ffn, tf=tf,
                          n_sub=2 if tf % (2 * LANES) == 0 else 1, x_buffers=2, eps=eps)
        return x, ckv_new, kr_new, lru_new

    y_p, ckv_new, kr_new, lru_new = run_group(x_prompt.reshape(bp * sp, d), sp, ctx_row, False)
    y_s, _, _, _ = run_group(x_sample.reshape(bs * ss, d), ss, lambda i, tm: 1 + (i * tm) // ss, True)
    new_ckv = jnp.stack([t.reshape(bp, sp, kl) for t in ckv_new], axis=1)
    new_kr = jnp.stack([t.reshape(bp, sp, r) for t in kr_new], axis=1)
    new_lru = jnp.stack(lru_new, axis=1)
    return y_p.reshape(bp, sp, d), y_s.reshape(bs, ss, d), new_ckv, new_kr, new_lru


CFG = Cfg(n_heads=16, q_lora=512, kv_lora=512, nope=128, rope=64, v_dim=128, grid_w=64, rope_base=10000.0,
          lru_blocks=16, lru_c=8.0, eps=1e-6)


def kernel(x_prompt, x_sample, cache_ckv, cache_krope, state_lru, c, c_ctx, g_mix, g_ffn, g_final, w_ada, b_ada, w_mla_in, g_mla_q, g_mla_kv, w_mla_uq, w_mla_uk, w_mla_uv, w_mla_o, w_rec_in, w_rec_conv, b_rec_conv, w_rec_gx, b_rec_gx, w_rec_ga, b_rec_ga, rec_lambda, w_rec_out, w_ffn_up, w_ffn_conv, b_ffn_conv, w_ffn_down):
    return _forward(CFG, x_prompt, x_sample, cache_ckv, cache_krope, state_lru, c, c_ctx, g_mix, g_ffn, g_final,
                    w_ada, b_ada, w_mla_in, g_mla_q, g_mla_kv, w_mla_uq, w_mla_uk, w_mla_uv, w_mla_o,
                    w_rec_in, w_rec_conv, b_rec_conv, w_rec_gx, b_rec_gx, w_rec_ga, b_rec_ga, rec_lambda, w_rec_out,
                    w_ffn_up, w_ffn_conv, b_ffn_conv, w_ffn_down)
```

```python
import functools
import math
from typing import NamedTuple

import jax
import jax.numpy as jnp
from jax import lax
from jax.experimental import pallas as pl
from jax.experimental.pallas import tpu as pltpu

F32 = jnp.float32
MM_DTYPE = jnp.bfloat16

LANES = 128
SUBLANES = 8
VMEM_LIMIT_BYTES = 60 * 1024 * 1024
N_MOD = 6
MOD_ROWS = 16


class Cfg(NamedTuple):
    n_heads: int
    q_lora: int
    kv_lora: int
    nope: int
    rope: int
    v_dim: int
    grid_w: int
    rope_base: float
    lru_blocks: int
    lru_c: float
    eps: float


def _params(sem):
    return pltpu.CompilerParams(dimension_semantics=sem, vmem_limit_bytes=VMEM_LIMIT_BYTES)


def _dot(a, b):
    return jnp.dot(a, b, preferred_element_type=F32)


def _dot_nt(a, b):
    return lax.dot_general(a, b, (((1,), (1,)), ((), ())), preferred_element_type=F32)


def _const_spec(shape):
    nd = len(shape)
    return pl.BlockSpec(shape, lambda *_: (0,) * nd)


def _layer_spec(arr, layer):
    nd = arr.ndim
    return pl.BlockSpec((None,) + arr.shape[1:], lambda *_: (layer,) + (0,) * (nd - 1))


def _mod_spec(d, layer, mod_row, tm):
    return pl.BlockSpec((None, None, N_MOD, d), lambda i, *_: (layer, mod_row(i, tm), 0, 0))


def _rms(x, g, eps):
    return x * lax.rsqrt(jnp.mean(x * x, axis=-1, keepdims=True) + eps) * g


def _norm_mod(x, g, mod_ref, shift_idx, scale_idx, eps):
    y = _rms(x, g, eps)
    return y * (1.0 + mod_ref[scale_idx:scale_idx + 1, :]) + mod_ref[shift_idx:shift_idx + 1, :]


ROW_PAD = SUBLANES


def _zero_row_pads(u_ref, rows):
    lead = (slice(None),) * (len(u_ref.shape) - 2)
    zeros = jnp.zeros(u_ref.shape[:-2] + (ROW_PAD, u_ref.shape[-1]), u_ref.dtype)
    u_ref[lead + (slice(0, ROW_PAD), slice(None))] = zeros
    u_ref[lead + (slice(ROW_PAD + rows, ROW_PAD + rows + ROW_PAD), slice(None))] = zeros


def _store_slabs(u_ref, first, u):
    for k in range(u.shape[1] // LANES):
        u_ref[first + k, ROW_PAD:ROW_PAD + u.shape[0], :] = u[:, k * LANES:(k + 1) * LANES]


def _seq_window(u_ref, off, rows, seq):
    w = u_ref[ROW_PAD + off:ROW_PAD + off + rows, :]
    if off == 0 or seq == rows:
        return w
    sub_row = lax.broadcasted_iota(jnp.int32, (SUBLANES, 1), 0)
    pieces = []
    for base in range(0, rows, seq):
        if off < 0:
            tile, bad = base, sub_row < -off
        else:
            tile, bad = base + seq - SUBLANES, sub_row >= SUBLANES - off
        pieces += [w[base:tile], jnp.where(bad, 0.0, w[tile:tile + SUBLANES]), w[tile + SUBLANES:base + seq]]
    return jnp.concatenate([p for p in pieces if p.shape[0]], axis=0)


def _adaln_body(cond_ref, w_ref, b_ref, o_ref):
    c = cond_ref[...]
    s = (c * jax.nn.sigmoid(c)).astype(MM_DTYPE)
    o_ref[...] = _dot(s, w_ref[...].astype(MM_DTYPE)) + b_ref[...]


def _adaln(cond, w_ada, b_ada, tn):
    n_layers, d, n_out = w_ada.shape
    return pl.pallas_call(
        _adaln_body,
        out_shape=jax.ShapeDtypeStruct((n_layers, MOD_ROWS, n_out), F32),
        grid=(n_layers, n_out // tn),
        in_specs=[
            _const_spec((MOD_ROWS, d)),
            pl.BlockSpec((None, d, tn), lambda l, n: (l, 0, n)),
            pl.BlockSpec((None, 1, tn), lambda l, n: (l, 0, n)),
        ],
        out_specs=pl.BlockSpec((None, MOD_ROWS, tn), lambda l, n: (l, 0, n)),
        compiler_params=_params(("arbitrary", "arbitrary")),
        name="adaln",
    )(cond, w_ada, b_ada.reshape(n_layers, 1, n_out))


def _nmm_body(x_ref, mod_ref, g_ref, w_ref, o_ref, h_ref, *, eps):
    @pl.when(pl.program_id(1) == 0)
    def _():
        h = _norm_mod(x_ref[...], g_ref[...], mod_ref, 0, 1, eps).astype(MM_DTYPE)
        h_ref[...] = h
        o_ref[...] = _dot(h, w_ref[...])

    @pl.when(pl.program_id(1) > 0)
    def _():
        o_ref[...] = _dot(h_ref[...], w_ref[...])


def _norm_mod_matmul(x, mod, mod_row, g, w, layer, wl, *, tm, tn, eps):
    n, d = x.shape
    m = w.shape[2]
    return pl.pallas_call(
        functools.partial(_nmm_body, eps=eps),
        out_shape=jax.ShapeDtypeStruct((n, m), F32),
        grid=(n // tm, m // tn),
        in_specs=[
            pl.BlockSpec((tm, d), lambda i, j: (i, 0)),
            _mod_spec(d, layer, mod_row, tm),
            _layer_spec(g, layer),
            pl.BlockSpec((None, d, tn), lambda i, j: (wl, 0, j)),
        ],
        out_specs=pl.BlockSpec((tm, tn), lambda i, j: (i, j)),
        scratch_shapes=[pltpu.VMEM((tm, d), MM_DTYPE)],
        compiler_params=_params(("arbitrary", "arbitrary")),
        name="norm_mod_matmul",
    )(x, mod, g, w)


def _rec_in_body(x_ref, mod_ref, g_ref, w_ref, cw_ref, cb_ref, o_ref, h_ref, u_ref, *, seq, n_y, eps):
    j = pl.program_id(1)
    tm = h_ref.shape[0]

    def gelu_tile(h):
        y = jax.nn.gelu(_dot(h, w_ref[...]))
        for c in range(o_ref.shape[0]):
            o_ref[c] = y[:, c * LANES:(c + 1) * LANES]

    @pl.when(j == 0)
    def _():
        h = _norm_mod(x_ref[...], g_ref[...], mod_ref, 0, 1, eps).astype(MM_DTYPE)
        h_ref[...] = h
        _zero_row_pads(u_ref, tm)
        gelu_tile(h)

    @pl.when((j > 0) & (j < n_y))
    def _():
        gelu_tile(h_ref[...])

    @pl.when(j >= n_y)
    def _():
        _store_slabs(u_ref, 0, _dot(h_ref[...], w_ref[...]))
        for c in range(u_ref.shape[0]):
            sl = slice(c * LANES, (c + 1) * LANES)
            xc = cb_ref[:, sl]
            for tap, off in enumerate((-2, -1, 0, 1)):
                xc = xc + _seq_window(u_ref.at[c], off, tm, seq) * cw_ref[tap:tap + 1, sl]
            o_ref[c] = xc


def _rec_in(x, mod, mod_row, g, w, w_conv, b_conv, layer, wl, *, seq, tm, tn, eps):
    n, d = x.shape
    m = w.shape[2]
    n_y = m // 2 // tn
    assert tm % seq == 0 and w_conv.shape[1] == 4 and m % (2 * tn) == 0
    conv_col = lambda i, j: (wl, 0, jnp.maximum(j - n_y, 0))
    return pl.pallas_call(
        functools.partial(_rec_in_body, seq=seq, n_y=n_y, eps=eps),
        out_shape=jax.ShapeDtypeStruct((m // LANES, n, LANES), F32),
        grid=(n // tm, m // tn),
        in_specs=[
            pl.BlockSpec((tm, d), lambda i, j: (i, 0)),
            _mod_spec(d, layer, mod_row, tm),
            _layer_spec(g, layer),
            pl.BlockSpec((None, d, tn), lambda i, j: (wl, 0, j)),
            pl.BlockSpec((None, 4, tn), conv_col),
            pl.BlockSpec((None, 1, tn), conv_col),
        ],
        out_specs=pl.BlockSpec((tn // LANES, tm, LANES), lambda i, j: (j, i, 0)),
        scratch_shapes=[pltpu.VMEM((tm, d), MM_DTYPE), pltpu.VMEM((tn // LANES, ROW_PAD + tm + ROW_PAD, LANES), F32)],
        compiler_params=_params(("arbitrary", "arbitrary")),
        name="rec_in",
    )(x, mod, g, w, w_conv, b_conv)


def _mmres_body(a_ref, w_ref, x_ref, mod_ref, o_ref):
    o_ref[...] = x_ref[...] + mod_ref[2:3, :] * _dot(a_ref[...], w_ref[...])


def _matmul_residual(a, w, wl, x, mod, layer, mod_row, *, tm):
    n, k = a.shape
    d = w.shape[2]
    return pl.pallas_call(
        _mmres_body,
        out_shape=jax.ShapeDtypeStruct((n, d), F32),
        grid=(n // tm,),
        in_specs=[
            pl.BlockSpec((tm, k), lambda i: (i, 0)),
            _layer_spec(w, wl),
            pl.BlockSpec((tm, d), lambda i: (i, 0)),
            _mod_spec(d, layer, mod_row, tm),
        ],
        out_specs=pl.BlockSpec((tm, d), lambda i: (i, 0)),
        compiler_params=_params(("arbitrary",)),
        name="matmul_residual",
    )(a, w, x, mod)


def _ffn_body(*refs, seq, eps, n_sub, final):
    if final:
        (x_ref, mod_ref, g_ref, wg_ref, wv_ref, cwg_ref, cwv_ref, cbg_ref, cbv_ref, wd_ref, gf_ref,
         o_ref, h_ref, act_ref, ug_ref, uv_ref) = refs
    else:
        (x_ref, mod_ref, g_ref, wg_ref, wv_ref, cwg_ref, cwv_ref, cbg_ref, cbv_ref, wd_ref,
         o_ref, h_ref, act_ref, ug_ref, uv_ref) = refs
    j = pl.program_id(1)
    n_tiles = pl.num_programs(1) - 1
    tm = h_ref.shape[0]
    ts = wg_ref.shape[1] // n_sub
    subs = [slice(s * ts, (s + 1) * ts) for s in range(n_sub)]

    def conv(u_ref, c, cw, cb):
        win = lambda off: _seq_window(u_ref.at[c], off, tm, seq)
        return cb + win(-1) * cw[0:1, :] + win(0) * cw[1:2, :] + win(1) * cw[2:3, :]

    def up():
        h = h_ref[...]
        for s, sl in enumerate(subs):
            _store_slabs(ug_ref, s * (ts // LANES), _dot(h, wg_ref[:, sl]))
            _store_slabs(uv_ref, s * (ts // LANES), _dot(h, wv_ref[:, sl]))

    def activate():
        for c in range(ug_ref.shape[0]):
            sl = slice(c * LANES, (c + 1) * LANES)
            hg = conv(ug_ref, c, 0.5 * cwg_ref[:, sl], 0.5 * cbg_ref[:, sl])
            val = conv(uv_ref, c, cwv_ref[:, sl], cbv_ref[:, sl])
            act_ref[:, sl] = (hg * (jnp.tanh(hg) + 1.0) * val).astype(MM_DTYPE)

    def down(act):
        o_ref[...] += _dot(act, wd_ref[...])

    @pl.when(j == 0)
    def _():
        h = _norm_mod(x_ref[...], g_ref[...], mod_ref, 3, 4, eps)
        h_ref[...] = h.astype(MM_DTYPE)
        o_ref[...] = jnp.zeros_like(o_ref)
        for u_ref in (ug_ref, uv_ref):
            _zero_row_pads(u_ref, tm)
        up()
        activate()

    @pl.when((j > 0) & (j < n_tiles))
    def _():
        down(act_ref[...])
        up()
        activate()

    @pl.when(j == n_tiles)
    def _():
        down(act_ref[...])
        y = x_ref[...] + mod_ref[5:6, :] * o_ref[...]
        if final:
            y = _rms(y, gf_ref[...], eps)
        o_ref[...] = y


def _conv_ffn(x, mod, layer, mod_row, g, w_up, w_conv, b_conv, w_down, g_final, *, seq, tm, tf, n_sub, x_buffers,
              eps):
    n, d = x.shape
    f = w_down.shape[1]
    assert w_conv.shape[1:] == (3, 2 * f) and tm % seq == 0 and f % tf == 0 and tf % (n_sub * LANES) == 0
    nf = f // tf
    final = g_final is not None
    up_tile = lambda j: jnp.minimum(j, nf - 1)
    lo = lambda rows: pl.BlockSpec((None, rows, tf), lambda i, j: (layer, 0, up_tile(j)))
    hi = lambda rows: pl.BlockSpec((None, rows, tf), lambda i, j: (layer, 0, up_tile(j) + nf))
    ins = [x, mod, g, w_up, w_up, w_conv, w_conv, b_conv, b_conv, w_down] + ([g_final] if final else [])
    in_specs = [
        pl.BlockSpec((tm, d), lambda i, j: (i, 0), pipeline_mode=pl.Buffered(x_buffers)),
        _mod_spec(d, layer, mod_row, tm),
        _layer_spec(g, layer),
        lo(d), hi(d), lo(3), hi(3), lo(1), hi(1),
        pl.BlockSpec((None, tf, d), lambda i, j: (layer, jnp.maximum(j - 1, 0), 0)),
    ] + ([_const_spec(g_final.shape)] if final else [])
    return pl.pallas_call(
        functools.partial(_ffn_body, seq=seq, eps=eps, n_sub=n_sub, final=final),
        out_shape=jax.ShapeDtypeStruct((n, d), F32),
        grid=(n // tm, nf + 1),
        in_specs=in_specs,
        out_specs=pl.BlockSpec((tm, d), lambda i, j: (i, 0)),
        scratch_shapes=[pltpu.VMEM((tm, d), MM_DTYPE), pltpu.VMEM((tm, tf), MM_DTYPE)]
        + [pltpu.VMEM((tf // LANES, ROW_PAD + tm + ROW_PAD, LANES), F32)] * 2,
        compiler_params=_params(("arbitrary", "arbitrary")),
        name="conv_ffn_final" if final else "conv_ffn",
    )(*ins)


def _mla_mid_body(*refs, cfg, rope):
    if rope:
        (proj_ref, gq_ref, gkv_ref, wqn_ref, wqr_ref, wqs_ref, wuk_ref, wuv_ref, cos_ref, sin_ref,
         qn_ref, qr_ref, kn_ref, v_ref, krp_ref) = refs
    else:
        (proj_ref, gq_ref, gkv_ref, wqn_ref, wqr_ref, wuk_ref, wuv_ref,
         qn_ref, qr_ref, kn_ref, v_ref, krp_ref, ckv_ref, kr_ref) = refs
    ql, kl = cfg.q_lora, cfg.kv_lora
    p = proj_ref[...]
    cqn = _rms(p[:, :ql], gq_ref[...], cfg.eps).astype(MM_DTYPE)
    ckv = _rms(p[:, ql:ql + kl], gkv_ref[...], cfg.eps)
    krp = p[:, ql + kl:ql + kl + LANES]
    qn_ref[...] = _dot(cqn, wqn_ref[...]).astype(MM_DTYPE)
    qr = _dot(cqn, wqr_ref[...])
    if rope:
        krs = p[:, ql + kl + LANES:ql + kl + 2 * LANES]
        qs = _dot(cqn, wqs_ref[...])
        cos = cos_ref[...]
        sin = sin_ref[...]
        for h in range(cfg.n_heads):
            sl = slice(h * LANES, (h + 1) * LANES)
            qr_ref[:, sl] = (qr[:, sl] * cos + qs[:, sl] * sin).astype(MM_DTYPE)
        krp_ref[...] = (krp * cos + krs * sin).astype(MM_DTYPE)
    else:
        qr_ref[...] = qr.astype(MM_DTYPE)
        krp_ref[...] = krp.astype(MM_DTYPE)
        ckv_ref[...] = ckv
        kr_ref[...] = krp[:, :cfg.rope]
    ckv_b = ckv.astype(MM_DTYPE)
    kn_ref[...] = _dot(ckv_b, wuk_ref[...]).astype(MM_DTYPE)
    v_ref[...] = _dot(ckv_b, wuv_ref[...]).astype(MM_DTYPE)


def _mla_mid(proj, g_q, g_kv, w_qn, w_qr, w_qs, w_uk, w_uv, wl, cos, sin, *, cfg, seq, tm):
    n, pw = proj.shape
    rope = cos is not None
    hw = cfg.n_heads * LANES
    row = lambda i: (i, 0)
    ws = [g_q, g_kv, w_qn, w_qr] + ([w_qs] if rope else []) + [w_uk, w_uv]
    ins = [proj] + ws
    in_specs = [pl.BlockSpec((tm, pw), row)] + [_layer_spec(w, wl) for w in ws]
    out_shape = [jax.ShapeDtypeStruct((n, hw), MM_DTYPE)] * 4 + [jax.ShapeDtypeStruct((n, LANES), MM_DTYPE)]
    out_specs = [pl.BlockSpec((tm, hw), row)] * 4 + [pl.BlockSpec((tm, LANES), row)]
    if rope:
        bps = seq // tm
        ins += [cos, sin]
        in_specs += [pl.BlockSpec((tm, LANES), lambda i: (i % bps, 0))] * 2
    else:
        out_shape += [jax.ShapeDtypeStruct((n, cfg.kv_lora), F32), jax.ShapeDtypeStruct((n, cfg.rope), F32)]
        out_specs += [pl.BlockSpec((tm, cfg.kv_lora), row), pl.BlockSpec((tm, cfg.rope), row)]
    return pl.pallas_call(
        functools.partial(_mla_mid_body, cfg=cfg, rope=rope),
        out_shape=out_shape,
        grid=(n // tm,),
        in_specs=in_specs,
        out_specs=out_specs,
        compiler_params=_params(("arbitrary",)),
        name="mla_mid_rope" if rope else "mla_mid",
    )(*ins)


def _cache_expand_body(ckv_ref, kr_ref, wuk_ref, wuv_ref, kn_ref, v_ref, krp_ref, *, rope):
    c = ckv_ref[...].astype(MM_DTYPE)
    kn_ref[...] = _dot(c, wuk_ref[...]).astype(MM_DTYPE)
    v_ref[...] = _dot(c, wuv_ref[...]).astype(MM_DTYPE)
    krp_ref[:, :rope] = kr_ref[...].astype(MM_DTYPE)
    krp_ref[:, rope:] = jnp.zeros((krp_ref.shape[0], LANES - rope), MM_DTYPE)


def _cache_expand(cache_ckv, cache_krope, wl, w_uk, w_uv, *, cfg):
    b, _, p, kl = cache_ckv.shape
    hw = cfg.n_heads * LANES
    row = lambda i: (i, 0)
    return pl.pallas_call(
        functools.partial(_cache_expand_body, rope=cfg.rope),
        out_shape=[jax.ShapeDtypeStruct((b * p, hw), MM_DTYPE)] * 2 + [jax.ShapeDtypeStruct((b * p, LANES), MM_DTYPE)],
        grid=(b,),
        in_specs=[
            pl.BlockSpec((None, None, p, kl), lambda i: (i, wl, 0, 0)),
            pl.BlockSpec((None, None, p, cfg.rope), lambda i: (i, wl, 0, 0)),
            _layer_spec(w_uk, wl),
            _layer_spec(w_uv, wl),
        ],
        out_specs=[pl.BlockSpec((p, hw), row)] * 2 + [pl.BlockSpec((p, LANES), row)],
        compiler_params=_params(("arbitrary",)),
        name="cache_expand",
    )(cache_ckv, cache_krope, w_uk, w_uv)


def _attn_body(*refs, hb, scale, cached, q_chunk):
    if cached:
        qn_ref, qr_ref, kn_ref, kr_ref, v_ref, knc_ref, krc_ref, vc_ref, o_ref = refs
    else:
        qn_ref, qr_ref, kn_ref, kr_ref, v_ref, o_ref = refs
    sq = qn_ref.shape[0]
    qc = min(sq, q_chunk)
    for h in range(hb):
        sl = slice(h * LANES, (h + 1) * LANES)
        k = jnp.concatenate([kn_ref[:, sl], kr_ref[...]], axis=1)
        if cached:
            kc = jnp.concatenate([knc_ref[:, sl], krc_ref[...]], axis=1)
        for r0 in range(0, sq, qc):
            rows = slice(r0, r0 + qc)
            q = jnp.concatenate([qn_ref[rows, sl], qr_ref[rows, sl]], axis=1)
            s = _dot_nt(q, k) * scale
            m = jnp.max(s, axis=-1, keepdims=True)
            if cached:
                sc = _dot_nt(q, kc) * scale
                m = jnp.maximum(m, jnp.max(sc, axis=-1, keepdims=True))
                pc = jnp.exp(sc - m)
            p = jnp.exp(s - m)
            den = jnp.sum(p, axis=-1, keepdims=True)
            o = _dot(p.astype(MM_DTYPE), v_ref[:, sl])
            if cached:
                den = den + jnp.sum(pc, axis=-1, keepdims=True)
                o = o + _dot(pc.astype(MM_DTYPE), vc_ref[:, sl])
            o_ref[rows, sl] = (o / den).astype(MM_DTYPE)


def _attention(qn, qr, kn, krp, v, cache, *, cfg, seq, hb):
    n, hw = qn.shape
    cached = cache is not None
    blk = lambda rows: pl.BlockSpec((rows, hb * LANES), lambda b, g: (b, g))
    shared = lambda rows: pl.BlockSpec((rows, LANES), lambda b, g: (b, 0))
    ins = [qn, qr, kn, krp, v]
    in_specs = [blk(seq), blk(seq), blk(seq), shared(seq), blk(seq)]
    if cached:
        knc, vc, krc = cache
        past = knc.shape[0] // (n // seq)
        ins += [knc, krc, vc]
        in_specs += [blk(past), shared(past), blk(past)]
    return pl.pallas_call(
        functools.partial(_attn_body, hb=hb, scale=float(cfg.nope + cfg.rope) ** -0.5, cached=cached, q_chunk=512),
        out_shape=jax.ShapeDtypeStruct((n, hw), MM_DTYPE),
        grid=(n // seq, cfg.n_heads // hb),
        in_specs=in_specs,
        out_specs=blk(seq),
        compiler_params=_params(("arbitrary", "arbitrary")),
        name="attention_cached" if cached else "attention",
    )(*ins)


def _rec_scan_body(*refs, seq, cfg, has_state):
    if has_state:
        (yg_ref, xc_ref, wg_ref, bg_ref, lam_ref, h0f_ref, h0b_ref,
         o_ref, af_ref, uf_ref, ab_ref, ub_ref, hf_ref, hb_ref) = refs
    else:
        (yg_ref, xc_ref, wg_ref, bg_ref, lam_ref,
         o_ref, sf_ref, sb_ref, af_ref, uf_ref, ab_ref, ub_ref, hf_ref, hb_ref) = refs
    nseq = SUBLANES
    pitch = seq + 4
    lam = lam_ref[...]
    softplus = jnp.maximum(-lam, 0.0) + jnp.log1p(jnp.exp(-jnp.abs(lam)))
    half_decay = (-0.5 * cfg.lru_c * math.log2(math.e)) * softplus
    wg = wg_ref[...]
    bg = bg_ref[...]
    for b in range(nseq):
        xc = xc_ref[b * seq:(b + 1) * seq, :]
        th = jnp.tanh(_dot(xc.astype(MM_DTYPE), wg) + bg)
        xh = 0.5 * xc
        for d, (a_ref, u_ref) in enumerate(((af_ref, uf_ref), (ab_ref, ub_ref))):
            tx = th[:, (2 * d) * LANES:(2 * d + 1) * LANES]
            ta = th[:, (2 * d + 1) * LANES:(2 * d + 2) * LANES]
            hd = half_decay[d:d + 1, :]
            a = jnp.exp2(ta * hd + hd)
            z = 1.0 - a * a
            u = jnp.where(z > 0.0, z * lax.rsqrt(z), 0.0) * ((tx + 1.0) * xh)
            a_ref[b * pitch:b * pitch + seq, :] = a
            u_ref[b * pitch:b * pitch + seq, :] = u

    if has_state:
        hf0 = h0f_ref[...]
        hb0 = h0b_ref[...]
    else:
        hf0 = jnp.zeros((nseq, LANES), F32)
        hb0 = hf0

    def step(t, carry):
        hf, hb = carry
        fwd = pl.ds(t, nseq, stride=pitch)
        hf = af_ref[fwd, :] * hf + uf_ref[fwd, :]
        hf_ref[fwd, :] = hf
        bwd = pl.ds(seq - 1 - t, nseq, stride=pitch)
        hb = ab_ref[bwd, :] * hb + ub_ref[bwd, :]
        hb_ref[bwd, :] = hb
        return hf, hb

    hf, hb = lax.fori_loop(0, seq, step, (hf0, hb0), unroll=8)
    if not has_state:
        sf_ref[...] = hf
        sb_ref[...] = hb
    for b in range(nseq):
        hsum = hf_ref[b * pitch:b * pitch + seq, :] + hb_ref[b * pitch:b * pitch + seq, :]
        o_ref[b * seq:(b + 1) * seq, :] = (hsum * yg_ref[b * seq:(b + 1) * seq, :]).astype(MM_DTYPE)


def _rec_scan(proj, w_gates, b_gates, lam, wl, state, *, cfg, seq):
    nb2, n, _ = proj.shape
    nb = nb2 // 2
    w = nb * LANES
    assert nb == cfg.lru_blocks
    rows = SUBLANES * seq
    has_state = state is not None
    col = lambda g, c: (g, c)
    ins = [proj, proj, w_gates, b_gates, lam]
    in_specs = [
        pl.BlockSpec((None, rows, LANES), lambda g, c: (c, g, 0)),
        pl.BlockSpec((None, rows, LANES), lambda g, c: (c + nb, g, 0)),
        pl.BlockSpec((None, None, LANES, 4 * LANES), lambda g, c: (wl, c, 0, 0)),
        pl.BlockSpec((None, None, 1, 4 * LANES), lambda g, c: (wl, c, 0, 0)),
        pl.BlockSpec((None, 2, LANES), lambda g, c: (wl, 0, c)),
    ]
    out_shape = [jax.ShapeDtypeStruct((n, w), MM_DTYPE)]
    out_specs = [pl.BlockSpec((rows, LANES), col)]
    if has_state:
        ins += list(state)
        in_specs += [pl.BlockSpec((SUBLANES, LANES), col)] * 2
    else:
        nstate = n // seq
        out_shape += [jax.ShapeDtypeStruct((nstate, w), F32)] * 2
        out_specs += [pl.BlockSpec((SUBLANES, LANES), col)] * 2
    scratch = [pltpu.VMEM((SUBLANES * (seq + 4), LANES), F32)] * 6
    return pl.pallas_call(
        functools.partial(_rec_scan_body, seq=seq, cfg=cfg, has_state=has_state),
        out_shape=out_shape,
        grid=(n // rows, nb),
        in_specs=in_specs,
        out_specs=out_specs,
        scratch_shapes=scratch,
        compiler_params=_params(("arbitrary", "arbitrary")),
        name="rec_scan_state" if has_state else "rec_scan",
    )(*ins)


def _swap_halves(w, rope):
    q = rope // 4
    return jnp.concatenate([w[..., q:2 * q], w[..., :q], w[..., 3 * q:], w[..., 2 * q:3 * q]], axis=-1)


def _pad_lanes(w):
    return jnp.pad(w, [(0, 0)] * (w.ndim - 1) + [(0, LANES - w.shape[-1])])


def _rope_tables(seq, cfg):
    rows = seq // cfg.grid_w
    row = jnp.repeat(jnp.arange(rows), cfg.grid_w).astype(F32)
    col = jnp.tile(jnp.arange(cfg.grid_w), rows).astype(F32)
    half = cfg.rope // 2
    inv = 1.0 / (cfg.rope_base ** (jnp.arange(0, half, 2, dtype=F32) / half))
    ar, ac = row[:, None] * inv, col[:, None] * inv
    cos = jnp.concatenate([jnp.cos(ar), jnp.cos(ar), jnp.cos(ac), jnp.cos(ac)], axis=-1)
    sin = jnp.concatenate([-jnp.sin(ar), jnp.sin(ar), -jnp.sin(ac), jnp.sin(ac)], axis=-1)
    return _pad_lanes(cos), _pad_lanes(sin)


def _pick(n, pref):
    return pref if n % pref == 0 else n


def _forward(cfg, x_prompt, x_sample, cache_ckv, cache_krope, state_lru, c, c_ctx,
             g_mix, g_ffn, g_final, w_ada, b_ada,
             w_mla_in, g_mla_q, g_mla_kv, w_mla_uq, w_mla_uk, w_mla_uv, w_mla_o,
             w_rec_in, w_rec_conv, b_rec_conv, w_rec_gx, b_rec_gx, w_rec_ga, b_rec_ga,
             rec_lambda, w_rec_out,
             w_ffn_up, w_ffn_conv, b_ffn_conv, w_ffn_down):
    depth, d = g_mix.shape
    bp, sp, _ = x_prompt.shape
    bs, ss, _ = x_sample.shape
    assert cfg.nope == LANES and cfg.v_dim == LANES and cfg.rope <= LANES and bs + 1 <= MOD_ROWS
    assert bp % SUBLANES == 0 and bs % SUBLANES == 0
    eps = cfg.eps
    cast = lambda w: w.astype(MM_DTYPE)
    row3 = lambda v: v[:, None, :]

    cond = jnp.zeros((MOD_ROWS, d), F32).at[0].set(c_ctx).at[1:1 + bs].set(c)
    mod = _adaln(cond, w_ada, b_ada, _pick(w_ada.shape[-1], 1024)).reshape(depth, MOD_ROWS, N_MOD, d)

    ql, kl, r, nh = cfg.q_lora, cfg.kv_lora, cfg.rope, cfg.n_heads
    w_kr = w_mla_in[..., ql + kl:]
    w_in_p = cast(jnp.concatenate(
        [w_mla_in[..., :ql + kl], _pad_lanes(w_kr), _pad_lanes(_swap_halves(w_kr, r))], axis=-1))
    uq = w_mla_uq.reshape(w_mla_uq.shape[0], ql, nh, cfg.nope + r)
    flat = lambda w: cast(w.reshape(w.shape[0], ql, nh * LANES))
    w_qn, w_qr, w_qs = flat(uq[..., :cfg.nope]), flat(_pad_lanes(uq[..., cfg.nope:])), flat(
        _pad_lanes(_swap_halves(uq[..., cfg.nope:], r)))
    w_uk, w_uv, w_o = cast(w_mla_uk), cast(w_mla_uv), cast(w_mla_o)
    g_q, g_kv = row3(g_mla_q), row3(g_mla_kv)
    cos, sin = _rope_tables(ss, cfg)

    w_gates = cast(0.5 * jnp.concatenate([w_rec_gx[:, 0], w_rec_ga[:, 0], w_rec_gx[:, 1], w_rec_ga[:, 1]], axis=-1))
    nb = cfg.lru_blocks
    blk = lambda b: b.reshape(b.shape[0], nb, 1, LANES)
    b_gates = 0.5 * jnp.concatenate(
        [blk(b_rec_gx[:, 0]), blk(b_rec_ga[:, 0]), blk(b_rec_gx[:, 1]), blk(b_rec_ga[:, 1])], axis=-1)
    tf = _pick(w_ffn_down.shape[1], 512)
    w_rin, w_rout = cast(w_rec_in), cast(w_rec_out)
    w_up, w_down = cast(w_ffn_up), cast(w_ffn_down)
    g_mix3, g_ffn3, b_rconv, b_fconv = row3(g_mix), row3(g_ffn), row3(b_rec_conv), row3(b_ffn_conv)

    ctx_row = lambda i, tm: 0

    def run_group(x, seq, mod_row, sample):
        n = x.shape[0]
        rows = lambda pref: min(seq, pref) if sample else _pick(n, pref)
        tm_ffn = seq if sample else max(seq, _pick(n, 1024))
        ckv_new, kr_new, lru_new = [], [], []
        for layer in range(depth):
            j = layer // 2
            if layer % 2 == 0:
                proj = _norm_mod_matmul(x, mod, mod_row, g_mix3, w_in_p, layer, j, tm=rows(1024),
                                        tn=w_in_p.shape[-1], eps=eps)
                if sample:
                    qn, qr, kn, v, krp = _mla_mid(proj, g_q, g_kv, w_qn, w_qr, w_qs, w_uk, w_uv, j, cos, sin,
                                                  cfg=cfg, seq=seq, tm=rows(256))
                    cache = _cache_expand(cache_ckv, cache_krope, j, w_uk, w_uv, cfg=cfg)
                    att = _attention(qn, qr, kn, krp, v, cache, cfg=cfg, seq=seq, hb=min(4, nh))
                else:
                    qn, qr, kn, v, krp, ckv, kr = _mla_mid(proj, g_q, g_kv, w_qn, w_qr, None, w_uk, w_uv, j,
                                                           None, None, cfg=cfg, seq=seq, tm=rows(256))
                    ckv_new.append(ckv)
                    kr_new.append(kr)
                    att = _attention(qn, qr, kn, krp, v, None, cfg=cfg, seq=seq, hb=nh)
                x = _matmul_residual(att, w_o, j, x, mod, layer, mod_row, tm=rows(512))
            else:
                proj = _rec_in(x, mod, mod_row, g_mix3, w_rin, w_rec_conv, b_rconv, layer, j, seq=seq,
                               tm=max(seq, rows(1024)), tn=_pick(w_rin.shape[-1] // 2, 1024), eps=eps)
                outs = _rec_scan(proj, w_gates, b_gates, rec_lambda, j,
                                 (state_lru[:, j, 0], state_lru[:, j, 1]) if sample else None, cfg=cfg, seq=seq)
                if not sample:
                    lru_new.append(jnp.stack(outs[1:], axis=1))
                x = _matmul_residual(outs[0], w_rout, j, x, mod, layer, mod_row, tm=rows(512))
            x = _conv_ffn(x, mod, layer, mod_row, g_ffn3, w_up, w_ffn_conv, b_fconv, w_down,
                          g_final[None] if layer == depth - 1 else None, seq=seq, tm=tm_ffn, tf=tf,
                          n_sub=2 if tf % (2 * LANES) == 0 else 1, x_buffers=2, eps=eps)
        return x, ckv_new, kr_new, lru_new

    y_p, ckv_new, kr_new, lru_new = run_group(x_prompt.reshape(bp * sp, d), sp, ctx_row, False)
    y_s, _, _, _ = run_group(x_sample.reshape(bs * ss, d), ss, lambda i, tm: 1 + (i * tm) // ss, True)
    new_ckv = jnp.stack([t.reshape(bp, sp, kl) for t in ckv_new], axis=1)
    new_kr = jnp.stack([t.reshape(bp, sp, r) for t in kr_new], axis=1)
    new_lru = jnp.stack(lru_new, axis=1)
    return y_p.reshape(bp, sp, d), y_s.reshape(bs, ss, d), new_ckv, new_kr, new_lru


CFG = Cfg(n_heads=16, q_lora=512, kv_lora=512, nope=128, rope=64, v_dim=128, grid_w=64, rope_base=10000.0,
          lru_blocks=16, lru_c=8.0, eps=1e-6)


def kernel(x_prompt, x_sample, cache_ckv, cache_krope, state_lru, c, c_ctx, g_mix, g_ffn, g_final, w_ada, b_ada, w_mla_in, g_mla_q, g_mla_kv, w_mla_uq, w_mla_uk, w_mla_uv, w_mla_o, w_rec_in, w_rec_conv, b_rec_conv, w_rec_gx, b_rec_gx, w_rec_ga, b_rec_ga, rec_lambda, w_rec_out, w_ffn_up, w_ffn_conv, b_ffn_conv, w_ffn_down):
    return _forward(CFG, x_prompt, x_sample, cache_ckv, cache_krope, state_lru, c, c_ctx, g_mix, g_ffn, g_final,
                    w_ada, b_ada, w_mla_in, g_mla_q, g_mla_kv, w_mla_uq, w_mla_uk, w_mla_uv, w_mla_o,
                    w_rec_in, w_rec_conv, b_rec_conv, w_rec_gx, b_rec_gx, w_rec_ga, b_rec_ga, rec_lambda, w_rec_out,
                    w_ffn_up, w_ffn_conv, b_ffn_conv, w_ffn_down)
```

```python
import functools
import math
from typing import NamedTuple

import jax
import jax.numpy as jnp
from jax import lax
from jax.experimental import pallas as pl
from jax.experimental.pallas import tpu as pltpu

F32 = jnp.float32
MM_DTYPE = jnp.bfloat16

LANES = 128
SUBLANES = 8
VMEM_LIMIT_BYTES = 60 * 1024 * 1024
N_MOD = 6
MOD_ROWS = 16


class Cfg(NamedTuple):
    n_heads: int
    q_lora: int
    kv_lora: int
    nope: int
    rope: int
    v_dim: int
    grid_w: int
    rope_base: float
    lru_blocks: int
    lru_c: float
    eps: float


def _params(sem):
    return pltpu.CompilerParams(dimension_semantics=sem, vmem_limit_bytes=VMEM_LIMIT_BYTES)


def _dot(a, b):
    return jnp.dot(a, b, preferred_element_type=F32)


def _dot_nt(a, b):
    return lax.dot_general(a, b, (((1,), (1,)), ((), ())), preferred_element_type=F32)


def _const_spec(shape):
    nd = len(shape)
    return pl.BlockSpec(shape, lambda *_: (0,) * nd)


def _layer_spec(arr, layer, buffers=2):
    nd = arr.ndim
    return pl.BlockSpec((None,) + arr.shape[1:], lambda *_: (layer,) + (0,) * (nd - 1),
                        pipeline_mode=pl.Buffered(buffers))


def _mod_spec(d, layer, mod_row, tm):
    return pl.BlockSpec((None, None, N_MOD, d), lambda i, *_: (layer, mod_row(i, tm), 0, 0))


def _rms(x, g, eps):
    return x * lax.rsqrt(jnp.mean(x * x, axis=-1, keepdims=True) + eps) * g


def _norm_mod(x, g, mod_ref, shift_idx, scale_idx, eps):
    y = _rms(x, g, eps)
    return y * (1.0 + mod_ref[scale_idx:scale_idx + 1, :]) + mod_ref[shift_idx:shift_idx + 1, :]


ROW_PAD = SUBLANES


def _zero_row_pads(u_ref, rows):
    lead = (slice(None),) * (len(u_ref.shape) - 2)
    zeros = jnp.zeros(u_ref.shape[:-2] + (ROW_PAD, u_ref.shape[-1]), u_ref.dtype)
    u_ref[lead + (slice(0, ROW_PAD), slice(None))] = zeros
    u_ref[lead + (slice(ROW_PAD + rows, ROW_PAD + rows + ROW_PAD), slice(None))] = zeros


def _store_slabs(u_ref, first, u):
    for k in range(u.shape[1] // LANES):
        u_ref[first + k, ROW_PAD:ROW_PAD + u.shape[0], :] = u[:, k * LANES:(k + 1) * LANES]


def _seq_window(u_ref, off, rows, seq):
    w = u_ref[ROW_PAD + off:ROW_PAD + off + rows, :]
    if off == 0 or seq == rows:
        return w
    sub_row = lax.broadcasted_iota(jnp.int32, (SUBLANES, 1), 0)
    pieces = []
    for base in range(0, rows, seq):
        if off < 0:
            tile, bad = base, sub_row < -off
        else:
            tile, bad = base + seq - SUBLANES, sub_row >= SUBLANES - off
        pieces += [w[base:tile], jnp.where(bad, 0.0, w[tile:tile + SUBLANES]), w[tile + SUBLANES:base + seq]]
    return jnp.concatenate([p for p in pieces if p.shape[0]], axis=0)


def _adaln_body(cond_ref, w_ref, b_ref, o_ref):
    c = cond_ref[...]
    s = (c * jax.nn.sigmoid(c)).astype(MM_DTYPE)
    o_ref[...] = _dot(s, w_ref[...].astype(MM_DTYPE)) + b_ref[...]


def _adaln(cond, w_ada, b_ada, tn):
    n_layers, d, n_out = w_ada.shape
    return pl.pallas_call(
        _adaln_body,
        out_shape=jax.ShapeDtypeStruct((n_layers, MOD_ROWS, n_out), F32),
        grid=(n_layers, n_out // tn),
        in_specs=[
            _const_spec((MOD_ROWS, d)),
            pl.BlockSpec((None, d, tn), lambda l, n: (l, 0, n)),
            pl.BlockSpec((None, 1, tn), lambda l, n: (l, 0, n)),
        ],
        out_specs=pl.BlockSpec((None, MOD_ROWS, tn), lambda l, n: (l, 0, n)),
        compiler_params=_params(("arbitrary", "arbitrary")),
        name="adaln",
    )(cond, w_ada, b_ada.reshape(n_layers, 1, n_out))


def _nmm_body(x_ref, mod_ref, g_ref, w_ref, o_ref, h_ref, *, eps):
    @pl.when(pl.program_id(1) == 0)
    def _():
        h = _norm_mod(x_ref[...], g_ref[...], mod_ref, 0, 1, eps).astype(MM_DTYPE)
        h_ref[...] = h
        o_ref[...] = _dot(h, w_ref[...])

    @pl.when(pl.program_id(1) > 0)
    def _():
        o_ref[...] = _dot(h_ref[...], w_ref[...])


def _norm_mod_matmul(x, mod, mod_row, g, w, layer, wl, *, tm, tn, eps):
    n, d = x.shape
    m = w.shape[2]
    return pl.pallas_call(
        functools.partial(_nmm_body, eps=eps),
        out_shape=jax.ShapeDtypeStruct((n, m), F32),
        grid=(n // tm, m // tn),
        in_specs=[
            pl.BlockSpec((tm, d), lambda i, j: (i, 0)),
            _mod_spec(d, layer, mod_row, tm),
            _layer_spec(g, layer),
            pl.BlockSpec((None, d, tn), lambda i, j: (wl, 0, j)),
        ],
        out_specs=pl.BlockSpec((tm, tn), lambda i, j: (i, j)),
        scratch_shapes=[pltpu.VMEM((tm, d), MM_DTYPE)],
        compiler_params=_params(("arbitrary", "arbitrary")),
        name="norm_mod_matmul",
    )(x, mod, g, w)


def _rec_in_body(x_ref, mod_ref, g_ref, w_ref, cw_ref, cb_ref, o_ref, h_ref, u_ref, *, seq, n_y, eps):
    j = pl.program_id(1)
    tm = h_ref.shape[0]

    def gelu_tile(h):
        y = jax.nn.gelu(_dot(h, w_ref[...]))
        for c in range(o_ref.shape[0]):
            o_ref[c] = y[:, c * LANES:(c + 1) * LANES]

    @pl.when(j == 0)
    def _():
        h = _norm_mod(x_ref[...], g_ref[...], mod_ref, 0, 1, eps).astype(MM_DTYPE)
        h_ref[...] = h
        _zero_row_pads(u_ref, tm)
        gelu_tile(h)

    @pl.when((j > 0) & (j < n_y))
    def _():
        gelu_tile(h_ref[...])

    @pl.when(j >= n_y)
    def _():
        _store_slabs(u_ref, 0, _dot(h_ref[...], w_ref[...]))
        for c in range(u_ref.shape[0]):
            sl = slice(c * LANES, (c + 1) * LANES)
            xc = cb_ref[:, sl]
            for tap, off in enumerate((-2, -1, 0, 1)):
                xc = xc + _seq_window(u_ref.at[c], off, tm, seq) * cw_ref[tap:tap + 1, sl]
            o_ref[c] = xc


def _rec_in(x, mod, mod_row, g, w, w_conv, b_conv, layer, wl, *, seq, tm, tn, eps):
    n, d = x.shape
    m = w.shape[2]
    n_y = m // 2 // tn
    assert tm % seq == 0 and w_conv.shape[1] == 4 and m % (2 * tn) == 0
    conv_col = lambda i, j: (wl, 0, jnp.maximum(j - n_y, 0))
    return pl.pallas_call(
        functools.partial(_rec_in_body, seq=seq, n_y=n_y, eps=eps),
        out_shape=jax.ShapeDtypeStruct((m // LANES, n, LANES), F32),
        grid=(n // tm, m // tn),
        in_specs=[
            pl.BlockSpec((tm, d), lambda i, j: (i, 0)),
            _mod_spec(d, layer, mod_row, tm),
            _layer_spec(g, layer),
            pl.BlockSpec((None, d, tn), lambda i, j: (wl, 0, j)),
            pl.BlockSpec((None, 4, tn), conv_col),
            pl.BlockSpec((None, 1, tn), conv_col),
        ],
        out_specs=pl.BlockSpec((tn // LANES, tm, LANES), lambda i, j: (j, i, 0)),
        scratch_shapes=[pltpu.VMEM((tm, d), MM_DTYPE), pltpu.VMEM((tn // LANES, ROW_PAD + tm + ROW_PAD, LANES), F32)],
        compiler_params=_params(("arbitrary", "arbitrary")),
        name="rec_in",
    )(x, mod, g, w, w_conv, b_conv)


def _mmres_body(a_ref, w_ref, x_ref, mod_ref, o_ref):
    o_ref[...] = x_ref[...] + mod_ref[2:3, :] * _dot(a_ref[...], w_ref[...])


def _matmul_residual(a, w, wl, x, mod, layer, mod_row, *, tm):
    n, k = a.shape
    d = w.shape[2]
    return pl.pallas_call(
        _mmres_body,
        out_shape=jax.ShapeDtypeStruct((n, d), F32),
        grid=(n // tm,),
        in_specs=[
            pl.BlockSpec((tm, k), lambda i: (i, 0)),
            _layer_spec(w, wl, buffers=1),
            pl.BlockSpec((tm, d), lambda i: (i, 0)),
            _mod_spec(d, layer, mod_row, tm),
        ],
        out_specs=pl.BlockSpec((tm, d), lambda i: (i, 0)),
        compiler_params=_params(("arbitrary",)),
        name="matmul_residual",
    )(a, w, x, mod)


def _ffn_body(*refs, seq, eps, n_sub, final):
    if final:
        (x_ref, mod_ref, g_ref, wg_ref, wv_ref, cwg_ref, cwv_ref, cbg_ref, cbv_ref, wd_ref, gf_ref,
         o_ref, h_ref, act_ref, ug_ref, uv_ref) = refs
    else:
        (x_ref, mod_ref, g_ref, wg_ref, wv_ref, cwg_ref, cwv_ref, cbg_ref, cbv_ref, wd_ref,
         o_ref, h_ref, act_ref, ug_ref, uv_ref) = refs
    j = pl.program_id(1)
    n_tiles = pl.num_programs(1) - 1
    tm = h_ref.shape[0]
    ts = wg_ref.shape[1] // n_sub
    subs = [slice(s * ts, (s + 1) * ts) for s in range(n_sub)]

    def conv(u_ref, c, cw, cb):
        win = lambda off: _seq_window(u_ref.at[c], off, tm, seq)
        return cb + win(-1) * cw[0:1, :] + win(0) * cw[1:2, :] + win(1) * cw[2:3, :]

    def up():
        h = h_ref[...]
        for s, sl in enumerate(subs):
            _store_slabs(ug_ref, s * (ts // LANES), _dot(h, wg_ref[:, sl]))
            _store_slabs(uv_ref, s * (ts // LANES), _dot(h, wv_ref[:, sl]))

    def activate():
        for c in range(ug_ref.shape[0]):
            sl = slice(c * LANES, (c + 1) * LANES)
            hg = conv(ug_ref, c, 0.5 * cwg_ref[:, sl], 0.5 * cbg_ref[:, sl])
            val = conv(uv_ref, c, cwv_ref[:, sl], cbv_ref[:, sl])
            act_ref[:, sl] = (hg * (jnp.tanh(hg) + 1.0) * val).astype(MM_DTYPE)

    def down(act):
        o_ref[...] += _dot(act, wd_ref[...])

    @pl.when(j == 0)
    def _():
        h = _norm_mod(x_ref[...], g_ref[...], mod_ref, 3, 4, eps)
        h_ref[...] = h.astype(MM_DTYPE)
        o_ref[...] = jnp.zeros_like(o_ref)
        for u_ref in (ug_ref, uv_ref):
            _zero_row_pads(u_ref, tm)
        up()
        activate()

    @pl.when((j > 0) & (j < n_tiles))
    def _():
        down(act_ref[...])
        up()
        activate()

    @pl.when(j == n_tiles)
    def _():
        down(act_ref[...])
        y = x_ref[...] + mod_ref[5:6, :] * o_ref[...]
        if final:
            y = _rms(y, gf_ref[...], eps)
        o_ref[...] = y


def _conv_ffn(x, mod, layer, mod_row, g, w_up, w_conv, b_conv, w_down, g_final, *, seq, tm, tf, n_sub, x_buffers,
              eps):
    n, d = x.shape
    f = w_down.shape[1]
    assert w_conv.shape[1:] == (3, 2 * f) and tm % seq == 0 and f % tf == 0 and tf % (n_sub * LANES) == 0
    nf = f // tf
    final = g_final is not None
    up_tile = lambda j: jnp.minimum(j, nf - 1)
    lo = lambda rows: pl.BlockSpec((None, rows, tf), lambda i, j: (layer, 0, up_tile(j)))
    hi = lambda rows: pl.BlockSpec((None, rows, tf), lambda i, j: (layer, 0, up_tile(j) + nf))
    ins = [x, mod, g, w_up, w_up, w_conv, w_conv, b_conv, b_conv, w_down] + ([g_final] if final else [])
    in_specs = [
        pl.BlockSpec((tm, d), lambda i, j: (i, 0), pipeline_mode=pl.Buffered(x_buffers)),
        _mod_spec(d, layer, mod_row, tm),
        _layer_spec(g, layer),
        lo(d), hi(d), lo(3), hi(3), lo(1), hi(1),
        pl.BlockSpec((None, tf, d), lambda i, j: (layer, jnp.maximum(j - 1, 0), 0)),
    ] + ([_const_spec(g_final.shape)] if final else [])
    return pl.pallas_call(
        functools.partial(_ffn_body, seq=seq, eps=eps, n_sub=n_sub, final=final),
        out_shape=jax.ShapeDtypeStruct((n, d), F32),
        grid=(n // tm, nf + 1),
        in_specs=in_specs,
        out_specs=pl.BlockSpec((tm, d), lambda i, j: (i, 0)),
        scratch_shapes=[pltpu.VMEM((tm, d), MM_DTYPE), pltpu.VMEM((tm, tf), MM_DTYPE)]
        + [pltpu.VMEM((tf // LANES, ROW_PAD + tm + ROW_PAD, LANES), F32)] * 2,
        compiler_params=_params(("arbitrary", "arbitrary")),
        name="conv_ffn_final" if final else "conv_ffn",
    )(*ins)


def _attn_log2_scale(cfg):
    return float(cfg.nope + cfg.rope) ** -0.5 * math.log2(math.e)


def _mla_mid_body(*refs, cfg, rope):
    if rope:
        (proj_ref, gq_ref, gkv_ref, wqn_ref, wqr_ref, wqs_ref, wuk_ref, wuv_ref, cos_ref, sin_ref,
         qn_ref, qr_ref, kn_ref, v_ref, krp_ref) = refs
    else:
        (proj_ref, gq_ref, gkv_ref, wqn_ref, wqr_ref, wuk_ref, wuv_ref,
         qn_ref, qr_ref, kn_ref, v_ref, krp_ref, ckv_ref, kr_ref) = refs
    ql, kl = cfg.q_lora, cfg.kv_lora
    p = proj_ref[...]
    cqn = _rms(p[:, :ql], gq_ref[...], cfg.eps).astype(MM_DTYPE)
    ckv = _rms(p[:, ql:ql + kl], gkv_ref[...], cfg.eps)
    krp = p[:, ql + kl:ql + kl + LANES]
    qk_scale = _attn_log2_scale(cfg)
    qn_ref[...] = (_dot(cqn, wqn_ref[...]) * qk_scale).astype(MM_DTYPE)
    qr = _dot(cqn, wqr_ref[...])
    if rope:
        krs = p[:, ql + kl + LANES:ql + kl + 2 * LANES]
        qs = _dot(cqn, wqs_ref[...])
        cos = cos_ref[...]
        sin = sin_ref[...]
        for h in range(cfg.n_heads):
            sl = slice(h * LANES, (h + 1) * LANES)
            qr_ref[:, sl] = ((qr[:, sl] * cos + qs[:, sl] * sin) * qk_scale).astype(MM_DTYPE)
        krp_ref[...] = (krp * cos + krs * sin).astype(MM_DTYPE)
    else:
        qr_ref[...] = (qr * qk_scale).astype(MM_DTYPE)
        krp_ref[...] = krp.astype(MM_DTYPE)
        ckv_ref[...] = ckv
        kr_ref[...] = krp[:, :cfg.rope]
    ckv_b = ckv.astype(MM_DTYPE)
    kn_ref[...] = _dot(ckv_b, wuk_ref[...]).astype(MM_DTYPE)
    v_ref[...] = _dot(ckv_b, wuv_ref[...]).astype(MM_DTYPE)


def _mla_mid(proj, g_q, g_kv, w_qn, w_qr, w_qs, w_uk, w_uv, wl, cos, sin, *, cfg, seq, tm):
    n, pw = proj.shape
    rope = cos is not None
    hw = cfg.n_heads * LANES
    row = lambda i: (i, 0)
    ws = [g_q, g_kv, w_qn, w_qr] + ([w_qs] if rope else []) + [w_uk, w_uv]
    ins = [proj] + ws
    in_specs = [pl.BlockSpec((tm, pw), row)] + [_layer_spec(w, wl) for w in ws]
    out_shape = [jax.ShapeDtypeStruct((n, hw), MM_DTYPE)] * 4 + [jax.ShapeDtypeStruct((n, LANES), MM_DTYPE)]
    out_specs = [pl.BlockSpec((tm, hw), row)] * 4 + [pl.BlockSpec((tm, LANES), row)]
    if rope:
        bps = seq // tm
        ins += [cos, sin]
        in_specs += [pl.BlockSpec((tm, LANES), lambda i: (i % bps, 0))] * 2
    else:
        out_shape += [jax.ShapeDtypeStruct((n, cfg.kv_lora), F32), jax.ShapeDtypeStruct((n, cfg.rope), F32)]
        out_specs += [pl.BlockSpec((tm, cfg.kv_lora), row), pl.BlockSpec((tm, cfg.rope), row)]
    return pl.pallas_call(
        functools.partial(_mla_mid_body, cfg=cfg, rope=rope),
        out_shape=out_shape,
        grid=(n // tm,),
        in_specs=in_specs,
        out_specs=out_specs,
        compiler_params=_params(("arbitrary",)),
        name="mla_mid_rope" if rope else "mla_mid",
    )(*ins)


def _cache_expand_body(ckv_ref, kr_ref, wuk_ref, wuv_ref, kn_ref, v_ref, krp_ref, *, rope):
    c = ckv_ref[...].astype(MM_DTYPE)
    kn_ref[...] = _dot(c, wuk_ref[...]).astype(MM_DTYPE)
    v_ref[...] = _dot(c, wuv_ref[...]).astype(MM_DTYPE)
    krp_ref[:, :rope] = kr_ref[...].astype(MM_DTYPE)
    krp_ref[:, rope:] = jnp.zeros((krp_ref.shape[0], LANES - rope), MM_DTYPE)


def _cache_expand(cache_ckv, cache_krope, wl, w_uk, w_uv, *, cfg):
    b, _, p, kl = cache_ckv.shape
    hw = cfg.n_heads * LANES
    row = lambda i: (i, 0)
    return pl.pallas_call(
        functools.partial(_cache_expand_body, rope=cfg.rope),
        out_shape=[jax.ShapeDtypeStruct((b * p, hw), MM_DTYPE)] * 2 + [jax.ShapeDtypeStruct((b * p, LANES), MM_DTYPE)],
        grid=(b,),
        in_specs=[
            pl.BlockSpec((None, None, p, kl), lambda i: (i, wl, 0, 0)),
            pl.BlockSpec((None, None, p, cfg.rope), lambda i: (i, wl, 0, 0)),
            _layer_spec(w_uk, wl),
            _layer_spec(w_uv, wl),
        ],
        out_specs=[pl.BlockSpec((p, hw), row)] * 2 + [pl.BlockSpec((p, LANES), row)],
        compiler_params=_params(("arbitrary",)),
        name="cache_expand",
    )(cache_ckv, cache_krope, w_uk, w_uv)


def _attn_body(*refs, hb, cached, q_chunk):
    if cached:
        qn_ref, qr_ref, kn_ref, kr_ref, v_ref, knc_ref, krc_ref, vc_ref, o_ref = refs
    else:
        qn_ref, qr_ref, kn_ref, kr_ref, v_ref, o_ref = refs
    sq = qn_ref.shape[0]
    qc = min(sq, q_chunk)
    for h in range(hb):
        sl = slice(h * LANES, (h + 1) * LANES)
        k = jnp.concatenate([kn_ref[:, sl], kr_ref[...]], axis=1)
        if cached:
            kc = jnp.concatenate([knc_ref[:, sl], krc_ref[...]], axis=1)
        for r0 in range(0, sq, qc):
            rows = slice(r0, r0 + qc)
            q = jnp.concatenate([qn_ref[rows, sl], qr_ref[rows, sl]], axis=1)
            s = _dot_nt(q, k)
            m = jnp.max(s, axis=-1, keepdims=True)
            if cached:
                sc = _dot_nt(q, kc)
                m = jnp.maximum(m, jnp.max(sc, axis=-1, keepdims=True))
                pc = jnp.exp2(sc - m)
            p = jnp.exp2(s - m)
            den = jnp.sum(p, axis=-1, keepdims=True)
            o = _dot(p.astype(MM_DTYPE), v_ref[:, sl])
            if cached:
                den = den + jnp.sum(pc, axis=-1, keepdims=True)
                o = o + _dot(pc.astype(MM_DTYPE), vc_ref[:, sl])
            o_ref[rows, sl] = (o / den).astype(MM_DTYPE)


def _attention(qn, qr, kn, krp, v, cache, *, cfg, seq, hb):
    n, hw = qn.shape
    cached = cache is not None
    blk = lambda rows: pl.BlockSpec((rows, hb * LANES), lambda b, g: (b, g))
    shared = lambda rows: pl.BlockSpec((rows, LANES), lambda b, g: (b, 0))
    ins = [qn, qr, kn, krp, v]
    in_specs = [blk(seq), blk(seq), blk(seq), shared(seq), blk(seq)]
    if cached:
        knc, vc, krc = cache
        past = knc.shape[0] // (n // seq)
        ins += [knc, krc, vc]
        in_specs += [blk(past), shared(past), blk(past)]
    return pl.pallas_call(
        functools.partial(_attn_body, hb=hb, cached=cached, q_chunk=512),
        out_shape=jax.ShapeDtypeStruct((n, hw), MM_DTYPE),
        grid=(n // seq, cfg.n_heads // hb),
        in_specs=in_specs,
        out_specs=blk(seq),
        compiler_params=_params(("arbitrary", "arbitrary")),
        name="attention_cached" if cached else "attention",
    )(*ins)


def _rec_scan_body(*refs, seq, cfg, has_state):
    if has_state:
        (yg_ref, xc_ref, wg_ref, bg_ref, lam_ref, h0f_ref, h0b_ref,
         o_ref, af_ref, uf_ref, ab_ref, ub_ref, hf_ref, hb_ref) = refs
    else:
        (yg_ref, xc_ref, wg_ref, bg_ref, lam_ref,
         o_ref, sf_ref, sb_ref, af_ref, uf_ref, ab_ref, ub_ref, hf_ref, hb_ref) = refs
    nseq = SUBLANES
    pitch = seq + 4
    lam = lam_ref[...]
    softplus = jnp.maximum(-lam, 0.0) + jnp.log1p(jnp.exp(-jnp.abs(lam)))
    half_decay = (-0.5 * cfg.lru_c * math.log2(math.e)) * softplus
    wg = wg_ref[...]
    bg = bg_ref[...]
    for b in range(nseq):
        xc = xc_ref[b * seq:(b + 1) * seq, :]
        th = jnp.tanh(_dot(xc.astype(MM_DTYPE), wg) + bg)
        xh = 0.5 * xc
        for d, (a_ref, u_ref) in enumerate(((af_ref, uf_ref), (ab_ref, ub_ref))):
            tx = th[:, (2 * d) * LANES:(2 * d + 1) * LANES]
            ta = th[:, (2 * d + 1) * LANES:(2 * d + 2) * LANES]
            hd = half_decay[d:d + 1, :]
            a = jnp.exp2(ta * hd + hd)
            z = 1.0 - a * a
            u = jnp.where(z > 0.0, z * lax.rsqrt(z), 0.0) * ((tx + 1.0) * xh)
            a_ref[b * pitch:b * pitch + seq, :] = a
            u_ref[b * pitch:b * pitch + seq, :] = u

    if has_state:
        hf0 = h0f_ref[...]
        hb0 = h0b_ref[...]
    else:
        hf0 = jnp.zeros((nseq, LANES), F32)
        hb0 = hf0

    def step(t, carry):
        hf, hb = carry
        fwd = pl.ds(t, nseq, stride=pitch)
        hf = af_ref[fwd, :] * hf + uf_ref[fwd, :]
        hf_ref[fwd, :] = hf
        bwd = pl.ds(seq - 1 - t, nseq, stride=pitch)
        hb = ab_ref[bwd, :] * hb + ub_ref[bwd, :]
        hb_ref[bwd, :] = hb
        return hf, hb

    hf, hb = lax.fori_loop(0, seq, step, (hf0, hb0), unroll=8)
    if not has_state:
        sf_ref[...] = hf
        sb_ref[...] = hb
    for b in range(nseq):
        hsum = hf_ref[b * pitch:b * pitch + seq, :] + hb_ref[b * pitch:b * pitch + seq, :]
        o_ref[b * seq:(b + 1) * seq, :] = (hsum * yg_ref[b * seq:(b + 1) * seq, :]).astype(MM_DTYPE)


def _rec_scan(proj, w_gates, b_gates, lam, wl, state, *, cfg, seq):
    nb2, n, _ = proj.shape
    nb = nb2 // 2
    w = nb * LANES
    assert nb == cfg.lru_blocks
    rows = SUBLANES * seq
    has_state = state is not None
    col = lambda g, c: (g, c)
    ins = [proj, proj, w_gates, b_gates, lam]
    in_specs = [
        pl.BlockSpec((None, rows, LANES), lambda g, c: (c, g, 0)),
        pl.BlockSpec((None, rows, LANES), lambda g, c: (c + nb, g, 0)),
        pl.BlockSpec((None, None, LANES, 4 * LANES), lambda g, c: (wl, c, 0, 0)),
        pl.BlockSpec((None, None, 1, 4 * LANES), lambda g, c: (wl, c, 0, 0)),
        pl.BlockSpec((None, 2, LANES), lambda g, c: (wl, 0, c)),
    ]
    out_shape = [jax.ShapeDtypeStruct((n, w), MM_DTYPE)]
    out_specs = [pl.BlockSpec((rows, LANES), col)]
    if has_state:
        ins += list(state)
        in_specs += [pl.BlockSpec((SUBLANES, LANES), col)] * 2
    else:
        nstate = n // seq
        out_shape += [jax.ShapeDtypeStruct((nstate, w), F32)] * 2
        out_specs += [pl.BlockSpec((SUBLANES, LANES), col)] * 2
    scratch = [pltpu.VMEM((SUBLANES * (seq + 4), LANES), F32)] * 6
    return pl.pallas_call(
        functools.partial(_rec_scan_body, seq=seq, cfg=cfg, has_state=has_state),
        out_shape=out_shape,
        grid=(n // rows, nb),
        in_specs=in_specs,
        out_specs=out_specs,
        scratch_shapes=scratch,
        compiler_params=_params(("arbitrary", "arbitrary")),
        name="rec_scan_state" if has_state else "rec_scan",
    )(*ins)


def _swap_halves(w, rope):
    q = rope // 4
    return jnp.concatenate([w[..., q:2 * q], w[..., :q], w[..., 3 * q:], w[..., 2 * q:3 * q]], axis=-1)


def _pad_lanes(w):
    return jnp.pad(w, [(0, 0)] * (w.ndim - 1) + [(0, LANES - w.shape[-1])])


def _rope_tables(seq, cfg):
    rows = seq // cfg.grid_w
    row = jnp.repeat(jnp.arange(rows), cfg.grid_w).astype(F32)
    col = jnp.tile(jnp.arange(cfg.grid_w), rows).astype(F32)
    half = cfg.rope // 2
    inv = 1.0 / (cfg.rope_base ** (jnp.arange(0, half, 2, dtype=F32) / half))
    ar, ac = row[:, None] * inv, col[:, None] * inv
    cos = jnp.concatenate([jnp.cos(ar), jnp.cos(ar), jnp.cos(ac), jnp.cos(ac)], axis=-1)
    sin = jnp.concatenate([-jnp.sin(ar), jnp.sin(ar), -jnp.sin(ac), jnp.sin(ac)], axis=-1)
    return _pad_lanes(cos), _pad_lanes(sin)


def _pick(n, pref):
    return pref if n % pref == 0 else n


def _forward(cfg, x_prompt, x_sample, cache_ckv, cache_krope, state_lru, c, c_ctx,
             g_mix, g_ffn, g_final, w_ada, b_ada,
             w_mla_in, g_mla_q, g_mla_kv, w_mla_uq, w_mla_uk, w_mla_uv, w_mla_o,
             w_rec_in, w_rec_conv, b_rec_conv, w_rec_gx, b_rec_gx, w_rec_ga, b_rec_ga,
             rec_lambda, w_rec_out,
             w_ffn_up, w_ffn_conv, b_ffn_conv, w_ffn_down):
    depth, d = g_mix.shape
    bp, sp, _ = x_prompt.shape
    bs, ss, _ = x_sample.shape
    assert cfg.nope == LANES and cfg.v_dim == LANES and cfg.rope <= LANES and bs + 1 <= MOD_ROWS
    assert bp % SUBLANES == 0 and bs % SUBLANES == 0
    eps = cfg.eps
    cast = lambda w: w.astype(MM_DTYPE)
    row3 = lambda v: v[:, None, :]

    cond = jnp.zeros((MOD_ROWS, d), F32).at[0].set(c_ctx).at[1:1 + bs].set(c)
    mod = _adaln(cond, w_ada, b_ada, _pick(w_ada.shape[-1], 1024)).reshape(depth, MOD_ROWS, N_MOD, d)

    ql, kl, r, nh = cfg.q_lora, cfg.kv_lora, cfg.rope, cfg.n_heads
    w_kr = w_mla_in[..., ql + kl:]
    w_in_p = cast(jnp.concatenate(
        [w_mla_in[..., :ql + kl], _pad_lanes(w_kr), _pad_lanes(_swap_halves(w_kr, r))], axis=-1))
    uq = w_mla_uq.reshape(w_mla_uq.shape[0], ql, nh, cfg.nope + r)
    flat = lambda w: cast(w.reshape(w.shape[0], ql, nh * LANES))
    w_qn, w_qr, w_qs = flat(uq[..., :cfg.nope]), flat(_pad_lanes(uq[..., cfg.nope:])), flat(
        _pad_lanes(_swap_halves(uq[..., cfg.nope:], r)))
    w_uk, w_uv, w_o = cast(w_mla_uk), cast(w_mla_uv), cast(w_mla_o)
    g_q, g_kv = row3(g_mla_q), row3(g_mla_kv)
    cos, sin = _rope_tables(ss, cfg)

    w_gates = cast(0.5 * jnp.concatenate([w_rec_gx[:, 0], w_rec_ga[:, 0], w_rec_gx[:, 1], w_rec_ga[:, 1]], axis=-1))
    nb = cfg.lru_blocks
    blk = lambda b: b.reshape(b.shape[0], nb, 1, LANES)
    b_gates = 0.5 * jnp.concatenate(
        [blk(b_rec_gx[:, 0]), blk(b_rec_ga[:, 0]), blk(b_rec_gx[:, 1]), blk(b_rec_ga[:, 1])], axis=-1)
    tf = _pick(w_ffn_down.shape[1], 512)
    w_rin, w_rout = cast(w_rec_in), cast(w_rec_out)
    w_up, w_down = cast(w_ffn_up), cast(w_ffn_down)
    g_mix3, g_ffn3, b_rconv, b_fconv = row3(g_mix), row3(g_ffn), row3(b_rec_conv), row3(b_ffn_conv)

    ctx_row = lambda i, tm: 0

    def run_group(x, seq, mod_row, sample):
        n = x.shape[0]
        rows = lambda pref: min(seq, pref) if sample else _pick(n, pref)
        tm_ffn = seq if sample else max(seq, _pick(n, 1024))
        ckv_new, kr_new, lru_new = [], [], []
        for layer in range(depth):
            j = layer // 2
            if layer % 2 == 0:
                proj = _norm_mod_matmul(x, mod, mod_row, g_mix3, w_in_p, layer, j, tm=rows(1024),
                                        tn=w_in_p.shape[-1], eps=eps)
                if sample:
                    qn, qr, kn, v, krp = _mla_mid(proj, g_q, g_kv, w_qn, w_qr, w_qs, w_uk, w_uv, j, cos, sin,
                                                  cfg=cfg, seq=seq, tm=rows(256))
                    cache = _cache_expand(cache_ckv, cache_krope, j, w_uk, w_uv, cfg=cfg)
                    att = _attention(qn, qr, kn, krp, v, cache, cfg=cfg, seq=seq, hb=min(4, nh))
                else:
                    qn, qr, kn, v, krp, ckv, kr = _mla_mid(proj, g_q, g_kv, w_qn, w_qr, None, w_uk, w_uv, j,
                                                           None, None, cfg=cfg, seq=seq, tm=rows(256))
                    ckv_new.append(ckv)
                    kr_new.append(kr)
                    att = _attention(qn, qr, kn, krp, v, None, cfg=cfg, seq=seq, hb=nh)
                x = _matmul_residual(att, w_o, j, x, mod, layer, mod_row, tm=rows(1024))
            else:
                proj = _rec_in(x, mod, mod_row, g_mix3, w_rin, w_rec_conv, b_rconv, layer, j, seq=seq,
                               tm=max(seq, rows(1024)), tn=_pick(w_rin.shape[-1] // 2, 1024), eps=eps)
                outs = _rec_scan(proj, w_gates, b_gates, rec_lambda, j,
                                 (state_lru[:, j, 0], state_lru[:, j, 1]) if sample else None, cfg=cfg, seq=seq)
                if not sample:
                    lru_new.append(jnp.stack(outs[1:], axis=1))
                x = _matmul_residual(outs[0], w_rout, j, x, mod, layer, mod_row, tm=rows(1024))
            x = _conv_ffn(x, mod, layer, mod_row, g_ffn3, w_up, w_ffn_conv, b_fconv, w_down,
                          g_final[None] if layer == depth - 1 else None, seq=seq, tm=tm_ffn, tf=tf,
                          n_sub=2 if tf % (2 * LANES) == 0 else 1, x_buffers=2, eps=eps)
        return x, ckv_new, kr_new, lru_new

    y_p, ckv_new, kr_new, lru_new = run_group(x_prompt.reshape(bp * sp, d), sp, ctx_row, False)
    y_s, _, _, _ = run_group(x_sample.reshape(bs * ss, d), ss, lambda i, tm: 1 + (i * tm) // ss, True)
    new_ckv = jnp.stack([t.reshape(bp, sp, kl) for t in ckv_new], axis=1)
    new_kr = jnp.stack([t.reshape(bp, sp, r) for t in kr_new], axis=1)
    new_lru = jnp.stack(lru_new, axis=1)
    return y_p.reshape(bp, sp, d), y_s.reshape(bs, ss, d), new_ckv, new_kr, new_lru


CFG = Cfg(n_heads=16, q_lora=512, kv_lora=512, nope=128, rope=64, v_dim=128, grid_w=64, rope_base=10000.0,
          lru_blocks=16, lru_c=8.0, eps=1e-6)


def kernel(x_prompt, x_sample, cache_ckv, cache_krope, state_lru, c, c_ctx, g_mix, g_ffn, g_final, w_ada, b_ada, w_mla_in, g_mla_q, g_mla_kv, w_mla_uq, w_mla_uk, w_mla_uv, w_mla_o, w_rec_in, w_rec_conv, b_rec_conv, w_rec_gx, b_rec_gx, w_rec_ga, b_rec_ga, rec_lambda, w_rec_out, w_ffn_up, w_ffn_conv, b_ffn_conv, w_ffn_down):
    return _forward(CFG, x_prompt, x_sample, cache_ckv, cache_krope, state_lru, c, c_ctx, g_mix, g_ffn, g_final,
                    w_ada, b_ada, w_mla_in, g_mla_q, g_mla_kv, w_mla_uq, w_mla_uk, w_mla_uv, w_mla_o,
                    w_rec_in, w_rec_conv, b_rec_conv, w_rec_gx, b_rec_gx, w_rec_ga, b_rec_ga, rec_lambda, w_rec_out,
                    w_ffn_up, w_ffn_conv, b_ffn_conv, w_ffn_down)
```

```python
import functools
import math
from typing import NamedTuple

import jax
import jax.numpy as jnp
from jax import lax
from jax.experimental import pallas as pl
from jax.experimental.pallas import tpu as pltpu

F32 = jnp.float32
MM_DTYPE = jnp.bfloat16

LANES = 128
SUBLANES = 8
VMEM_LIMIT_BYTES = 60 * 1024 * 1024
N_MOD = 6
MOD_ROWS = 16


class Cfg(NamedTuple):
    n_heads: int
    q_lora: int
    kv_lora: int
    nope: int
    rope: int
    v_dim: int
    grid_w: int
    rope_base: float
    lru_blocks: int
    lru_c: float
    eps: float


def _params(sem):
    return pltpu.CompilerParams(dimension_semantics=sem, vmem_limit_bytes=VMEM_LIMIT_BYTES)


def _dot(a, b):
    return jnp.dot(a, b, preferred_element_type=F32)


def _dot_nt(a, b):
    return lax.dot_general(a, b, (((1,), (1,)), ((), ())), preferred_element_type=F32)


def _const_spec(shape):
    nd = len(shape)
    return pl.BlockSpec(shape, lambda *_: (0,) * nd)


def _layer_spec(arr, layer):
    nd = arr.ndim
    return pl.BlockSpec((None,) + arr.shape[1:], lambda *_: (layer,) + (0,) * (nd - 1))


def _mod_spec(d, layer, mod_row, tm):
    return pl.BlockSpec((None, None, N_MOD, d), lambda i, *_: (layer, mod_row(i, tm), 0, 0))


def _rms(x, g, eps):
    return x * lax.rsqrt(jnp.mean(x * x, axis=-1, keepdims=True) + eps) * g


def _norm_mod(x, g, mod_ref, shift_idx, scale_idx, eps):
    y = _rms(x, g, eps)
    return y * (1.0 + mod_ref[scale_idx:scale_idx + 1, :]) + mod_ref[shift_idx:shift_idx + 1, :]


ROW_PAD = SUBLANES


def _zero_row_pads(u_ref, rows):
    lead = (slice(None),) * (len(u_ref.shape) - 2)
    zeros = jnp.zeros(u_ref.shape[:-2] + (ROW_PAD, u_ref.shape[-1]), u_ref.dtype)
    u_ref[lead + (slice(0, ROW_PAD), slice(None))] = zeros
    u_ref[lead + (slice(ROW_PAD + rows, ROW_PAD + rows + ROW_PAD), slice(None))] = zeros


def _store_slabs(u_ref, first, u):
    for k in range(u.shape[1] // LANES):
        u_ref[first + k, ROW_PAD:ROW_PAD + u.shape[0], :] = u[:, k * LANES:(k + 1) * LANES]


def _seq_window(u_ref, off, rows, seq):
    w = u_ref[ROW_PAD + off:ROW_PAD + off + rows, :]
    if off == 0 or seq == rows:
        return w
    sub_row = lax.broadcasted_iota(jnp.int32, (SUBLANES, 1), 0)
    pieces = []
    for base in range(0, rows, seq):
        if off < 0:
            tile, bad = base, sub_row < -off
        else:
            tile, bad = base + seq - SUBLANES, sub_row >= SUBLANES - off
        pieces += [w[base:tile], jnp.where(bad, 0.0, w[tile:tile + SUBLANES]), w[tile + SUBLANES:base + seq]]
    return jnp.concatenate([p for p in pieces if p.shape[0]], axis=0)


def _adaln_body(cond_ref, w_ref, b_ref, o_ref):
    c = cond_ref[...]
    s = (c * jax.nn.sigmoid(c)).astype(MM_DTYPE)
    o_ref[...] = _dot(s, w_ref[...].astype(MM_DTYPE)) + b_ref[...]


def _adaln(cond, w_ada, b_ada, tn):
    n_layers, d, n_out = w_ada.shape
    return pl.pallas_call(
        _adaln_body,
        out_shape=jax.ShapeDtypeStruct((n_layers, MOD_ROWS, n_out), F32),
        grid=(n_layers, n_out // tn),
        in_specs=[
            _const_spec((MOD_ROWS, d)),
            pl.BlockSpec((None, d, tn), lambda l, n: (l, 0, n)),
            pl.BlockSpec((None, 1, tn), lambda l, n: (l, 0, n)),
        ],
        out_specs=pl.BlockSpec((None, MOD_ROWS, tn), lambda l, n: (l, 0, n)),
        compiler_params=_params(("arbitrary", "arbitrary")),
        name="adaln",
    )(cond, w_ada, b_ada.reshape(n_layers, 1, n_out))


def _nmm_body(x_ref, mod_ref, g_ref, w_ref, o_ref, h_ref, *, eps):
    @pl.when(pl.program_id(1) == 0)
    def _():
        h = _norm_mod(x_ref[...], g_ref[...], mod_ref, 0, 1, eps).astype(MM_DTYPE)
        h_ref[...] = h
        o_ref[...] = _dot(h, w_ref[...])

    @pl.when(pl.program_id(1) > 0)
    def _():
        o_ref[...] = _dot(h_ref[...], w_ref[...])


def _norm_mod_matmul(x, mod, mod_row, g, w, layer, wl, *, tm, tn, eps):
    n, d = x.shape
    m = w.shape[2]
    return pl.pallas_call(
        functools.partial(_nmm_body, eps=eps),
        out_shape=jax.ShapeDtypeStruct((n, m), F32),
        grid=(n // tm, m // tn),
        in_specs=[
            pl.BlockSpec((tm, d), lambda i, j: (i, 0)),
            _mod_spec(d, layer, mod_row, tm),
            _layer_spec(g, layer),
            pl.BlockSpec((None, d, tn), lambda i, j: (wl, 0, j)),
        ],
        out_specs=pl.BlockSpec((tm, tn), lambda i, j: (i, j)),
        scratch_shapes=[pltpu.VMEM((tm, d), MM_DTYPE)],
        compiler_params=_params(("arbitrary", "arbitrary")),
        name="norm_mod_matmul",
    )(x, mod, g, w)


def _rec_in_body(x_ref, mod_ref, g_ref, w_ref, cw_ref, cb_ref, o_ref, h_ref, u_ref, *, seq, n_y, eps):
    j = pl.program_id(1)
    tm = h_ref.shape[0]

    def gelu_tile(h):
        y = jax.nn.gelu(_dot(h, w_ref[...]))
        for c in range(o_ref.shape[0]):
            o_ref[c] = y[:, c * LANES:(c + 1) * LANES]

    @pl.when(j == 0)
    def _():
        h = _norm_mod(x_ref[...], g_ref[...], mod_ref, 0, 1, eps).astype(MM_DTYPE)
        h_ref[...] = h
        _zero_row_pads(u_ref, tm)
        gelu_tile(h)

    @pl.when((j > 0) & (j < n_y))
    def _():
        gelu_tile(h_ref[...])

    @pl.when(j >= n_y)
    def _():
        _store_slabs(u_ref, 0, _dot(h_ref[...], w_ref[...]))
        for c in range(u_ref.shape[0]):
            sl = slice(c * LANES, (c + 1) * LANES)
            xc = cb_ref[:, sl]
            for tap, off in enumerate((-2, -1, 0, 1)):
                xc = xc + _seq_window(u_ref.at[c], off, tm, seq) * cw_ref[tap:tap + 1, sl]
            o_ref[c] = xc


def _rec_in(x, mod, mod_row, g, w, w_conv, b_conv, layer, wl, *, seq, tm, tn, eps):
    n, d = x.shape
    m = w.shape[2]
    n_y = m // 2 // tn
    assert tm % seq == 0 and w_conv.shape[1] == 4 and m % (2 * tn) == 0
    conv_col = lambda i, j: (wl, 0, jnp.maximum(j - n_y, 0))
    return pl.pallas_call(
        functools.partial(_rec_in_body, seq=seq, n_y=n_y, eps=eps),
        out_shape=jax.ShapeDtypeStruct((m // LANES, n, LANES), F32),
        grid=(n // tm, m // tn),
        in_specs=[
            pl.BlockSpec((tm, d), lambda i, j: (i, 0)),
            _mod_spec(d, layer, mod_row, tm),
            _layer_spec(g, layer),
            pl.BlockSpec((None, d, tn), lambda i, j: (wl, 0, j)),
            pl.BlockSpec((None, 4, tn), conv_col),
            pl.BlockSpec((None, 1, tn), conv_col),
        ],
        out_specs=pl.BlockSpec((tn // LANES, tm, LANES), lambda i, j: (j, i, 0)),
        scratch_shapes=[pltpu.VMEM((tm, d), MM_DTYPE), pltpu.VMEM((tn // LANES, ROW_PAD + tm + ROW_PAD, LANES), F32)],
        compiler_params=_params(("arbitrary", "arbitrary")),
        name="rec_in",
    )(x, mod, g, w, w_conv, b_conv)


def _mmres_body(a_ref, w_ref, x_ref, mod_ref, o_ref):
    o_ref[...] = x_ref[...] + mod_ref[2:3, :] * _dot(a_ref[...], w_ref[...])


def _matmul_residual(a, w, wl, x, mod, layer, mod_row, *, tm):
    n, k = a.shape
    d = w.shape[2]
    return pl.pallas_call(
        _mmres_body,
        out_shape=jax.ShapeDtypeStruct((n, d), F32),
        grid=(n // tm,),
        in_specs=[
            pl.BlockSpec((tm, k), lambda i: (i, 0)),
            _layer_spec(w, wl),
            pl.BlockSpec((tm, d), lambda i: (i, 0)),
            _mod_spec(d, layer, mod_row, tm),
        ],
        out_specs=pl.BlockSpec((tm, d), lambda i: (i, 0)),
        compiler_params=_params(("arbitrary",)),
        name="matmul_residual",
    )(a, w, x, mod)


def _ffn_body(*refs, seq, eps, n_sub, final):
    if final:
        (x_ref, mod_ref, g_ref, wg_ref, wv_ref, cwg_ref, cwv_ref, cbg_ref, cbv_ref, wd_ref, gf_ref,
         o_ref, h_ref, act_ref, ug_ref, uv_ref) = refs
    else:
        (x_ref, mod_ref, g_ref, wg_ref, wv_ref, cwg_ref, cwv_ref, cbg_ref, cbv_ref, wd_ref,
         o_ref, h_ref, act_ref, ug_ref, uv_ref) = refs
    j = pl.program_id(1)
    n_tiles = pl.num_programs(1) - 1
    tm = h_ref.shape[0]
    ts = wg_ref.shape[1] // n_sub
    subs = [slice(s * ts, (s + 1) * ts) for s in range(n_sub)]

    def conv(u_ref, c, cw, cb):
        win = lambda off: _seq_window(u_ref.at[c], off, tm, seq)
        return cb + win(-1) * cw[0:1, :] + win(0) * cw[1:2, :] + win(1) * cw[2:3, :]

    def up():
        h = h_ref[...]
        for s, sl in enumerate(subs):
            _store_slabs(ug_ref, s * (ts // LANES), _dot(h, wg_ref[:, sl]))
            _store_slabs(uv_ref, s * (ts // LANES), _dot(h, wv_ref[:, sl]))

    def activate():
        for c in range(ug_ref.shape[0]):
            sl = slice(c * LANES, (c + 1) * LANES)
            hg = conv(ug_ref, c, 0.5 * cwg_ref[:, sl], 0.5 * cbg_ref[:, sl])
            val = conv(uv_ref, c, cwv_ref[:, sl], cbv_ref[:, sl])
            act_ref[:, sl] = (hg * (jnp.tanh(hg) + 1.0) * val).astype(MM_DTYPE)

    def down(act):
        o_ref[...] += _dot(act, wd_ref[...])

    @pl.when(j == 0)
    def _():
        h = _norm_mod(x_ref[...], g_ref[...], mod_ref, 3, 4, eps)
        h_ref[...] = h.astype(MM_DTYPE)
        o_ref[...] = jnp.zeros_like(o_ref)
        for u_ref in (ug_ref, uv_ref):
            _zero_row_pads(u_ref, tm)
        up()
        activate()

    @pl.when((j > 0) & (j < n_tiles))
    def _():
        down(act_ref[...])
        up()
        activate()

    @pl.when(j == n_tiles)
    def _():
        down(act_ref[...])
        y = x_ref[...] + mod_ref[5:6, :] * o_ref[...]
        if final:
            y = _rms(y, gf_ref[...], eps)
        o_ref[...] = y


def _conv_ffn(x, mod, layer, mod_row, g, w_up, w_conv, b_conv, w_down, g_final, *, seq, tm, tf, n_sub, x_buffers,
              eps):
    n, d = x.shape
    f = w_down.shape[1]
    assert w_conv.shape[1:] == (3, 2 * f) and tm % seq == 0 and f % tf == 0 and tf % (n_sub * LANES) == 0
    nf = f // tf
    final = g_final is not None
    up_tile = lambda j: jnp.minimum(j, nf - 1)
    lo = lambda rows: pl.BlockSpec((None, rows, tf), lambda i, j: (layer, 0, up_tile(j)))
    hi = lambda rows: pl.BlockSpec((None, rows, tf), lambda i, j: (layer, 0, up_tile(j) + nf))
    ins = [x, mod, g, w_up, w_up, w_conv, w_conv, b_conv, b_conv, w_down] + ([g_final] if final else [])
    in_specs = [
        pl.BlockSpec((tm, d), lambda i, j: (i, 0), pipeline_mode=pl.Buffered(x_buffers)),
        _mod_spec(d, layer, mod_row, tm),
        _layer_spec(g, layer),
        lo(d), hi(d), lo(3), hi(3), lo(1), hi(1),
        pl.BlockSpec((None, tf, d), lambda i, j: (layer, jnp.maximum(j - 1, 0), 0)),
    ] + ([_const_spec(g_final.shape)] if final else [])
    return pl.pallas_call(
        functools.partial(_ffn_body, seq=seq, eps=eps, n_sub=n_sub, final=final),
        out_shape=jax.ShapeDtypeStruct((n, d), F32),
        grid=(n // tm, nf + 1),
        in_specs=in_specs,
        out_specs=pl.BlockSpec((tm, d), lambda i, j: (i, 0)),
        scratch_shapes=[pltpu.VMEM((tm, d), MM_DTYPE), pltpu.VMEM((tm, tf), MM_DTYPE)]
        + [pltpu.VMEM((tf // LANES, ROW_PAD + tm + ROW_PAD, LANES), F32)] * 2,
        compiler_params=_params(("arbitrary", "arbitrary")),
        name="conv_ffn_final" if final else "conv_ffn",
    )(*ins)


def _attn_log2_scale(cfg):
    return float(cfg.nope + cfg.rope) ** -0.5 * math.log2(math.e)


def _mla_mid_body(*refs, cfg, rope):
    if rope:
        (proj_ref, gq_ref, gkv_ref, wqn_ref, wqr_ref, wqs_ref, wuk_ref, wuv_ref, cos_ref, sin_ref,
         qn_ref, qr_ref, kn_ref, v_ref, krp_ref) = refs
    else:
        (proj_ref, gq_ref, gkv_ref, wqn_ref, wqr_ref, wuk_ref, wuv_ref,
         qn_ref, qr_ref, kn_ref, v_ref, krp_ref, ckv_ref, kr_ref) = refs
    ql, kl = cfg.q_lora, cfg.kv_lora
    p = proj_ref[...]
    cqn = _rms(p[:, :ql], gq_ref[...], cfg.eps).astype(MM_DTYPE)
    ckv = _rms(p[:, ql:ql + kl], gkv_ref[...], cfg.eps)
    krp = p[:, ql + kl:ql + kl + LANES]
    qk_scale = _attn_log2_scale(cfg)
    qn_ref[...] = (_dot(cqn, wqn_ref[...]) * qk_scale).astype(MM_DTYPE)
    qr = _dot(cqn, wqr_ref[...])
    if rope:
        krs = p[:, ql + kl + LANES:ql + kl + 2 * LANES]
        qs = _dot(cqn, wqs_ref[...])
        cos = cos_ref[...]
        sin = sin_ref[...]
        for h in range(cfg.n_heads):
            sl = slice(h * LANES, (h + 1) * LANES)
            qr_ref[:, sl] = ((qr[:, sl] * cos + qs[:, sl] * sin) * qk_scale).astype(MM_DTYPE)
        krp_ref[...] = (krp * cos + krs * sin).astype(MM_DTYPE)
    else:
        qr_ref[...] = (qr * qk_scale).astype(MM_DTYPE)
        krp_ref[...] = krp.astype(MM_DTYPE)
        ckv_ref[...] = ckv
        kr_ref[...] = krp[:, :cfg.rope]
    ckv_b = ckv.astype(MM_DTYPE)
    kn_ref[...] = _dot(ckv_b, wuk_ref[...]).astype(MM_DTYPE)
    v_ref[...] = _dot(ckv_b, wuv_ref[...]).astype(MM_DTYPE)


def _mla_mid(proj, g_q, g_kv, w_qn, w_qr, w_qs, w_uk, w_uv, wl, cos, sin, *, cfg, seq, tm):
    n, pw = proj.shape
    rope = cos is not None
    hw = cfg.n_heads * LANES
    row = lambda i: (i, 0)
    ws = [g_q, g_kv, w_qn, w_qr] + ([w_qs] if rope else []) + [w_uk, w_uv]
    ins = [proj] + ws
    in_specs = [pl.BlockSpec((tm, pw), row)] + [_layer_spec(w, wl) for w in ws]
    out_shape = [jax.ShapeDtypeStruct((n, hw), MM_DTYPE)] * 4 + [jax.ShapeDtypeStruct((n, LANES), MM_DTYPE)]
    out_specs = [pl.BlockSpec((tm, hw), row)] * 4 + [pl.BlockSpec((tm, LANES), row)]
    if rope:
        bps = seq // tm
        ins += [cos, sin]
        in_specs += [pl.BlockSpec((tm, LANES), lambda i: (i % bps, 0))] * 2
    else:
        out_shape += [jax.ShapeDtypeStruct((n, cfg.kv_lora), F32), jax.ShapeDtypeStruct((n, cfg.rope), F32)]
        out_specs += [pl.BlockSpec((tm, cfg.kv_lora), row), pl.BlockSpec((tm, cfg.rope), row)]
    return pl.pallas_call(
        functools.partial(_mla_mid_body, cfg=cfg, rope=rope),
        out_shape=out_shape,
        grid=(n // tm,),
        in_specs=in_specs,
        out_specs=out_specs,
        compiler_params=_params(("arbitrary",)),
        name="mla_mid_rope" if rope else "mla_mid",
    )(*ins)


def _cache_expand_body(ckv_ref, kr_ref, wuk_ref, wuv_ref, kn_ref, v_ref, krp_ref, *, rope):
    c = ckv_ref[...].astype(MM_DTYPE)
    kn_ref[...] = _dot(c, wuk_ref[...]).astype(MM_DTYPE)
    v_ref[...] = _dot(c, wuv_ref[...]).astype(MM_DTYPE)
    krp_ref[:, :rope] = kr_ref[...].astype(MM_DTYPE)
    krp_ref[:, rope:] = jnp.zeros((krp_ref.shape[0], LANES - rope), MM_DTYPE)


def _cache_expand(cache_ckv, cache_krope, wl, w_uk, w_uv, *, cfg):
    b, _, p, kl = cache_ckv.shape
    hw = cfg.n_heads * LANES
    row = lambda i: (i, 0)
    return pl.pallas_call(
        functools.partial(_cache_expand_body, rope=cfg.rope),
        out_shape=[jax.ShapeDtypeStruct((b * p, hw), MM_DTYPE)] * 2 + [jax.ShapeDtypeStruct((b * p, LANES), MM_DTYPE)],
        grid=(b,),
        in_specs=[
            pl.BlockSpec((None, None, p, kl), lambda i: (i, wl, 0, 0)),
            pl.BlockSpec((None, None, p, cfg.rope), lambda i: (i, wl, 0, 0)),
            _layer_spec(w_uk, wl),
            _layer_spec(w_uv, wl),
        ],
        out_specs=[pl.BlockSpec((p, hw), row)] * 2 + [pl.BlockSpec((p, LANES), row)],
        compiler_params=_params(("arbitrary",)),
        name="cache_expand",
    )(cache_ckv, cache_krope, w_uk, w_uv)


def _attn_body(*refs, hb, cached, q_chunk):
    if cached:
        qn_ref, qr_ref, kn_ref, kr_ref, v_ref, knc_ref, krc_ref, vc_ref, o_ref = refs
    else:
        qn_ref, qr_ref, kn_ref, kr_ref, v_ref, o_ref = refs
    sq = qn_ref.shape[0]
    qc = min(sq, q_chunk)
    for h in range(hb):
        sl = slice(h * LANES, (h + 1) * LANES)
        k = jnp.concatenate([kn_ref[:, sl], kr_ref[...]], axis=1)
        if cached:
            kc = jnp.concatenate([knc_ref[:, sl], krc_ref[...]], axis=1)
        for r0 in range(0, sq, qc):
            rows = slice(r0, r0 + qc)
            q = jnp.concatenate([qn_ref[rows, sl], qr_ref[rows, sl]], axis=1)
            s = _dot_nt(q, k)
            m = jnp.max(s, axis=-1, keepdims=True)
            if cached:
                sc = _dot_nt(q, kc)
                m = jnp.maximum(m, jnp.max(sc, axis=-1, keepdims=True))
                pc = jnp.exp2(sc - m)
            p = jnp.exp2(s - m)
            den = jnp.sum(p, axis=-1, keepdims=True)
            o = _dot(p.astype(MM_DTYPE), v_ref[:, sl])
            if cached:
                den = den + jnp.sum(pc, axis=-1, keepdims=True)
                o = o + _dot(pc.astype(MM_DTYPE), vc_ref[:, sl])
            o_ref[rows, sl] = (o / den).astype(MM_DTYPE)


def _attention(qn, qr, kn, krp, v, cache, *, cfg, seq, hb):
    n, hw = qn.shape
    cached = cache is not None
    blk = lambda rows: pl.BlockSpec((rows, hb * LANES), lambda b, g: (b, g))
    shared = lambda rows: pl.BlockSpec((rows, LANES), lambda b, g: (b, 0))
    ins = [qn, qr, kn, krp, v]
    in_specs = [blk(seq), blk(seq), blk(seq), shared(seq), blk(seq)]
    if cached:
        knc, vc, krc = cache
        past = knc.shape[0] // (n // seq)
        ins += [knc, krc, vc]
        in_specs += [blk(past), shared(past), blk(past)]
    return pl.pallas_call(
        functools.partial(_attn_body, hb=hb, cached=cached, q_chunk=512),
        out_shape=jax.ShapeDtypeStruct((n, hw), MM_DTYPE),
        grid=(n // seq, cfg.n_heads // hb),
        in_specs=in_specs,
        out_specs=blk(seq),
        compiler_params=_params(("arbitrary", "arbitrary")),
        name="attention_cached" if cached else "attention",
    )(*ins)


def _rec_scan_body(*refs, seq, cfg, has_state):
    if has_state:
        (yg_ref, xc_ref, wg_ref, bg_ref, lam_ref, h0f_ref, h0b_ref,
         o_ref, af_ref, uf_ref, ab_ref, ub_ref, hf_ref, hb_ref) = refs
    else:
        (yg_ref, xc_ref, wg_ref, bg_ref, lam_ref,
         o_ref, sf_ref, sb_ref, af_ref, uf_ref, ab_ref, ub_ref, hf_ref, hb_ref) = refs
    nseq = SUBLANES
    pitch = seq + 4
    lam = lam_ref[...]
    softplus = jnp.maximum(-lam, 0.0) + jnp.log1p(jnp.exp(-jnp.abs(lam)))
    half_decay = (-0.5 * cfg.lru_c * math.log2(math.e)) * softplus
    wg = wg_ref[...]
    bg = bg_ref[...]
    for b in range(nseq):
        xc = xc_ref[b * seq:(b + 1) * seq, :]
        th = jnp.tanh(_dot(xc.astype(MM_DTYPE), wg) + bg)
        xh = 0.5 * xc
        for d, (a_ref, u_ref) in enumerate(((af_ref, uf_ref), (ab_ref, ub_ref))):
            tx = th[:, (2 * d) * LANES:(2 * d + 1) * LANES]
            ta = th[:, (2 * d + 1) * LANES:(2 * d + 2) * LANES]
            hd = half_decay[d:d + 1, :]
            a = jnp.exp2(ta * hd + hd)
            z = 1.0 - a * a
            u = jnp.where(z > 0.0, z * lax.rsqrt(z), 0.0) * ((tx + 1.0) * xh)
            a_ref[b * pitch:b * pitch + seq, :] = a
            u_ref[b * pitch:b * pitch + seq, :] = u

    if has_state:
        hf0 = h0f_ref[...]
        hb0 = h0b_ref[...]
    else:
        hf0 = jnp.zeros((nseq, LANES), F32)
        hb0 = hf0

    def step(t, carry):
        hf, hb = carry
        fwd = pl.ds(t, nseq, stride=pitch)
        hf = af_ref[fwd, :] * hf + uf_ref[fwd, :]
        hf_ref[fwd, :] = hf
        bwd = pl.ds(seq - 1 - t, nseq, stride=pitch)
        hb = ab_ref[bwd, :] * hb + ub_ref[bwd, :]
        hb_ref[bwd, :] = hb
        return hf, hb

    hf, hb = lax.fori_loop(0, seq, step, (hf0, hb0), unroll=8)
    if not has_state:
        sf_ref[...] = hf
        sb_ref[...] = hb
    for b in range(nseq):
        hsum = hf_ref[b * pitch:b * pitch + seq, :] + hb_ref[b * pitch:b * pitch + seq, :]
        o_ref[b * seq:(b + 1) * seq, :] = (hsum * yg_ref[b * seq:(b + 1) * seq, :]).astype(MM_DTYPE)


def _rec_scan(proj, w_gates, b_gates, lam, wl, state, *, cfg, seq):
    nb2, n, _ = proj.shape
    nb = nb2 // 2
    w = nb * LANES
    assert nb == cfg.lru_blocks
    rows = SUBLANES * seq
    has_state = state is not None
    col = lambda g, c: (g, c)
    ins = [proj, proj, w_gates, b_gates, lam]
    in_specs = [
        pl.BlockSpec((None, rows, LANES), lambda g, c: (c, g, 0)),
        pl.BlockSpec((None, rows, LANES), lambda g, c: (c + nb, g, 0)),
        pl.BlockSpec((None, None, LANES, 4 * LANES), lambda g, c: (wl, c, 0, 0)),
        pl.BlockSpec((None, None, 1, 4 * LANES), lambda g, c: (wl, c, 0, 0)),
        pl.BlockSpec((None, 2, LANES), lambda g, c: (wl, 0, c)),
    ]
    out_shape = [jax.ShapeDtypeStruct((n, w), MM_DTYPE)]
    out_specs = [pl.BlockSpec((rows, LANES), col)]
    if has_state:
        ins += list(state)
        in_specs += [pl.BlockSpec((SUBLANES, LANES), col)] * 2
    else:
        nstate = n // seq
        out_shape += [jax.ShapeDtypeStruct((nstate, w), F32)] * 2
        out_specs += [pl.BlockSpec((SUBLANES, LANES), col)] * 2
    scratch = [pltpu.VMEM((SUBLANES * (seq + 4), LANES), F32)] * 6
    return pl.pallas_call(
        functools.partial(_rec_scan_body, seq=seq, cfg=cfg, has_state=has_state),
        out_shape=out_shape,
        grid=(n // rows, nb),
        in_specs=in_specs,
        out_specs=out_specs,
        scratch_shapes=scratch,
        compiler_params=_params(("arbitrary", "arbitrary")),
        name="rec_scan_state" if has_state else "rec_scan",
    )(*ins)


def _swap_halves(w, rope):
    q = rope // 4
    return jnp.concatenate([w[..., q:2 * q], w[..., :q], w[..., 3 * q:], w[..., 2 * q:3 * q]], axis=-1)


def _pad_lanes(w):
    return jnp.pad(w, [(0, 0)] * (w.ndim - 1) + [(0, LANES - w.shape[-1])])


def _rope_tables(seq, cfg):
    rows = seq // cfg.grid_w
    row = jnp.repeat(jnp.arange(rows), cfg.grid_w).astype(F32)
    col = jnp.tile(jnp.arange(cfg.grid_w), rows).astype(F32)
    half = cfg.rope // 2
    inv = 1.0 / (cfg.rope_base ** (jnp.arange(0, half, 2, dtype=F32) / half))
    ar, ac = row[:, None] * inv, col[:, None] * inv
    cos = jnp.concatenate([jnp.cos(ar), jnp.cos(ar), jnp.cos(ac), jnp.cos(ac)], axis=-1)
    sin = jnp.concatenate([-jnp.sin(ar), jnp.sin(ar), -jnp.sin(ac), jnp.sin(ac)], axis=-1)
    return _pad_lanes(cos), _pad_lanes(sin)


def _pick(n, pref):
    return pref if n % pref == 0 else n


def _forward(cfg, x_prompt, x_sample, cache_ckv, cache_krope, state_lru, c, c_ctx,
             g_mix, g_ffn, g_final, w_ada, b_ada,
             w_mla_in, g_mla_q, g_mla_kv, w_mla_uq, w_mla_uk, w_mla_uv, w_mla_o,
             w_rec_in, w_rec_conv, b_rec_conv, w_rec_gx, b_rec_gx, w_rec_ga, b_rec_ga,
             rec_lambda, w_rec_out,
             w_ffn_up, w_ffn_conv, b_ffn_conv, w_ffn_down):
    depth, d = g_mix.shape
    bp, sp, _ = x_prompt.shape
    bs, ss, _ = x_sample.shape
    assert cfg.nope == LANES and cfg.v_dim == LANES and cfg.rope <= LANES and bs + 1 <= MOD_ROWS
    assert bp % SUBLANES == 0 and bs % SUBLANES == 0
    eps = cfg.eps
    cast = lambda w: w.astype(MM_DTYPE)
    row3 = lambda v: v[:, None, :]

    cond = jnp.zeros((MOD_ROWS, d), F32).at[0].set(c_ctx).at[1:1 + bs].set(c)
    mod = _adaln(cond, w_ada, b_ada, _pick(w_ada.shape[-1], 1024)).reshape(depth, MOD_ROWS, N_MOD, d)

    ql, kl, r, nh = cfg.q_lora, cfg.kv_lora, cfg.rope, cfg.n_heads
    w_kr = w_mla_in[..., ql + kl:]
    w_in_p = cast(jnp.concatenate(
        [w_mla_in[..., :ql + kl], _pad_lanes(w_kr), _pad_lanes(_swap_halves(w_kr, r))], axis=-1))
    uq = w_mla_uq.reshape(w_mla_uq.shape[0], ql, nh, cfg.nope + r)
    flat = lambda w: cast(w.reshape(w.shape[0], ql, nh * LANES))
    w_qn, w_qr, w_qs = flat(uq[..., :cfg.nope]), flat(_pad_lanes(uq[..., cfg.nope:])), flat(
        _pad_lanes(_swap_halves(uq[..., cfg.nope:], r)))
    w_uk, w_uv, w_o = cast(w_mla_uk), cast(w_mla_uv), cast(w_mla_o)
    g_q, g_kv = row3(g_mla_q), row3(g_mla_kv)
    cos, sin = _rope_tables(ss, cfg)

    w_gates = cast(0.5 * jnp.concatenate([w_rec_gx[:, 0], w_rec_ga[:, 0], w_rec_gx[:, 1], w_rec_ga[:, 1]], axis=-1))
    nb = cfg.lru_blocks
    blk = lambda b: b.reshape(b.shape[0], nb, 1, LANES)
    b_gates = 0.5 * jnp.concatenate(
        [blk(b_rec_gx[:, 0]), blk(b_rec_ga[:, 0]), blk(b_rec_gx[:, 1]), blk(b_rec_ga[:, 1])], axis=-1)
    tf = _pick(w_ffn_down.shape[1], 512)
    w_rin, w_rout = cast(w_rec_in), cast(w_rec_out)
    w_up, w_down = cast(w_ffn_up), cast(w_ffn_down)
    g_mix3, g_ffn3, b_rconv, b_fconv = row3(g_mix), row3(g_ffn), row3(b_rec_conv), row3(b_ffn_conv)

    ctx_row = lambda i, tm: 0

    def run_group(x, seq, mod_row, sample):
        n = x.shape[0]
        rows = lambda pref: min(seq, pref) if sample else _pick(n, pref)
        tm_ffn = seq if sample else max(seq, _pick(n, 1024))
        ckv_new, kr_new, lru_new = [], [], []
        for layer in range(depth):
            j = layer // 2
            if layer % 2 == 0:
                proj = _norm_mod_matmul(x, mod, mod_row, g_mix3, w_in_p, layer, j, tm=rows(1024),
                                        tn=w_in_p.shape[-1], eps=eps)
                if sample:
                    qn, qr, kn, v, krp = _mla_mid(proj, g_q, g_kv, w_qn, w_qr, w_qs, w_uk, w_uv, j, cos, sin,
                                                  cfg=cfg, seq=seq, tm=rows(512))
                    cache = _cache_expand(cache_ckv, cache_krope, j, w_uk, w_uv, cfg=cfg)
                    att = _attention(qn, qr, kn, krp, v, cache, cfg=cfg, seq=seq, hb=min(4, nh))
                else:
                    qn, qr, kn, v, krp, ckv, kr = _mla_mid(proj, g_q, g_kv, w_qn, w_qr, None, w_uk, w_uv, j,
                                                           None, None, cfg=cfg, seq=seq, tm=rows(512))
                    ckv_new.append(ckv)
                    kr_new.append(kr)
                    att = _attention(qn, qr, kn, krp, v, None, cfg=cfg, seq=seq, hb=nh)
                x = _matmul_residual(att, w_o, j, x, mod, layer, mod_row, tm=rows(512))
            else:
                proj = _rec_in(x, mod, mod_row, g_mix3, w_rin, w_rec_conv, b_rconv, layer, j, seq=seq,
                               tm=max(seq, rows(1024)), tn=_pick(w_rin.shape[-1] // 2, 1024), eps=eps)
                outs = _rec_scan(proj, w_gates, b_gates, rec_lambda, j,
                                 (state_lru[:, j, 0], state_lru[:, j, 1]) if sample else None, cfg=cfg, seq=seq)
                if not sample:
                    lru_new.append(jnp.stack(outs[1:], axis=1))
                x = _matmul_residual(outs[0], w_rout, j, x, mod, layer, mod_row, tm=rows(512))
            x = _conv_ffn(x, mod, layer, mod_row, g_ffn3, w_up, w_ffn_conv, b_fconv, w_down,
                          g_final[None] if layer == depth - 1 else None, seq=seq, tm=tm_ffn, tf=tf,
                          n_sub=2 if tf % (2 * LANES) == 0 else 1, x_buffers=2, eps=eps)
        return x, ckv_new, kr_new, lru_new

    y_p, ckv_new, kr_new, lru_new = run_group(x_prompt.reshape(bp * sp, d), sp, ctx_row, False)
    y_s, _, _, _ = run_group(x_sample.reshape(bs * ss, d), ss, lambda i, tm: 1 + (i * tm) // ss, True)
    new_ckv = jnp.stack([t.reshape(bp, sp, kl) for t in ckv_new], axis=1)
    new_kr = jnp.stack([t.reshape(bp, sp, r) for t in kr_new], axis=1)
    new_lru = jnp.stack(lru_new, axis=1)
    return y_p.reshape(bp, sp, d), y_s.reshape(bs, ss, d), new_ckv, new_kr, new_lru


CFG = Cfg(n_heads=16, q_lora=512, kv_lora=512, nope=128, rope=64, v_dim=128, grid_w=64, rope_base=10000.0,
          lru_blocks=16, lru_c=8.0, eps=1e-6)


def kernel(x_prompt, x_sample, cache_ckv, cache_krope, state_lru, c, c_ctx, g_mix, g_ffn, g_final, w_ada, b_ada, w_mla_in, g_mla_q, g_mla_kv, w_mla_uq, w_mla_uk, w_mla_uv, w_mla_o, w_rec_in, w_rec_conv, b_rec_conv, w_rec_gx, b_rec_gx, w_rec_ga, b_rec_ga, rec_lambda, w_rec_out, w_ffn_up, w_ffn_conv, b_ffn_conv, w_ffn_down):
    return _forward(CFG, x_prompt, x_sample, cache_ckv, cache_krope, state_lru, c, c_ctx, g_mix, g_ffn, g_final,
                    w_ada, b_ada, w_mla_in, g_mla_q, g_mla_kv, w_mla_uq, w_mla_uk, w_mla_uv, w_mla_o,
                    w_rec_in, w_rec_conv, b_rec_conv, w_rec_gx, b_rec_gx, w_rec_ga, b_rec_ga, rec_lambda, w_rec_out,
                    w_ffn_up, w_ffn_conv, b_ffn_conv, w_ffn_down)
```

```python
import functools
import math
from typing import NamedTuple

import jax
import jax.numpy as jnp
from jax import lax
from jax.experimental import pallas as pl
from jax.experimental.pallas import tpu as pltpu

F32 = jnp.float32
MM_DTYPE = jnp.bfloat16

LANES = 128
SUBLANES = 8
VMEM_LIMIT_BYTES = 60 * 1024 * 1024
N_MOD = 6
MOD_ROWS = 16

TOKEN_BLOCK = 1024
RESIDUAL_BLOCK = 512
MLA_MID_BLOCK = 512
FF_TILE = 512
IN_PROJ_TILE = 1024
ADALN_TILE = 1024
ATTN_Q_CHUNK = 512
ATTN_CACHED_HEADS = 4


class Cfg(NamedTuple):
    n_heads: int
    q_lora: int
    kv_lora: int
    nope: int
    rope: int
    v_dim: int
    grid_w: int
    rope_base: float
    lru_blocks: int
    lru_c: float
    eps: float


def _params(sem):
    return pltpu.CompilerParams(dimension_semantics=sem, vmem_limit_bytes=VMEM_LIMIT_BYTES)


def _dot(a, b):
    return jnp.dot(a, b, preferred_element_type=F32)


def _dot_nt(a, b):
    return lax.dot_general(a, b, (((1,), (1,)), ((), ())), preferred_element_type=F32)


def _const_spec(shape):
    nd = len(shape)
    return pl.BlockSpec(shape, lambda *_: (0,) * nd)


def _layer_spec(arr, layer):
    nd = arr.ndim
    return pl.BlockSpec((None,) + arr.shape[1:], lambda *_: (layer,) + (0,) * (nd - 1))


def _mod_spec(d, layer, mod_row, tm):
    return pl.BlockSpec((None, None, N_MOD, d), lambda i, *_: (layer, mod_row(i, tm), 0, 0))


def _rms(x, g, eps):
    return x * lax.rsqrt(jnp.mean(x * x, axis=-1, keepdims=True) + eps) * g


def _norm_mod(x, g, mod_ref, shift_idx, scale_idx, eps):
    y = _rms(x, g, eps)
    return y * (1.0 + mod_ref[scale_idx:scale_idx + 1, :]) + mod_ref[shift_idx:shift_idx + 1, :]


ROW_PAD = SUBLANES


def _zero_row_pads(u_ref, rows):
    lead = (slice(None),) * (len(u_ref.shape) - 2)
    zeros = jnp.zeros(u_ref.shape[:-2] + (ROW_PAD, u_ref.shape[-1]), u_ref.dtype)
    u_ref[lead + (slice(0, ROW_PAD), slice(None))] = zeros
    u_ref[lead + (slice(ROW_PAD + rows, ROW_PAD + rows + ROW_PAD), slice(None))] = zeros


def _store_slabs(u_ref, first, u):
    for k in range(u.shape[1] // LANES):
        u_ref[first + k, ROW_PAD:ROW_PAD + u.shape[0], :] = u[:, k * LANES:(k + 1) * LANES]


def _seq_window(u_ref, off, rows, seq):
    w = u_ref[ROW_PAD + off:ROW_PAD + off + rows, :]
    if off == 0 or seq == rows:
        return w
    sub_row = lax.broadcasted_iota(jnp.int32, (SUBLANES, 1), 0)
    pieces = []
    for base in range(0, rows, seq):
        if off < 0:
            tile, bad = base, sub_row < -off
        else:
            tile, bad = base + seq - SUBLANES, sub_row >= SUBLANES - off
        pieces += [w[base:tile], jnp.where(bad, 0.0, w[tile:tile + SUBLANES]), w[tile + SUBLANES:base + seq]]
    return jnp.concatenate([p for p in pieces if p.shape[0]], axis=0)


def _adaln_body(cond_ref, w_ref, b_ref, o_ref):
    c = cond_ref[...]
    s = (c * jax.nn.sigmoid(c)).astype(MM_DTYPE)
    o_ref[...] = _dot(s, w_ref[...].astype(MM_DTYPE)) + b_ref[...]


def _adaln(cond, w_ada, b_ada, tn):
    n_layers, d, n_out = w_ada.shape
    return pl.pallas_call(
        _adaln_body,
        out_shape=jax.ShapeDtypeStruct((n_layers, MOD_ROWS, n_out), F32),
        grid=(n_layers, n_out // tn),
        in_specs=[
            _const_spec((MOD_ROWS, d)),
            pl.BlockSpec((None, d, tn), lambda l, n: (l, 0, n)),
            pl.BlockSpec((None, 1, tn), lambda l, n: (l, 0, n)),
        ],
        out_specs=pl.BlockSpec((None, MOD_ROWS, tn), lambda l, n: (l, 0, n)),
        compiler_params=_params(("arbitrary", "arbitrary")),
        name="adaln",
    )(cond, w_ada, b_ada.reshape(n_layers, 1, n_out))


def _nmm_body(x_ref, mod_ref, g_ref, w_ref, o_ref, h_ref, *, eps):
    @pl.when(pl.program_id(1) == 0)
    def _():
        h = _norm_mod(x_ref[...], g_ref[...], mod_ref, 0, 1, eps).astype(MM_DTYPE)
        h_ref[...] = h
        o_ref[...] = _dot(h, w_ref[...])

    @pl.when(pl.program_id(1) > 0)
    def _():
        o_ref[...] = _dot(h_ref[...], w_ref[...])


def _norm_mod_matmul(x, mod, mod_row, g, w, layer, wl, *, tm, tn, eps):
    n, d = x.shape
    m = w.shape[2]
    return pl.pallas_call(
        functools.partial(_nmm_body, eps=eps),
        out_shape=jax.ShapeDtypeStruct((n, m), F32),
        grid=(n // tm, m // tn),
        in_specs=[
            pl.BlockSpec((tm, d), lambda i, j: (i, 0)),
            _mod_spec(d, layer, mod_row, tm),
            _layer_spec(g, layer),
            pl.BlockSpec((None, d, tn), lambda i, j: (wl, 0, j)),
        ],
        out_specs=pl.BlockSpec((tm, tn), lambda i, j: (i, j)),
        scratch_shapes=[pltpu.VMEM((tm, d), MM_DTYPE)],
        compiler_params=_params(("arbitrary", "arbitrary")),
        name="norm_mod_matmul",
    )(x, mod, g, w)


def _rec_in_body(x_ref, mod_ref, g_ref, w_ref, cw_ref, cb_ref, o_ref, h_ref, u_ref, *, seq, n_y, eps):
    j = pl.program_id(1)
    tm = h_ref.shape[0]

    def gelu_tile(h):
        y = jax.nn.gelu(_dot(h, w_ref[...]))
        for c in range(o_ref.shape[0]):
            o_ref[c] = y[:, c * LANES:(c + 1) * LANES]

    @pl.when(j == 0)
    def _():
        h = _norm_mod(x_ref[...], g_ref[...], mod_ref, 0, 1, eps).astype(MM_DTYPE)
        h_ref[...] = h
        _zero_row_pads(u_ref, tm)
        gelu_tile(h)

    @pl.when((j > 0) & (j < n_y))
    def _():
        gelu_tile(h_ref[...])

    @pl.when(j >= n_y)
    def _():
        _store_slabs(u_ref, 0, _dot(h_ref[...], w_ref[...]))
        for c in range(u_ref.shape[0]):
            sl = slice(c * LANES, (c + 1) * LANES)
            xc = cb_ref[:, sl]
            for tap, off in enumerate((-2, -1, 0, 1)):
                xc = xc + _seq_window(u_ref.at[c], off, tm, seq) * cw_ref[tap:tap + 1, sl]
            o_ref[c] = xc


def _rec_in(x, mod, mod_row, g, w, w_conv, b_conv, layer, wl, *, seq, tm, tn, eps):
    n, d = x.shape
    m = w.shape[2]
    n_y = m // 2 // tn
    assert tm % seq == 0 and w_conv.shape[1] == 4 and m % (2 * tn) == 0
    conv_col = lambda i, j: (wl, 0, jnp.maximum(j - n_y, 0))
    return pl.pallas_call(
        functools.partial(_rec_in_body, seq=seq, n_y=n_y, eps=eps),
        out_shape=jax.ShapeDtypeStruct((m // LANES, n, LANES), F32),
        grid=(n // tm, m // tn),
        in_specs=[
            pl.BlockSpec((tm, d), lambda i, j: (i, 0)),
            _mod_spec(d, layer, mod_row, tm),
            _layer_spec(g, layer),
            pl.BlockSpec((None, d, tn), lambda i, j: (wl, 0, j)),
            pl.BlockSpec((None, 4, tn), conv_col),
            pl.BlockSpec((None, 1, tn), conv_col),
        ],
        out_specs=pl.BlockSpec((tn // LANES, tm, LANES), lambda i, j: (j, i, 0)),
        scratch_shapes=[pltpu.VMEM((tm, d), MM_DTYPE), pltpu.VMEM((tn // LANES, ROW_PAD + tm + ROW_PAD, LANES), F32)],
        compiler_params=_params(("arbitrary", "arbitrary")),
        name="rec_in",
    )(x, mod, g, w, w_conv, b_conv)


def _mmres_body(a_ref, w_ref, x_ref, mod_ref, o_ref):
    o_ref[...] = x_ref[...] + mod_ref[2:3, :] * _dot(a_ref[...], w_ref[...])


def _matmul_residual(a, w, wl, x, mod, layer, mod_row, *, tm):
    n, k = a.shape
    d = w.shape[2]
    return pl.pallas_call(
        _mmres_body,
        out_shape=jax.ShapeDtypeStruct((n, d), F32),
        grid=(n // tm,),
        in_specs=[
            pl.BlockSpec((tm, k), lambda i: (i, 0)),
            _layer_spec(w, wl),
            pl.BlockSpec((tm, d), lambda i: (i, 0)),
            _mod_spec(d, layer, mod_row, tm),
        ],
        out_specs=pl.BlockSpec((tm, d), lambda i: (i, 0)),
        compiler_params=_params(("arbitrary",)),
        name="matmul_residual",
    )(a, w, x, mod)


def _ffn_body(*refs, seq, eps, n_sub, final):
    if final:
        (x_ref, mod_ref, g_ref, wg_ref, wv_ref, cwg_ref, cwv_ref, cbg_ref, cbv_ref, wd_ref, gf_ref,
         o_ref, h_ref, act_ref, ug_ref, uv_ref) = refs
    else:
        (x_ref, mod_ref, g_ref, wg_ref, wv_ref, cwg_ref, cwv_ref, cbg_ref, cbv_ref, wd_ref,
         o_ref, h_ref, act_ref, ug_ref, uv_ref) = refs
    j = pl.program_id(1)
    n_tiles = pl.num_programs(1) - 1
    tm = h_ref.shape[0]
    ts = wg_ref.shape[1] // n_sub
    subs = [slice(s * ts, (s + 1) * ts) for s in range(n_sub)]

    def conv(u_ref, c, cw, cb):
        win = lambda off: _seq_window(u_ref.at[c], off, tm, seq)
        return cb + win(-1) * cw[0:1, :] + win(0) * cw[1:2, :] + win(1) * cw[2:3, :]

    def up():
        h = h_ref[...]
        for s, sl in enumerate(subs):
            _store_slabs(ug_ref, s * (ts // LANES), _dot(h, wg_ref[:, sl]))
            _store_slabs(uv_ref, s * (ts // LANES), _dot(h, wv_ref[:, sl]))

    def activate():
        for c in range(ug_ref.shape[0]):
            sl = slice(c * LANES, (c + 1) * LANES)
            hg = conv(ug_ref, c, 0.5 * cwg_ref[:, sl], 0.5 * cbg_ref[:, sl])
            val = conv(uv_ref, c, cwv_ref[:, sl], cbv_ref[:, sl])
            act_ref[:, sl] = (hg * (jnp.tanh(hg) + 1.0) * val).astype(MM_DTYPE)

    def down(act):
        o_ref[...] += _dot(act, wd_ref[...])

    @pl.when(j == 0)
    def _():
        h = _norm_mod(x_ref[...], g_ref[...], mod_ref, 3, 4, eps)
        h_ref[...] = h.astype(MM_DTYPE)
        o_ref[...] = jnp.zeros_like(o_ref)
        for u_ref in (ug_ref, uv_ref):
            _zero_row_pads(u_ref, tm)
        up()
        activate()

    @pl.when((j > 0) & (j < n_tiles))
    def _():
        down(act_ref[...])
        up()
        activate()

    @pl.when(j == n_tiles)
    def _():
        down(act_ref[...])
        y = x_ref[...] + mod_ref[5:6, :] * o_ref[...]
        if final:
            y = _rms(y, gf_ref[...], eps)
        o_ref[...] = y


def _conv_ffn(x, mod, layer, mod_row, g, w_up, w_conv, b_conv, w_down, g_final, *, seq, tm, tf, n_sub, x_buffers,
              eps):
    n, d = x.shape
    f = w_down.shape[1]
    assert w_conv.shape[1:] == (3, 2 * f) and tm % seq == 0 and f % tf == 0 and tf % (n_sub * LANES) == 0
    nf = f // tf
    final = g_final is not None
    up_tile = lambda j: jnp.minimum(j, nf - 1)
    lo = lambda rows: pl.BlockSpec((None, rows, tf), lambda i, j: (layer, 0, up_tile(j)))
    hi = lambda rows: pl.BlockSpec((None, rows, tf), lambda i, j: (layer, 0, up_tile(j) + nf))
    ins = [x, mod, g, w_up, w_up, w_conv, w_conv, b_conv, b_conv, w_down] + ([g_final] if final else [])
    in_specs = [
        pl.BlockSpec((tm, d), lambda i, j: (i, 0), pipeline_mode=pl.Buffered(x_buffers)),
        _mod_spec(d, layer, mod_row, tm),
        _layer_spec(g, layer),
        lo(d), hi(d), lo(3), hi(3), lo(1), hi(1),
        pl.BlockSpec((None, tf, d), lambda i, j: (layer, jnp.maximum(j - 1, 0), 0)),
    ] + ([_const_spec(g_final.shape)] if final else [])
    return pl.pallas_call(
        functools.partial(_ffn_body, seq=seq, eps=eps, n_sub=n_sub, final=final),
        out_shape=jax.ShapeDtypeStruct((n, d), F32),
        grid=(n // tm, nf + 1),
        in_specs=in_specs,
        out_specs=pl.BlockSpec((tm, d), lambda i, j: (i, 0)),
        scratch_shapes=[pltpu.VMEM((tm, d), MM_DTYPE), pltpu.VMEM((tm, tf), MM_DTYPE)]
        + [pltpu.VMEM((tf // LANES, ROW_PAD + tm + ROW_PAD, LANES), F32)] * 2,
        compiler_params=_params(("arbitrary", "arbitrary")),
        name="conv_ffn_final" if final else "conv_ffn",
    )(*ins)


def _attn_log2_scale(cfg):
    return float(cfg.nope + cfg.rope) ** -0.5 * math.log2(math.e)


def _mla_mid_body(*refs, cfg, rope):
    if rope:
        (proj_ref, gq_ref, gkv_ref, wqn_ref, wqr_ref, wqs_ref, wuk_ref, wuv_ref, cos_ref, sin_ref,
         qn_ref, qr_ref, kn_ref, v_ref, krp_ref) = refs
    else:
        (proj_ref, gq_ref, gkv_ref, wqn_ref, wqr_ref, wuk_ref, wuv_ref,
         qn_ref, qr_ref, kn_ref, v_ref, krp_ref, ckv_ref, kr_ref) = refs
    ql, kl = cfg.q_lora, cfg.kv_lora
    p = proj_ref[...]
    cqn = _rms(p[:, :ql], gq_ref[...], cfg.eps).astype(MM_DTYPE)
    ckv = _rms(p[:, ql:ql + kl], gkv_ref[...], cfg.eps)
    krp = p[:, ql + kl:ql + kl + LANES]
    qk_scale = _attn_log2_scale(cfg)
    qn_ref[...] = (_dot(cqn, wqn_ref[...]) * qk_scale).astype(MM_DTYPE)
    qr = _dot(cqn, wqr_ref[...])
    if rope:
        krs = p[:, ql + kl + LANES:ql + kl + 2 * LANES]
        qs = _dot(cqn, wqs_ref[...])
        cos = cos_ref[...]
        sin = sin_ref[...]
        for h in range(cfg.n_heads):
            sl = slice(h * LANES, (h + 1) * LANES)
            qr_ref[:, sl] = ((qr[:, sl] * cos + qs[:, sl] * sin) * qk_scale).astype(MM_DTYPE)
        krp_ref[...] = (krp * cos + krs * sin).astype(MM_DTYPE)
    else:
        qr_ref[...] = (qr * qk_scale).astype(MM_DTYPE)
        krp_ref[...] = krp.astype(MM_DTYPE)
        ckv_ref[...] = ckv
        kr_ref[...] = krp[:, :cfg.rope]
    ckv_b = ckv.astype(MM_DTYPE)
    kn_ref[...] = _dot(ckv_b, wuk_ref[...]).astype(MM_DTYPE)
    v_ref[...] = _dot(ckv_b, wuv_ref[...]).astype(MM_DTYPE)


def _mla_mid(proj, g_q, g_kv, w_qn, w_qr, w_qs, w_uk, w_uv, wl, cos, sin, *, cfg, seq, tm):
    n, pw = proj.shape
    rope = cos is not None
    hw = cfg.n_heads * LANES
    row = lambda i: (i, 0)
    ws = [g_q, g_kv, w_qn, w_qr] + ([w_qs] if rope else []) + [w_uk, w_uv]
    ins = [proj] + ws
    in_specs = [pl.BlockSpec((tm, pw), row)] + [_layer_spec(w, wl) for w in ws]
    out_shape = [jax.ShapeDtypeStruct((n, hw), MM_DTYPE)] * 4 + [jax.ShapeDtypeStruct((n, LANES), MM_DTYPE)]
    out_specs = [pl.BlockSpec((tm, hw), row)] * 4 + [pl.BlockSpec((tm, LANES), row)]
    if rope:
        bps = seq // tm
        ins += [cos, sin]
        in_specs += [pl.BlockSpec((tm, LANES), lambda i: (i % bps, 0))] * 2
    else:
        out_shape += [jax.ShapeDtypeStruct((n, cfg.kv_lora), F32), jax.ShapeDtypeStruct((n, cfg.rope), F32)]
        out_specs += [pl.BlockSpec((tm, cfg.kv_lora), row), pl.BlockSpec((tm, cfg.rope), row)]
    return pl.pallas_call(
        functools.partial(_mla_mid_body, cfg=cfg, rope=rope),
        out_shape=out_shape,
        grid=(n // tm,),
        in_specs=in_specs,
        out_specs=out_specs,
        compiler_params=_params(("arbitrary",)),
        name="mla_mid_rope" if rope else "mla_mid",
    )(*ins)


def _cache_expand_body(ckv_ref, kr_ref, wuk_ref, wuv_ref, kn_ref, v_ref, krp_ref, *, rope):
    c = ckv_ref[...].astype(MM_DTYPE)
    kn_ref[...] = _dot(c, wuk_ref[...]).astype(MM_DTYPE)
    v_ref[...] = _dot(c, wuv_ref[...]).astype(MM_DTYPE)
    krp_ref[:, :rope] = kr_ref[...].astype(MM_DTYPE)
    krp_ref[:, rope:] = jnp.zeros((krp_ref.shape[0], LANES - rope), MM_DTYPE)


def _cache_expand(cache_ckv, cache_krope, wl, w_uk, w_uv, *, cfg):
    b, _, p, kl = cache_ckv.shape
    hw = cfg.n_heads * LANES
    row = lambda i: (i, 0)
    return pl.pallas_call(
        functools.partial(_cache_expand_body, rope=cfg.rope),
        out_shape=[jax.ShapeDtypeStruct((b * p, hw), MM_DTYPE)] * 2 + [jax.ShapeDtypeStruct((b * p, LANES), MM_DTYPE)],
        grid=(b,),
        in_specs=[
            pl.BlockSpec((None, None, p, kl), lambda i: (i, wl, 0, 0)),
            pl.BlockSpec((None, None, p, cfg.rope), lambda i: (i, wl, 0, 0)),
            _layer_spec(w_uk, wl),
            _layer_spec(w_uv, wl),
        ],
        out_specs=[pl.BlockSpec((p, hw), row)] * 2 + [pl.BlockSpec((p, LANES), row)],
        compiler_params=_params(("arbitrary",)),
        name="cache_expand",
    )(cache_ckv, cache_krope, w_uk, w_uv)


def _attn_body(*refs, hb, cached, q_chunk):
    if cached:
        qn_ref, qr_ref, kn_ref, kr_ref, v_ref, knc_ref, krc_ref, vc_ref, o_ref = refs
    else:
        qn_ref, qr_ref, kn_ref, kr_ref, v_ref, o_ref = refs
    sq = qn_ref.shape[0]
    qc = min(sq, q_chunk)
    for h in range(hb):
        sl = slice(h * LANES, (h + 1) * LANES)
        k = jnp.concatenate([kn_ref[:, sl], kr_ref[...]], axis=1)
        if cached:
            kc = jnp.concatenate([knc_ref[:, sl], krc_ref[...]], axis=1)
        for r0 in range(0, sq, qc):
            rows = slice(r0, r0 + qc)
            q = jnp.concatenate([qn_ref[rows, sl], qr_ref[rows, sl]], axis=1)
            s = _dot_nt(q, k)
            m = jnp.max(s, axis=-1, keepdims=True)
            if cached:
                sc = _dot_nt(q, kc)
                m = jnp.maximum(m, jnp.max(sc, axis=-1, keepdims=True))
                pc = jnp.exp2(sc - m)
            p = jnp.exp2(s - m)
            den = jnp.sum(p, axis=-1, keepdims=True)
            o = _dot(p.astype(MM_DTYPE), v_ref[:, sl])
            if cached:
                den = den + jnp.sum(pc, axis=-1, keepdims=True)
                o = o + _dot(pc.astype(MM_DTYPE), vc_ref[:, sl])
            o_ref[rows, sl] = (o / den).astype(MM_DTYPE)


def _attention(qn, qr, kn, krp, v, cache, *, cfg, seq, hb):
    n, hw = qn.shape
    cached = cache is not None
    blk = lambda rows: pl.BlockSpec((rows, hb * LANES), lambda b, g: (b, g))
    shared = lambda rows: pl.BlockSpec((rows, LANES), lambda b, g: (b, 0))
    ins = [qn, qr, kn, krp, v]
    in_specs = [blk(seq), blk(seq), blk(seq), shared(seq), blk(seq)]
    if cached:
        knc, vc, krc = cache
        past = knc.shape[0] // (n // seq)
        ins += [knc, krc, vc]
        in_specs += [blk(past), shared(past), blk(past)]
    return pl.pallas_call(
        functools.partial(_attn_body, hb=hb, cached=cached, q_chunk=ATTN_Q_CHUNK),
        out_shape=jax.ShapeDtypeStruct((n, hw), MM_DTYPE),
        grid=(n // seq, cfg.n_heads // hb),
        in_specs=in_specs,
        out_specs=blk(seq),
        compiler_params=_params(("arbitrary", "arbitrary")),
        name="attention_cached" if cached else "attention",
    )(*ins)


def _rec_scan_body(*refs, seq, cfg, has_state):
    if has_state:
        (yg_ref, xc_ref, wg_ref, bg_ref, lam_ref, h0f_ref, h0b_ref,
         o_ref, af_ref, uf_ref, ab_ref, ub_ref, hf_ref, hb_ref) = refs
    else:
        (yg_ref, xc_ref, wg_ref, bg_ref, lam_ref,
         o_ref, sf_ref, sb_ref, af_ref, uf_ref, ab_ref, ub_ref, hf_ref, hb_ref) = refs
    nseq = SUBLANES
    pitch = seq + 4
    lam = lam_ref[...]
    softplus = jnp.maximum(-lam, 0.0) + jnp.log1p(jnp.exp(-jnp.abs(lam)))
    half_decay = (-0.5 * cfg.lru_c * math.log2(math.e)) * softplus
    wg = wg_ref[...]
    bg = bg_ref[...]
    for b in range(nseq):
        xc = xc_ref[b * seq:(b + 1) * seq, :]
        th = jnp.tanh(_dot(xc.astype(MM_DTYPE), wg) + bg)
        xh = 0.5 * xc
        for d, (a_ref, u_ref) in enumerate(((af_ref, uf_ref), (ab_ref, ub_ref))):
            tx = th[:, (2 * d) * LANES:(2 * d + 1) * LANES]
            ta = th[:, (2 * d + 1) * LANES:(2 * d + 2) * LANES]
            hd = half_decay[d:d + 1, :]
            a = jnp.exp2(ta * hd + hd)
            z = 1.0 - a * a
            u = jnp.where(z > 0.0, z * lax.rsqrt(z), 0.0) * ((tx + 1.0) * xh)
            a_ref[b * pitch:b * pitch + seq, :] = a
            u_ref[b * pitch:b * pitch + seq, :] = u

    if has_state:
        hf0 = h0f_ref[...]
        hb0 = h0b_ref[...]
    else:
        hf0 = jnp.zeros((nseq, LANES), F32)
        hb0 = hf0

    def step(t, carry):
        hf, hb = carry
        fwd = pl.ds(t, nseq, stride=pitch)
        hf = af_ref[fwd, :] * hf + uf_ref[fwd, :]
        hf_ref[fwd, :] = hf
        bwd = pl.ds(seq - 1 - t, nseq, stride=pitch)
        hb = ab_ref[bwd, :] * hb + ub_ref[bwd, :]
        hb_ref[bwd, :] = hb
        return hf, hb

    hf, hb = lax.fori_loop(0, seq, step, (hf0, hb0), unroll=8)
    if not has_state:
        sf_ref[...] = hf
        sb_ref[...] = hb
    for b in range(nseq):
        hsum = hf_ref[b * pitch:b * pitch + seq, :] + hb_ref[b * pitch:b * pitch + seq, :]
        o_ref[b * seq:(b + 1) * seq, :] = (hsum * yg_ref[b * seq:(b + 1) * seq, :]).astype(MM_DTYPE)


def _rec_scan(proj, w_gates, b_gates, lam, wl, state, *, cfg, seq):
    nb2, n, _ = proj.shape
    nb = nb2 // 2
    w = nb * LANES
    assert nb == cfg.lru_blocks
    rows = SUBLANES * seq
    has_state = state is not None
    col = lambda g, c: (g, c)
    ins = [proj, proj, w_gates, b_gates, lam]
    in_specs = [
        pl.BlockSpec((None, rows, LANES), lambda g, c: (c, g, 0)),
        pl.BlockSpec((None, rows, LANES), lambda g, c: (c + nb, g, 0)),
        pl.BlockSpec((None, None, LANES, 4 * LANES), lambda g, c: (wl, c, 0, 0)),
        pl.BlockSpec((None, None, 1, 4 * LANES), lambda g, c: (wl, c, 0, 0)),
        pl.BlockSpec((None, 2, LANES), lambda g, c: (wl, 0, c)),
    ]
    out_shape = [jax.ShapeDtypeStruct((n, w), MM_DTYPE)]
    out_specs = [pl.BlockSpec((rows, LANES), col)]
    if has_state:
        ins += list(state)
        in_specs += [pl.BlockSpec((SUBLANES, LANES), col)] * 2
    else:
        nstate = n // seq
        out_shape += [jax.ShapeDtypeStruct((nstate, w), F32)] * 2
        out_specs += [pl.BlockSpec((SUBLANES, LANES), col)] * 2
    scratch = [pltpu.VMEM((SUBLANES * (seq + 4), LANES), F32)] * 6
    return pl.pallas_call(
        functools.partial(_rec_scan_body, seq=seq, cfg=cfg, has_state=has_state),
        out_shape=out_shape,
        grid=(n // rows, nb),
        in_specs=in_specs,
        out_specs=out_specs,
        scratch_shapes=scratch,
        compiler_params=_params(("arbitrary", "arbitrary")),
        name="rec_scan_state" if has_state else "rec_scan",
    )(*ins)


def _swap_halves(w, rope):
    q = rope // 4
    return jnp.concatenate([w[..., q:2 * q], w[..., :q], w[..., 3 * q:], w[..., 2 * q:3 * q]], axis=-1)


def _pad_lanes(w):
    return jnp.pad(w, [(0, 0)] * (w.ndim - 1) + [(0, LANES - w.shape[-1])])


def _rope_tables(seq, cfg):
    rows = seq // cfg.grid_w
    row = jnp.repeat(jnp.arange(rows), cfg.grid_w).astype(F32)
    col = jnp.tile(jnp.arange(cfg.grid_w), rows).astype(F32)
    half = cfg.rope // 2
    inv = 1.0 / (cfg.rope_base ** (jnp.arange(0, half, 2, dtype=F32) / half))
    ar, ac = row[:, None] * inv, col[:, None] * inv
    cos = jnp.concatenate([jnp.cos(ar), jnp.cos(ar), jnp.cos(ac), jnp.cos(ac)], axis=-1)
    sin = jnp.concatenate([-jnp.sin(ar), jnp.sin(ar), -jnp.sin(ac), jnp.sin(ac)], axis=-1)
    return _pad_lanes(cos), _pad_lanes(sin)


def _pick(n, pref):
    return pref if n % pref == 0 else n


def _forward(cfg, x_prompt, x_sample, cache_ckv, cache_krope, state_lru, c, c_ctx,
             g_mix, g_ffn, g_final, w_ada, b_ada,
             w_mla_in, g_mla_q, g_mla_kv, w_mla_uq, w_mla_uk, w_mla_uv, w_mla_o,
             w_rec_in, w_rec_conv, b_rec_conv, w_rec_gx, b_rec_gx, w_rec_ga, b_rec_ga,
             rec_lambda, w_rec_out,
             w_ffn_up, w_ffn_conv, b_ffn_conv, w_ffn_down):
    depth, d = g_mix.shape
    bp, sp, _ = x_prompt.shape
    bs, ss, _ = x_sample.shape
    assert cfg.nope == LANES and cfg.v_dim == LANES and cfg.rope <= LANES and bs + 1 <= MOD_ROWS
    assert bp % SUBLANES == 0 and bs % SUBLANES == 0
    eps = cfg.eps
    cast = lambda w: w.astype(MM_DTYPE)
    row3 = lambda v: v[:, None, :]

    cond = jnp.zeros((MOD_ROWS, d), F32).at[0].set(c_ctx).at[1:1 + bs].set(c)
    mod = _adaln(cond, w_ada, b_ada, _pick(w_ada.shape[-1], ADALN_TILE)).reshape(depth, MOD_ROWS, N_MOD, d)

    ql, kl, r, nh = cfg.q_lora, cfg.kv_lora, cfg.rope, cfg.n_heads
    w_kr = w_mla_in[..., ql + kl:]
    w_in_p = cast(jnp.concatenate(
        [w_mla_in[..., :ql + kl], _pad_lanes(w_kr), _pad_lanes(_swap_halves(w_kr, r))], axis=-1))
    uq = w_mla_uq.reshape(w_mla_uq.shape[0], ql, nh, cfg.nope + r)
    flat = lambda w: cast(w.reshape(w.shape[0], ql, nh * LANES))
    w_qn, w_qr, w_qs = flat(uq[..., :cfg.nope]), flat(_pad_lanes(uq[..., cfg.nope:])), flat(
        _pad_lanes(_swap_halves(uq[..., cfg.nope:], r)))
    w_uk, w_uv, w_o = cast(w_mla_uk), cast(w_mla_uv), cast(w_mla_o)
    g_q, g_kv = row3(g_mla_q), row3(g_mla_kv)
    cos, sin = _rope_tables(ss, cfg)

    w_gates = cast(0.5 * jnp.concatenate([w_rec_gx[:, 0], w_rec_ga[:, 0], w_rec_gx[:, 1], w_rec_ga[:, 1]], axis=-1))
    nb = cfg.lru_blocks
    blk = lambda b: b.reshape(b.shape[0], nb, 1, LANES)
    b_gates = 0.5 * jnp.concatenate(
        [blk(b_rec_gx[:, 0]), blk(b_rec_ga[:, 0]), blk(b_rec_gx[:, 1]), blk(b_rec_ga[:, 1])], axis=-1)
    tf = _pick(w_ffn_down.shape[1], FF_TILE)
    w_rin, w_rout = cast(w_rec_in), cast(w_rec_out)
    w_up, w_down = cast(w_ffn_up), cast(w_ffn_down)
    g_mix3, g_ffn3, b_rconv, b_fconv = row3(g_mix), row3(g_ffn), row3(b_rec_conv), row3(b_ffn_conv)

    ctx_row = lambda i, tm: 0

    def run_group(x, seq, mod_row, sample):
        n = x.shape[0]
        rows = lambda pref: min(seq, pref) if sample else _pick(n, pref)
        tm_seq = max(seq, rows(TOKEN_BLOCK))
        ckv_new, kr_new, lru_new = [], [], []
        for layer in range(depth):
            j = layer // 2
            if layer % 2 == 0:
                proj = _norm_mod_matmul(x, mod, mod_row, g_mix3, w_in_p, layer, j, tm=rows(TOKEN_BLOCK),
                                        tn=w_in_p.shape[-1], eps=eps)
                if sample:
                    qn, qr, kn, v, krp = _mla_mid(proj, g_q, g_kv, w_qn, w_qr, w_qs, w_uk, w_uv, j, cos, sin,
                                                  cfg=cfg, seq=seq, tm=rows(MLA_MID_BLOCK))
                    cache = _cache_expand(cache_ckv, cache_krope, j, w_uk, w_uv, cfg=cfg)
                    att = _attention(qn, qr, kn, krp, v, cache, cfg=cfg, seq=seq, hb=min(ATTN_CACHED_HEADS, nh))
                else:
                    qn, qr, kn, v, krp, ckv, kr = _mla_mid(proj, g_q, g_kv, w_qn, w_qr, None, w_uk, w_uv, j,
                                                           None, None, cfg=cfg, seq=seq, tm=rows(MLA_MID_BLOCK))
                    ckv_new.append(ckv)
                    kr_new.append(kr)
                    att = _attention(qn, qr, kn, krp, v, None, cfg=cfg, seq=seq, hb=nh)
                x = _matmul_residual(att, w_o, j, x, mod, layer, mod_row, tm=rows(RESIDUAL_BLOCK))
            else:
                proj = _rec_in(x, mod, mod_row, g_mix3, w_rin, w_rec_conv, b_rconv, layer, j, seq=seq,
                               tm=tm_seq, tn=_pick(w_rin.shape[-1] // 2, IN_PROJ_TILE), eps=eps)
                outs = _rec_scan(proj, w_gates, b_gates, rec_lambda, j,
                                 (state_lru[:, j, 0], state_lru[:, j, 1]) if sample else None, cfg=cfg, seq=seq)
                if not sample:
                    lru_new.append(jnp.stack(outs[1:], axis=1))
                x = _matmul_residual(outs[0], w_rout, j, x, mod, layer, mod_row, tm=rows(RESIDUAL_BLOCK))
            x = _conv_ffn(x, mod, layer, mod_row, g_ffn3, w_up, w_ffn_conv, b_fconv, w_down,
                          g_final[None] if layer == depth - 1 else None, seq=seq, tm=tm_seq, tf=tf,
                          n_sub=2 if tf % (2 * LANES) == 0 else 1, x_buffers=2, eps=eps)
        return x, ckv_new, kr_new, lru_new

    y_p, ckv_new, kr_new, lru_new = run_group(x_prompt.reshape(bp * sp, d), sp, ctx_row, False)
    y_s, _, _, _ = run_group(x_sample.reshape(bs * ss, d), ss, lambda i, tm: 1 + (i * tm) // ss, True)
    new_ckv = jnp.stack([t.reshape(bp, sp, kl) for t in ckv_new], axis=1)
    new_kr = jnp.stack([t.reshape(bp, sp, r) for t in kr_new], axis=1)
    new_lru = jnp.stack(lru_new, axis=1)
    return y_p.reshape(bp, sp, d), y_s.reshape(bs, ss, d), new_ckv, new_kr, new_lru


CFG = Cfg(n_heads=16, q_lora=512, kv_lora=512, nope=128, rope=64, v_dim=128, grid_w=64, rope_base=10000.0,
          lru_blocks=16, lru_c=8.0, eps=1e-6)


def kernel(x_prompt, x_sample, cache_ckv, cache_krope, state_lru, c, c_ctx, g_mix, g_ffn, g_final, w_ada, b_ada, w_mla_in, g_mla_q, g_mla_kv, w_mla_uq, w_mla_uk, w_mla_uv, w_mla_o, w_rec_in, w_rec_conv, b_rec_conv, w_rec_gx, b_rec_gx, w_rec_ga, b_rec_ga, rec_lambda, w_rec_out, w_ffn_up, w_ffn_conv, b_ffn_conv, w_ffn_down):
    return _forward(CFG, x_prompt, x_sample, cache_ckv, cache_krope, state_lru, c, c_ctx, g_mix, g_ffn, g_final,
                    w_ada, b_ada, w_mla_in, g_mla_q, g_mla_kv, w_mla_uq, w_mla_uk, w_mla_uv, w_mla_o,
                    w_rec_in, w_rec_conv, b_rec_conv, w_rec_gx, b_rec_gx, w_rec_ga, b_rec_ga, rec_lambda, w_rec_out,
                    w_ffn_up, w_ffn_conv, b_ffn_conv, w_ffn_down)
```

```python
import functools
import math
from typing import NamedTuple

import jax
import jax.numpy as jnp
from jax import lax
from jax.experimental import pallas as pl
from jax.experimental.pallas import tpu as pltpu

F32 = jnp.float32
MM_DTYPE = jnp.bfloat16

LANES = 128
SUBLANES = 8
VMEM_LIMIT_BYTES = 60 * 1024 * 1024
N_MOD = 6
MOD_ROWS = 16

TOKEN_BLOCK = 1024
RESIDUAL_BLOCK = 512
MLA_MID_BLOCK = 512
FF_TILE = 512
IN_PROJ_TILE = 1024
ADALN_TILE = 1024
ATTN_Q_CHUNK = 512
ATTN_CACHED_HEADS = 4


class Cfg(NamedTuple):
    n_heads: int
    q_lora: int
    kv_lora: int
    nope: int
    rope: int
    v_dim: int
    grid_w: int
    rope_base: float
    lru_blocks: int
    lru_c: float
    eps: float


def _params(sem):
    return pltpu.CompilerParams(dimension_semantics=sem, vmem_limit_bytes=VMEM_LIMIT_BYTES)


def _dot(a, b):
    return jnp.dot(a, b, preferred_element_type=F32)


def _dot_nt(a, b):
    return lax.dot_general(a, b, (((1,), (1,)), ((), ())), preferred_element_type=F32)


def _const_spec(shape):
    nd = len(shape)
    return pl.BlockSpec(shape, lambda *_: (0,) * nd)


def _layer_spec(arr, layer):
    nd = arr.ndim
    return pl.BlockSpec((None,) + arr.shape[1:], lambda *_: (layer,) + (0,) * (nd - 1))


def _mod_spec(d, layer, mod_row, tm):
    return pl.BlockSpec((None, None, N_MOD, d), lambda i, *_: (layer, mod_row(i, tm), 0, 0))


def _rms(x, g, eps):
    return x * lax.rsqrt(jnp.mean(x * x, axis=-1, keepdims=True) + eps) * g


def _norm_mod(x, g, mod_ref, shift_idx, scale_idx, eps):
    y = _rms(x, g, eps)
    return y * (1.0 + mod_ref[scale_idx:scale_idx + 1, :]) + mod_ref[shift_idx:shift_idx + 1, :]


ROW_PAD = SUBLANES


def _zero_row_pads(u_ref, rows):
    lead = (slice(None),) * (len(u_ref.shape) - 2)
    zeros = jnp.zeros(u_ref.shape[:-2] + (ROW_PAD, u_ref.shape[-1]), u_ref.dtype)
    u_ref[lead + (slice(0, ROW_PAD), slice(None))] = zeros
    u_ref[lead + (slice(ROW_PAD + rows, ROW_PAD + rows + ROW_PAD), slice(None))] = zeros


def _store_slabs(u_ref, first, u):
    for k in range(u.shape[1] // LANES):
        u_ref[first + k, ROW_PAD:ROW_PAD + u.shape[0], :] = u[:, k * LANES:(k + 1) * LANES]


def _seq_window(u_ref, off, rows, seq):
    w = u_ref[ROW_PAD + off:ROW_PAD + off + rows, :]
    if off == 0 or seq == rows:
        return w
    sub_row = lax.broadcasted_iota(jnp.int32, (SUBLANES, 1), 0)
    pieces = []
    for base in range(0, rows, seq):
        if off < 0:
            tile, bad = base, sub_row < -off
        else:
            tile, bad = base + seq - SUBLANES, sub_row >= SUBLANES - off
        pieces += [w[base:tile], jnp.where(bad, 0.0, w[tile:tile + SUBLANES]), w[tile + SUBLANES:base + seq]]
    return jnp.concatenate([p for p in pieces if p.shape[0]], axis=0)


def _adaln_body(cond_ref, w_ref, b_ref, o_ref):
    c = cond_ref[...]
    s = (c * jax.nn.sigmoid(c)).astype(MM_DTYPE)
    o_ref[...] = _dot(s, w_ref[...].astype(MM_DTYPE)) + b_ref[...]


def _adaln(cond, w_ada, b_ada, tn):
    n_layers, d, n_out = w_ada.shape
    return pl.pallas_call(
        _adaln_body,
        out_shape=jax.ShapeDtypeStruct((n_layers, MOD_ROWS, n_out), F32),
        grid=(n_layers, n_out // tn),
        in_specs=[
            _const_spec((MOD_ROWS, d)),
            pl.BlockSpec((None, d, tn), lambda l, n: (l, 0, n)),
            pl.BlockSpec((None, 1, tn), lambda l, n: (l, 0, n)),
        ],
        out_specs=pl.BlockSpec((None, MOD_ROWS, tn), lambda l, n: (l, 0, n)),
        compiler_params=_params(("arbitrary", "arbitrary")),
        name="adaln",
    )(cond, w_ada, b_ada.reshape(n_layers, 1, n_out))


def _nmm_body(x_ref, mod_ref, g_ref, w_ref, o_ref, h_ref, *, eps):
    @pl.when(pl.program_id(1) == 0)
    def _():
        h = _norm_mod(x_ref[...], g_ref[...], mod_ref, 0, 1, eps).astype(MM_DTYPE)
        h_ref[...] = h
        o_ref[...] = _dot(h, w_ref[...])

    @pl.when(pl.program_id(1) > 0)
    def _():
        o_ref[...] = _dot(h_ref[...], w_ref[...])


def _norm_mod_matmul(x, mod, mod_row, g, w, layer, wl, *, tm, tn, eps):
    n, d = x.shape
    m = w.shape[2]
    return pl.pallas_call(
        functools.partial(_nmm_body, eps=eps),
        out_shape=jax.ShapeDtypeStruct((n, m), F32),
        grid=(n // tm, m // tn),
        in_specs=[
            pl.BlockSpec((tm, d), lambda i, j: (i, 0)),
            _mod_spec(d, layer, mod_row, tm),
            _layer_spec(g, layer),
            pl.BlockSpec((None, d, tn), lambda i, j: (wl, 0, j)),
        ],
        out_specs=pl.BlockSpec((tm, tn), lambda i, j: (i, j)),
        scratch_shapes=[pltpu.VMEM((tm, d), MM_DTYPE)],
        compiler_params=_params(("arbitrary", "arbitrary")),
        name="norm_mod_matmul",
    )(x, mod, g, w)


def _rec_in_body(x_ref, mod_ref, g_ref, w_ref, cw_ref, cb_ref, o_ref, h_ref, u_ref, *, seq, n_y, eps):
    j = pl.program_id(1)
    tm = h_ref.shape[0]

    def gelu_tile(h):
        y = jax.nn.gelu(_dot(h, w_ref[...]))
        for c in range(o_ref.shape[0]):
            o_ref[c] = y[:, c * LANES:(c + 1) * LANES]

    @pl.when(j == 0)
    def _():
        h = _norm_mod(x_ref[...], g_ref[...], mod_ref, 0, 1, eps).astype(MM_DTYPE)
        h_ref[...] = h
        _zero_row_pads(u_ref, tm)
        gelu_tile(h)

    @pl.when((j > 0) & (j < n_y))
    def _():
        gelu_tile(h_ref[...])

    @pl.when(j >= n_y)
    def _():
        _store_slabs(u_ref, 0, _dot(h_ref[...], w_ref[...]))
        for c in range(u_ref.shape[0]):
            sl = slice(c * LANES, (c + 1) * LANES)
            xc = cb_ref[:, sl]
            for tap, off in enumerate((-2, -1, 0, 1)):
                xc = xc + _seq_window(u_ref.at[c], off, tm, seq) * cw_ref[tap:tap + 1, sl]
            o_ref[c] = xc


def _rec_in(x, mod, mod_row, g, w, w_conv, b_conv, layer, wl, *, seq, tm, tn, eps):
    n, d = x.shape
    m = w.shape[2]
    n_y = m // 2 // tn
    assert tm % seq == 0 and w_conv.shape[1] == 4 and m % (2 * tn) == 0
    conv_col = lambda i, j: (wl, 0, jnp.maximum(j - n_y, 0))
    return pl.pallas_call(
        functools.partial(_rec_in_body, seq=seq, n_y=n_y, eps=eps),
        out_shape=jax.ShapeDtypeStruct((m // LANES, n, LANES), F32),
        grid=(n // tm, m // tn),
        in_specs=[
            pl.BlockSpec((tm, d), lambda i, j: (i, 0)),
            _mod_spec(d, layer, mod_row, tm),
            _layer_spec(g, layer),
            pl.BlockSpec((None, d, tn), lambda i, j: (wl, 0, j)),
            pl.BlockSpec((None, 4, tn), conv_col),
            pl.BlockSpec((None, 1, tn), conv_col),
        ],
        out_specs=pl.BlockSpec((tn // LANES, tm, LANES), lambda i, j: (j, i, 0)),
        scratch_shapes=[pltpu.VMEM((tm, d), MM_DTYPE), pltpu.VMEM((tn // LANES, ROW_PAD + tm + ROW_PAD, LANES), F32)],
        compiler_params=_params(("arbitrary", "arbitrary")),
        name="rec_in",
    )(x, mod, g, w, w_conv, b_conv)


def _mmres_body(a_ref, w_ref, x_ref, mod_ref, o_ref):
    o_ref[...] = x_ref[...] + mod_ref[2:3, :] * _dot(a_ref[...], w_ref[...])


def _matmul_residual(a, w, wl, x, mod, layer, mod_row, *, tm):
    n, k = a.shape
    d = w.shape[2]
    return pl.pallas_call(
        _mmres_body,
        out_shape=jax.ShapeDtypeStruct((n, d), F32),
        grid=(n // tm,),
        in_specs=[
            pl.BlockSpec((tm, k), lambda i: (i, 0)),
            _layer_spec(w, wl),
            pl.BlockSpec((tm, d), lambda i: (i, 0)),
            _mod_spec(d, layer, mod_row, tm),
        ],
        out_specs=pl.BlockSpec((tm, d), lambda i: (i, 0)),
        compiler_params=_params(("arbitrary",)),
        name="matmul_residual",
    )(a, w, x, mod)


def _ffn_body(*refs, seq, eps, n_sub, final):
    if final:
        (x_ref, mod_ref, g_ref, wg_ref, wv_ref, cwg_ref, cwv_ref, cbg_ref, cbv_ref, wd_ref, gf_ref,
         o_ref, h_ref, act_ref, ug_ref, uv_ref) = refs
    else:
        (x_ref, mod_ref, g_ref, wg_ref, wv_ref, cwg_ref, cwv_ref, cbg_ref, cbv_ref, wd_ref,
         o_ref, h_ref, act_ref, ug_ref, uv_ref) = refs
    j = pl.program_id(1)
    n_tiles = pl.num_programs(1) - 1
    tm = h_ref.shape[0]
    ts = wg_ref.shape[1] // n_sub
    subs = [slice(s * ts, (s + 1) * ts) for s in range(n_sub)]

    def conv(u_ref, c, cw, cb):
        win = lambda off: _seq_window(u_ref.at[c], off, tm, seq)
        return cb + win(-1) * cw[0:1, :] + win(0) * cw[1:2, :] + win(1) * cw[2:3, :]

    def up():
        h = h_ref[...]
        for s, sl in enumerate(subs):
            _store_slabs(ug_ref, s * (ts // LANES), _dot(h, wg_ref[:, sl]))
            _store_slabs(uv_ref, s * (ts // LANES), _dot(h, wv_ref[:, sl]))

    def activate():
        for c in range(ug_ref.shape[0]):
            sl = slice(c * LANES, (c + 1) * LANES)
            hg = conv(ug_ref, c, 0.5 * cwg_ref[:, sl], 0.5 * cbg_ref[:, sl])
            val = conv(uv_ref, c, cwv_ref[:, sl], cbv_ref[:, sl])
            act_ref[:, sl] = (hg * (jnp.tanh(hg) + 1.0) * val).astype(MM_DTYPE)

    def down(act):
        o_ref[...] += _dot(act, wd_ref[...])

    @pl.when(j == 0)
    def _():
        h = _norm_mod(x_ref[...], g_ref[...], mod_ref, 3, 4, eps)
        h_ref[...] = h.astype(MM_DTYPE)
        o_ref[...] = jnp.zeros_like(o_ref)
        for u_ref in (ug_ref, uv_ref):
            _zero_row_pads(u_ref, tm)
        up()
        activate()

    @pl.when((j > 0) & (j < n_tiles))
    def _():
        down(act_ref[...])
        up()
        activate()

    @pl.when(j == n_tiles)
    def _():
        down(act_ref[...])
        y = x_ref[...] + mod_ref[5:6, :] * o_ref[...]
        if final:
            y = _rms(y, gf_ref[...], eps)
        o_ref[...] = y


def _conv_ffn(x, mod, layer, mod_row, g, w_up, w_conv, b_conv, w_down, g_final, *, seq, tm, tf, n_sub, x_buffers,
              eps):
    n, d = x.shape
    f = w_down.shape[1]
    assert w_conv.shape[1:] == (3, 2 * f) and tm % seq == 0 and f % tf == 0 and tf % (n_sub * LANES) == 0
    nf = f // tf
    final = g_final is not None
    up_tile = lambda j: jnp.minimum(j, nf - 1)
    lo = lambda rows: pl.BlockSpec((None, rows, tf), lambda i, j: (layer, 0, up_tile(j)))
    hi = lambda rows: pl.BlockSpec((None, rows, tf), lambda i, j: (layer, 0, up_tile(j) + nf))
    ins = [x, mod, g, w_up, w_up, w_conv, w_conv, b_conv, b_conv, w_down] + ([g_final] if final else [])
    in_specs = [
        pl.BlockSpec((tm, d), lambda i, j: (i, 0), pipeline_mode=pl.Buffered(x_buffers)),
        _mod_spec(d, layer, mod_row, tm),
        _layer_spec(g, layer),
        lo(d), hi(d), lo(3), hi(3), lo(1), hi(1),
        pl.BlockSpec((None, tf, d), lambda i, j: (layer, jnp.maximum(j - 1, 0), 0)),
    ] + ([_const_spec(g_final.shape)] if final else [])
    return pl.pallas_call(
        functools.partial(_ffn_body, seq=seq, eps=eps, n_sub=n_sub, final=final),
        out_shape=jax.ShapeDtypeStruct((n, d), F32),
        grid=(n // tm, nf + 1),
        in_specs=in_specs,
        out_specs=pl.BlockSpec((tm, d), lambda i, j: (i, 0)),
        scratch_shapes=[pltpu.VMEM((tm, d), MM_DTYPE), pltpu.VMEM((tm, tf), MM_DTYPE)]
        + [pltpu.VMEM((tf // LANES, ROW_PAD + tm + ROW_PAD, LANES), F32)] * 2,
        compiler_params=_params(("arbitrary", "arbitrary")),
        name="conv_ffn_final" if final else "conv_ffn",
    )(*ins)


def _attn_log2_scale(cfg):
    return float(cfg.nope + cfg.rope) ** -0.5 * math.log2(math.e)


def _mla_mid_body(*refs, cfg, rope):
    if rope:
        (proj_ref, gq_ref, gkv_ref, wqn_ref, wqr_ref, wqs_ref, wuk_ref, wuv_ref, cos_ref, sin_ref,
         qn_ref, qr_ref, kn_ref, v_ref, krp_ref) = refs
    else:
        (proj_ref, gq_ref, gkv_ref, wqn_ref, wqr_ref, wuk_ref, wuv_ref,
         qn_ref, qr_ref, kn_ref, v_ref, krp_ref, ckv_ref, kr_ref) = refs
    ql, kl = cfg.q_lora, cfg.kv_lora
    p = proj_ref[...]
    cqn = _rms(p[:, :ql], gq_ref[...], cfg.eps).astype(MM_DTYPE)
    ckv = _rms(p[:, ql:ql + kl], gkv_ref[...], cfg.eps)
    krp = p[:, ql + kl:ql + kl + LANES]
    qk_scale = _attn_log2_scale(cfg)
    qn_ref[...] = (_dot(cqn, wqn_ref[...]) * qk_scale).astype(MM_DTYPE)
    qr = _dot(cqn, wqr_ref[...])
    if rope:
        krs = p[:, ql + kl + LANES:ql + kl + 2 * LANES]
        qs = _dot(cqn, wqs_ref[...])
        cos = cos_ref[...]
        sin = sin_ref[...]
        for h in range(cfg.n_heads):
            sl = slice(h * LANES, (h + 1) * LANES)
            qr_ref[:, sl] = ((qr[:, sl] * cos + qs[:, sl] * sin) * qk_scale).astype(MM_DTYPE)
        krp_ref[...] = (krp * cos + krs * sin).astype(MM_DTYPE)
    else:
        qr_ref[...] = (qr * qk_scale).astype(MM_DTYPE)
        krp_ref[...] = krp.astype(MM_DTYPE)
        ckv_ref[...] = ckv
        kr_ref[...] = krp[:, :cfg.rope]
    ckv_b = ckv.astype(MM_DTYPE)
    kn_ref[...] = _dot(ckv_b, wuk_ref[...]).astype(MM_DTYPE)
    v_ref[...] = _dot(ckv_b, wuv_ref[...]).astype(MM_DTYPE)


def _mla_mid(proj, g_q, g_kv, w_qn, w_qr, w_qs, w_uk, w_uv, wl, cos, sin, *, cfg, seq, tm):
    n, pw = proj.shape
    rope = cos is not None
    hw = cfg.n_heads * LANES
    row = lambda i: (i, 0)
    ws = [g_q, g_kv, w_qn, w_qr] + ([w_qs] if rope else []) + [w_uk, w_uv]
    ins = [proj] + ws
    in_specs = [pl.BlockSpec((tm, pw), row)] + [_layer_spec(w, wl) for w in ws]
    out_shape = [jax.ShapeDtypeStruct((n, hw), MM_DTYPE)] * 4 + [jax.ShapeDtypeStruct((n, LANES), MM_DTYPE)]
    out_specs = [pl.BlockSpec((tm, hw), row)] * 4 + [pl.BlockSpec((tm, LANES), row)]
    if rope:
        bps = seq // tm
        ins += [cos, sin]
        in_specs += [pl.BlockSpec((tm, LANES), lambda i: (i % bps, 0))] * 2
    else:
        out_shape += [jax.ShapeDtypeStruct((n, cfg.kv_lora), F32), jax.ShapeDtypeStruct((n, cfg.rope), F32)]
        out_specs += [pl.BlockSpec((tm, cfg.kv_lora), row), pl.BlockSpec((tm, cfg.rope), row)]
    return pl.pallas_call(
        functools.partial(_mla_mid_body, cfg=cfg, rope=rope),
        out_shape=out_shape,
        grid=(n // tm,),
        in_specs=in_specs,
        out_specs=out_specs,
        compiler_params=_params(("arbitrary",)),
        name="mla_mid_rope" if rope else "mla_mid",
    )(*ins)


def _cache_expand_body(ckv_ref, kr_ref, wuk_ref, wuv_ref, kn_ref, v_ref, krp_ref, *, rope):
    c = ckv_ref[...].astype(MM_DTYPE)
    kn_ref[...] = _dot(c, wuk_ref[...]).astype(MM_DTYPE)
    v_ref[...] = _dot(c, wuv_ref[...]).astype(MM_DTYPE)
    krp_ref[:, :rope] = kr_ref[...].astype(MM_DTYPE)
    krp_ref[:, rope:] = jnp.zeros((krp_ref.shape[0], LANES - rope), MM_DTYPE)


def _cache_expand(cache_ckv, cache_krope, wl, w_uk, w_uv, *, cfg):
    b, _, p, kl = cache_ckv.shape
    hw = cfg.n_heads * LANES
    row = lambda i: (i, 0)
    return pl.pallas_call(
        functools.partial(_cache_expand_body, rope=cfg.rope),
        out_shape=[jax.ShapeDtypeStruct((b * p, hw), MM_DTYPE)] * 2 + [jax.ShapeDtypeStruct((b * p, LANES), MM_DTYPE)],
        grid=(b,),
        in_specs=[
            pl.BlockSpec((None, None, p, kl), lambda i: (i, wl, 0, 0)),
            pl.BlockSpec((None, None, p, cfg.rope), lambda i: (i, wl, 0, 0)),
            _layer_spec(w_uk, wl),
            _layer_spec(w_uv, wl),
        ],
        out_specs=[pl.BlockSpec((p, hw), row)] * 2 + [pl.BlockSpec((p, LANES), row)],
        compiler_params=_params(("arbitrary",)),
        name="cache_expand",
    )(cache_ckv, cache_krope, w_uk, w_uv)


def _attn_body(*refs, hb, cached, q_chunk):
    if cached:
        qn_ref, qr_ref, kn_ref, kr_ref, v_ref, knc_ref, krc_ref, vc_ref, o_ref = refs
    else:
        qn_ref, qr_ref, kn_ref, kr_ref, v_ref, o_ref = refs
    sq = qn_ref.shape[0]
    qc = min(sq, q_chunk)
    for h in range(hb):
        sl = slice(h * LANES, (h + 1) * LANES)
        k = jnp.concatenate([kn_ref[:, sl], kr_ref[...]], axis=1)
        if cached:
            kc = jnp.concatenate([knc_ref[:, sl], krc_ref[...]], axis=1)
        for r0 in range(0, sq, qc):
            rows = slice(r0, r0 + qc)
            q = jnp.concatenate([qn_ref[rows, sl], qr_ref[rows, sl]], axis=1)
            s = _dot_nt(q, k)
            m = jnp.max(s, axis=-1, keepdims=True)
            if cached:
                sc = _dot_nt(q, kc)
                m = jnp.maximum(m, jnp.max(sc, axis=-1, keepdims=True))
                pc = jnp.exp2(sc - m)
            p = jnp.exp2(s - m)
            den = jnp.sum(p, axis=-1, keepdims=True)
            o = _dot(p.astype(MM_DTYPE), v_ref[:, sl])
            if cached:
                den = den + jnp.sum(pc, axis=-1, keepdims=True)
                o = o + _dot(pc.astype(MM_DTYPE), vc_ref[:, sl])
            o_ref[rows, sl] = (o / den).astype(MM_DTYPE)


def _attention(qn, qr, kn, krp, v, cache, *, cfg, seq, hb):
    n, hw = qn.shape
    cached = cache is not None
    blk = lambda rows: pl.BlockSpec((rows, hb * LANES), lambda b, g: (b, g))
    shared = lambda rows: pl.BlockSpec((rows, LANES), lambda b, g: (b, 0))
    ins = [qn, qr, kn, krp, v]
    in_specs = [blk(seq), blk(seq), blk(seq), shared(seq), blk(seq)]
    if cached:
        knc, vc, krc = cache
        past = knc.shape[0] // (n // seq)
        ins += [knc, krc, vc]
        in_specs += [blk(past), shared(past), blk(past)]
    return pl.pallas_call(
        functools.partial(_attn_body, hb=hb, cached=cached, q_chunk=ATTN_Q_CHUNK),
        out_shape=jax.ShapeDtypeStruct((n, hw), MM_DTYPE),
        grid=(n // seq, cfg.n_heads // hb),
        in_specs=in_specs,
        out_specs=blk(seq),
        compiler_params=_params(("arbitrary", "arbitrary")),
        name="attention_cached" if cached else "attention",
    )(*ins)


SCAN_PITCH_PAD = 4
SCAN_SCRATCH_BYTES = 16 * 1024 * 1024


def _scan_lane_tiles(seq, n_blocks):
    two = 2 * 6 * SUBLANES * (seq + SCAN_PITCH_PAD) * LANES * 4
    return 2 if n_blocks % 2 == 0 and two <= SCAN_SCRATCH_BYTES else 1


def _rec_scan_body(*refs, seq, cfg, has_state, tiles):
    n_in = 7 if has_state else 5
    yg_ref, xc_ref, wg_ref, bg_ref, lam_ref = refs[:5]
    o_ref = refs[n_in]
    scratch = refs[-6 * tiles:]
    nseq = SUBLANES
    pitch = seq + SCAN_PITCH_PAD
    lam = lam_ref[...]
    softplus = jnp.maximum(-lam, 0.0) + jnp.log1p(jnp.exp(-jnp.abs(lam)))
    half_decay = (-0.5 * cfg.lru_c * math.log2(math.e)) * softplus
    for k in range(tiles):
        af_ref, uf_ref, ab_ref, ub_ref = scratch[6 * k:6 * k + 4]
        lanes = slice(k * LANES, (k + 1) * LANES)
        wg = wg_ref[k]
        bg = bg_ref[k]
        for b in range(nseq):
            xc = xc_ref[k, b * seq:(b + 1) * seq, :]
            th = jnp.tanh(_dot(xc.astype(MM_DTYPE), wg) + bg)
            xh = 0.5 * xc
            for d, (a_ref, u_ref) in enumerate(((af_ref, uf_ref), (ab_ref, ub_ref))):
                tx = th[:, (2 * d) * LANES:(2 * d + 1) * LANES]
                ta = th[:, (2 * d + 1) * LANES:(2 * d + 2) * LANES]
                hd = half_decay[d:d + 1, lanes]
                a = jnp.exp2(ta * hd + hd)
                z = 1.0 - a * a
                u = jnp.where(z > 0.0, z * lax.rsqrt(z), 0.0) * ((tx + 1.0) * xh)
                a_ref[b * pitch:b * pitch + seq, :] = a
                u_ref[b * pitch:b * pitch + seq, :] = u

    if has_state:
        h0 = [refs[5 + d][:, k * LANES:(k + 1) * LANES] for k in range(tiles) for d in (0, 1)]
    else:
        h0 = [jnp.zeros((nseq, LANES), F32)] * (2 * tiles)

    def step(t, carry):
        fwd = pl.ds(t, nseq, stride=pitch)
        bwd = pl.ds(seq - 1 - t, nseq, stride=pitch)
        new = []
        for k in range(tiles):
            af_ref, uf_ref, ab_ref, ub_ref, hf_ref, hb_ref = scratch[6 * k:6 * k + 6]
            hf = af_ref[fwd, :] * carry[2 * k] + uf_ref[fwd, :]
            hf_ref[fwd, :] = hf
            hb = ab_ref[bwd, :] * carry[2 * k + 1] + ub_ref[bwd, :]
            hb_ref[bwd, :] = hb
            new += [hf, hb]
        return tuple(new)

    last = lax.fori_loop(0, seq, step, tuple(h0), unroll=8)
    for k in range(tiles):
        lanes = slice(k * LANES, (k + 1) * LANES)
        hf_ref, hb_ref = scratch[6 * k + 4:6 * k + 6]
        if not has_state:
            refs[n_in + 1][:, lanes] = last[2 * k]
            refs[n_in + 2][:, lanes] = last[2 * k + 1]
        for b in range(nseq):
            hsum = hf_ref[b * pitch:b * pitch + seq, :] + hb_ref[b * pitch:b * pitch + seq, :]
            o_ref[b * seq:(b + 1) * seq, lanes] = (hsum * yg_ref[k, b * seq:(b + 1) * seq, :]).astype(MM_DTYPE)


def _rec_scan(proj, w_gates, b_gates, lam, wl, state, *, cfg, seq, tiles):
    nb2, n, _ = proj.shape
    nb = nb2 // 2
    w = nb * LANES
    assert nb == cfg.lru_blocks and nb % tiles == 0
    rows = SUBLANES * seq
    width = tiles * LANES
    has_state = state is not None
    col = lambda g, c: (g, c)
    ins = [proj, proj, w_gates, b_gates, lam]
    in_specs = [
        pl.BlockSpec((tiles, rows, LANES), lambda g, c: (c, g, 0)),
        pl.BlockSpec((tiles, rows, LANES), lambda g, c: (c + nb // tiles, g, 0)),
        pl.BlockSpec((None, tiles, LANES, 4 * LANES), lambda g, c: (wl, c, 0, 0)),
        pl.BlockSpec((None, tiles, 1, 4 * LANES), lambda g, c: (wl, c, 0, 0)),
        pl.BlockSpec((None, 2, width), lambda g, c: (wl, 0, c)),
    ]
    out_shape = [jax.ShapeDtypeStruct((n, w), MM_DTYPE)]
    out_specs = [pl.BlockSpec((rows, width), col)]
    if has_state:
        ins += list(state)
        in_specs += [pl.BlockSpec((SUBLANES, width), col)] * 2
    else:
        nstate = n // seq
        out_shape += [jax.ShapeDtypeStruct((nstate, w), F32)] * 2
        out_specs += [pl.BlockSpec((SUBLANES, width), col)] * 2
    scratch = [pltpu.VMEM((SUBLANES * (seq + SCAN_PITCH_PAD), LANES), F32)] * (6 * tiles)
    return pl.pallas_call(
        functools.partial(_rec_scan_body, seq=seq, cfg=cfg, has_state=has_state, tiles=tiles),
        out_shape=out_shape,
        grid=(n // rows, nb // tiles),
        in_specs=in_specs,
        out_specs=out_specs,
        scratch_shapes=scratch,
        compiler_params=_params(("arbitrary", "arbitrary")),
        name="rec_scan_state" if has_state else "rec_scan",
    )(*ins)


def _swap_halves(w, rope):
    q = rope // 4
    return jnp.concatenate([w[..., q:2 * q], w[..., :q], w[..., 3 * q:], w[..., 2 * q:3 * q]], axis=-1)


def _pad_lanes(w):
    return jnp.pad(w, [(0, 0)] * (w.ndim - 1) + [(0, LANES - w.shape[-1])])


def _rope_tables(seq, cfg):
    rows = seq // cfg.grid_w
    row = jnp.repeat(jnp.arange(rows), cfg.grid_w).astype(F32)
    col = jnp.tile(jnp.arange(cfg.grid_w), rows).astype(F32)
    half = cfg.rope // 2
    inv = 1.0 / (cfg.rope_base ** (jnp.arange(0, half, 2, dtype=F32) / half))
    ar, ac = row[:, None] * inv, col[:, None] * inv
    cos = jnp.concatenate([jnp.cos(ar), jnp.cos(ar), jnp.cos(ac), jnp.cos(ac)], axis=-1)
    sin = jnp.concatenate([-jnp.sin(ar), jnp.sin(ar), -jnp.sin(ac), jnp.sin(ac)], axis=-1)
    return _pad_lanes(cos), _pad_lanes(sin)


def _pick(n, pref):
    return pref if n % pref == 0 else n


def _forward(cfg, x_prompt, x_sample, cache_ckv, cache_krope, state_lru, c, c_ctx,
             g_mix, g_ffn, g_final, w_ada, b_ada,
             w_mla_in, g_mla_q, g_mla_kv, w_mla_uq, w_mla_uk, w_mla_uv, w_mla_o,
             w_rec_in, w_rec_conv, b_rec_conv, w_rec_gx, b_rec_gx, w_rec_ga, b_rec_ga,
             rec_lambda, w_rec_out,
             w_ffn_up, w_ffn_conv, b_ffn_conv, w_ffn_down):
    depth, d = g_mix.shape
    bp, sp, _ = x_prompt.shape
    bs, ss, _ = x_sample.shape
    assert cfg.nope == LANES and cfg.v_dim == LANES and cfg.rope <= LANES and bs + 1 <= MOD_ROWS
    assert bp % SUBLANES == 0 and bs % SUBLANES == 0
    eps = cfg.eps
    cast = lambda w: w.astype(MM_DTYPE)
    row3 = lambda v: v[:, None, :]

    cond = jnp.zeros((MOD_ROWS, d), F32).at[0].set(c_ctx).at[1:1 + bs].set(c)
    mod = _adaln(cond, w_ada, b_ada, _pick(w_ada.shape[-1], ADALN_TILE)).reshape(depth, MOD_ROWS, N_MOD, d)

    ql, kl, r, nh = cfg.q_lora, cfg.kv_lora, cfg.rope, cfg.n_heads
    w_kr = w_mla_in[..., ql + kl:]
    w_in_p = cast(jnp.concatenate(
        [w_mla_in[..., :ql + kl], _pad_lanes(w_kr), _pad_lanes(_swap_halves(w_kr, r))], axis=-1))
    uq = w_mla_uq.reshape(w_mla_uq.shape[0], ql, nh, cfg.nope + r)
    flat = lambda w: cast(w.reshape(w.shape[0], ql, nh * LANES))
    w_qn, w_qr, w_qs = flat(uq[..., :cfg.nope]), flat(_pad_lanes(uq[..., cfg.nope:])), flat(
        _pad_lanes(_swap_halves(uq[..., cfg.nope:], r)))
    w_uk, w_uv, w_o = cast(w_mla_uk), cast(w_mla_uv), cast(w_mla_o)
    g_q, g_kv = row3(g_mla_q), row3(g_mla_kv)
    cos, sin = _rope_tables(ss, cfg)

    w_gates = cast(0.5 * jnp.concatenate([w_rec_gx[:, 0], w_rec_ga[:, 0], w_rec_gx[:, 1], w_rec_ga[:, 1]], axis=-1))
    nb = cfg.lru_blocks
    blk = lambda b: b.reshape(b.shape[0], nb, 1, LANES)
    b_gates = 0.5 * jnp.concatenate(
        [blk(b_rec_gx[:, 0]), blk(b_rec_ga[:, 0]), blk(b_rec_gx[:, 1]), blk(b_rec_ga[:, 1])], axis=-1)
    tf = _pick(w_ffn_down.shape[1], FF_TILE)
    w_rin, w_rout = cast(w_rec_in), cast(w_rec_out)
    w_up, w_down = cast(w_ffn_up), cast(w_ffn_down)
    g_mix3, g_ffn3, b_rconv, b_fconv = row3(g_mix), row3(g_ffn), row3(b_rec_conv), row3(b_ffn_conv)

    ctx_row = lambda i, tm: 0

    def run_group(x, seq, mod_row, sample):
        n = x.shape[0]
        rows = lambda pref: min(seq, pref) if sample else _pick(n, pref)
        tm_seq = max(seq, rows(TOKEN_BLOCK))
        ckv_new, kr_new, lru_new = [], [], []
        for layer in range(depth):
            j = layer // 2
            if layer % 2 == 0:
                proj = _norm_mod_matmul(x, mod, mod_row, g_mix3, w_in_p, layer, j, tm=rows(TOKEN_BLOCK),
                                        tn=w_in_p.shape[-1], eps=eps)
                if sample:
                    qn, qr, kn, v, krp = _mla_mid(proj, g_q, g_kv, w_qn, w_qr, w_qs, w_uk, w_uv, j, cos, sin,
                                                  cfg=cfg, seq=seq, tm=rows(MLA_MID_BLOCK))
                    cache = _cache_expand(cache_ckv, cache_krope, j, w_uk, w_uv, cfg=cfg)
                    att = _attention(qn, qr, kn, krp, v, cache, cfg=cfg, seq=seq, hb=min(ATTN_CACHED_HEADS, nh))
                else:
                    qn, qr, kn, v, krp, ckv, kr = _mla_mid(proj, g_q, g_kv, w_qn, w_qr, None, w_uk, w_uv, j,
                                                           None, None, cfg=cfg, seq=seq, tm=rows(MLA_MID_BLOCK))
                    ckv_new.append(ckv)
                    kr_new.append(kr)
                    att = _attention(qn, qr, kn, krp, v, None, cfg=cfg, seq=seq, hb=nh)
                x = _matmul_residual(att, w_o, j, x, mod, layer, mod_row, tm=rows(RESIDUAL_BLOCK))
            else:
                proj = _rec_in(x, mod, mod_row, g_mix3, w_rin, w_rec_conv, b_rconv, layer, j, seq=seq,
                               tm=tm_seq, tn=_pick(w_rin.shape[-1] // 2, IN_PROJ_TILE), eps=eps)
                outs = _rec_scan(proj, w_gates, b_gates, rec_lambda, j,
                                 (state_lru[:, j, 0], state_lru[:, j, 1]) if sample else None, cfg=cfg, seq=seq,
                                 tiles=_scan_lane_tiles(seq, nb))
                if not sample:
                    lru_new.append(jnp.stack(outs[1:], axis=1))
                x = _matmul_residual(outs[0], w_rout, j, x, mod, layer, mod_row, tm=rows(RESIDUAL_BLOCK))
            x = _conv_ffn(x, mod, layer, mod_row, g_ffn3, w_up, w_ffn_conv, b_fconv, w_down,
                          g_final[None] if layer == depth - 1 else None, seq=seq, tm=tm_seq, tf=tf,
                          n_sub=2 if tf % (2 * LANES) == 0 else 1, x_buffers=2, eps=eps)
        return x, ckv_new, kr_new, lru_new

    y_p, ckv_new, kr_new, lru_new = run_group(x_prompt.reshape(bp * sp, d), sp, ctx_row, False)
    y_s, _, _, _ = run_group(x_sample.reshape(bs * ss, d), ss, lambda i, tm: 1 + (i * tm) // ss, True)
    new_ckv = jnp.stack([t.reshape(bp, sp, kl) for t in ckv_new], axis=1)
    new_kr = jnp.stack([t.reshape(bp, sp, r) for t in kr_new], axis=1)
    new_lru = jnp.stack(lru_new, axis=1)
    return y_p.reshape(bp, sp, d), y_s.reshape(bs, ss, d), new_ckv, new_kr, new_lru


CFG = Cfg(n_heads=16, q_lora=512, kv_lora=512, nope=128, rope=64, v_dim=128, grid_w=64, rope_base=10000.0,
          lru_blocks=16, lru_c=8.0, eps=1e-6)


def kernel(x_prompt, x_sample, cache_ckv, cache_krope, state_lru, c, c_ctx, g_mix, g_ffn, g_final, w_ada, b_ada, w_mla_in, g_mla_q, g_mla_kv, w_mla_uq, w_mla_uk, w_mla_uv, w_mla_o, w_rec_in, w_rec_conv, b_rec_conv, w_rec_gx, b_rec_gx, w_rec_ga, b_rec_ga, rec_lambda, w_rec_out, w_ffn_up, w_ffn_conv, b_ffn_conv, w_ffn_down):
    return _forward(CFG, x_prompt, x_sample, cache_ckv, cache_krope, state_lru, c, c_ctx, g_mix, g_ffn, g_final,
                    w_ada, b_ada, w_mla_in, g_mla_q, g_mla_kv, w_mla_uq, w_mla_uk, w_mla_uv, w_mla_o,
                    w_rec_in, w_rec_conv, b_rec_conv, w_rec_gx, b_rec_gx, w_rec_ga, b_rec_ga, rec_lambda, w_rec_out,
                    w_ffn_up, w_ffn_conv, b_ffn_conv, w_ffn_down)
```

```python
import functools
import math
from typing import NamedTuple

import jax
import jax.numpy as jnp
from jax import lax
from jax.experimental import pallas as pl
from jax.experimental.pallas import tpu as pltpu

F32 = jnp.float32
MM_DTYPE = jnp.bfloat16

LANES = 128
SUBLANES = 8
VMEM_LIMIT_BYTES = 60 * 1024 * 1024
N_MOD = 6
MOD_ROWS = 16

TOKEN_BLOCK = 1024
RESIDUAL_BLOCK = 512
MLA_MID_BLOCK = 512
FF_TILE = 512
IN_PROJ_TILE = 1024
ADALN_TILE = 1024
ATTN_Q_CHUNK = 512
ATTN_CACHED_HEADS = 4


class Cfg(NamedTuple):
    n_heads: int
    q_lora: int
    kv_lora: int
    nope: int
    rope: int
    v_dim: int
    grid_w: int
    rope_base: float
    lru_blocks: int
    lru_c: float
    eps: float


def _params(sem):
    return pltpu.CompilerParams(dimension_semantics=sem, vmem_limit_bytes=VMEM_LIMIT_BYTES)


def _dot(a, b):
    return jnp.dot(a, b, preferred_element_type=F32)


def _dot_nt(a, b):
    return lax.dot_general(a, b, (((1,), (1,)), ((), ())), preferred_element_type=F32)


def _const_spec(shape):
    nd = len(shape)
    return pl.BlockSpec(shape, lambda *_: (0,) * nd)


def _layer_spec(arr, layer):
    nd = arr.ndim
    return pl.BlockSpec((None,) + arr.shape[1:], lambda *_: (layer,) + (0,) * (nd - 1))


def _mod_spec(d, layer, mod_row, tm):
    return pl.BlockSpec((None, None, N_MOD, d), lambda i, *_: (layer, mod_row(i, tm), 0, 0))


def _rms(x, g, eps):
    return x * lax.rsqrt(jnp.mean(x * x, axis=-1, keepdims=True) + eps) * g


def _norm_mod(x, g, mod_ref, shift_idx, scale_idx, eps):
    y = _rms(x, g, eps)
    return y * (1.0 + mod_ref[scale_idx:scale_idx + 1, :]) + mod_ref[shift_idx:shift_idx + 1, :]


ROW_PAD = SUBLANES


def _zero_row_pads(u_ref, rows):
    lead = (slice(None),) * (len(u_ref.shape) - 2)
    zeros = jnp.zeros(u_ref.shape[:-2] + (ROW_PAD, u_ref.shape[-1]), u_ref.dtype)
    u_ref[lead + (slice(0, ROW_PAD), slice(None))] = zeros
    u_ref[lead + (slice(ROW_PAD + rows, ROW_PAD + rows + ROW_PAD), slice(None))] = zeros


def _store_slabs(u_ref, first, u):
    for k in range(u.shape[1] // LANES):
        u_ref[first + k, ROW_PAD:ROW_PAD + u.shape[0], :] = u[:, k * LANES:(k + 1) * LANES]


def _seq_window(u_ref, off, rows, seq):
    w = u_ref[ROW_PAD + off:ROW_PAD + off + rows, :]
    if off == 0 or seq == rows:
        return w
    sub_row = lax.broadcasted_iota(jnp.int32, (SUBLANES, 1), 0)
    pieces = []
    for base in range(0, rows, seq):
        if off < 0:
            tile, bad = base, sub_row < -off
        else:
            tile, bad = base + seq - SUBLANES, sub_row >= SUBLANES - off
        pieces += [w[base:tile], jnp.where(bad, 0.0, w[tile:tile + SUBLANES]), w[tile + SUBLANES:base + seq]]
    return jnp.concatenate([p for p in pieces if p.shape[0]], axis=0)


def _adaln_body(cond_ref, w_ref, b_ref, o_ref):
    c = cond_ref[...]
    s = (c * jax.nn.sigmoid(c)).astype(MM_DTYPE)
    o_ref[...] = _dot(s, w_ref[...].astype(MM_DTYPE)) + b_ref[...]


def _adaln(cond, w_ada, b_ada, tn):
    n_layers, d, n_out = w_ada.shape
    return pl.pallas_call(
        _adaln_body,
        out_shape=jax.ShapeDtypeStruct((n_layers, MOD_ROWS, n_out), F32),
        grid=(n_layers, n_out // tn),
        in_specs=[
            _const_spec((MOD_ROWS, d)),
            pl.BlockSpec((None, d, tn), lambda l, n: (l, 0, n)),
            pl.BlockSpec((None, 1, tn), lambda l, n: (l, 0, n)),
        ],
        out_specs=pl.BlockSpec((None, MOD_ROWS, tn), lambda l, n: (l, 0, n)),
        compiler_params=_params(("arbitrary", "arbitrary")),
        name="adaln",
    )(cond, w_ada, b_ada.reshape(n_layers, 1, n_out))


def _nmm_body(x_ref, mod_ref, g_ref, w_ref, o_ref, h_ref, *, eps):
    @pl.when(pl.program_id(1) == 0)
    def _():
        h = _norm_mod(x_ref[...], g_ref[...], mod_ref, 0, 1, eps).astype(MM_DTYPE)
        h_ref[...] = h
        o_ref[...] = _dot(h, w_ref[...])

    @pl.when(pl.program_id(1) > 0)
    def _():
        o_ref[...] = _dot(h_ref[...], w_ref[...])


def _norm_mod_matmul(x, mod, mod_row, g, w, layer, wl, *, tm, tn, eps):
    n, d = x.shape
    m = w.shape[2]
    return pl.pallas_call(
        functools.partial(_nmm_body, eps=eps),
        out_shape=jax.ShapeDtypeStruct((n, m), F32),
        grid=(n // tm, m // tn),
        in_specs=[
            pl.BlockSpec((tm, d), lambda i, j: (i, 0)),
            _mod_spec(d, layer, mod_row, tm),
            _layer_spec(g, layer),
            pl.BlockSpec((None, d, tn), lambda i, j: (wl, 0, j)),
        ],
        out_specs=pl.BlockSpec((tm, tn), lambda i, j: (i, j)),
        scratch_shapes=[pltpu.VMEM((tm, d), MM_DTYPE)],
        compiler_params=_params(("arbitrary", "arbitrary")),
        name="norm_mod_matmul",
    )(x, mod, g, w)


def _rec_in_body(x_ref, mod_ref, g_ref, w_ref, cw_ref, cb_ref, o_ref, h_ref, u_ref, *, seq, n_y, eps):
    j = pl.program_id(1)
    tm = h_ref.shape[0]

    def gelu_tile(h):
        y = jax.nn.gelu(_dot(h, w_ref[...]))
        for c in range(o_ref.shape[0]):
            o_ref[c] = y[:, c * LANES:(c + 1) * LANES]

    @pl.when(j == 0)
    def _():
        h = _norm_mod(x_ref[...], g_ref[...], mod_ref, 0, 1, eps).astype(MM_DTYPE)
        h_ref[...] = h
        _zero_row_pads(u_ref, tm)
        gelu_tile(h)

    @pl.when((j > 0) & (j < n_y))
    def _():
        gelu_tile(h_ref[...])

    @pl.when(j >= n_y)
    def _():
        _store_slabs(u_ref, 0, _dot(h_ref[...], w_ref[...]))
        for c in range(u_ref.shape[0]):
            sl = slice(c * LANES, (c + 1) * LANES)
            xc = cb_ref[:, sl]
            for tap, off in enumerate((-2, -1, 0, 1)):
                xc = xc + _seq_window(u_ref.at[c], off, tm, seq) * cw_ref[tap:tap + 1, sl]
            o_ref[c] = xc


def _rec_in(x, mod, mod_row, g, w, w_conv, b_conv, layer, wl, *, seq, tm, tn, eps):
    n, d = x.shape
    m = w.shape[2]
    n_y = m // 2 // tn
    assert tm % seq == 0 and w_conv.shape[1] == 4 and m % (2 * tn) == 0
    conv_col = lambda i, j: (wl, 0, jnp.maximum(j - n_y, 0))
    return pl.pallas_call(
        functools.partial(_rec_in_body, seq=seq, n_y=n_y, eps=eps),
        out_shape=jax.ShapeDtypeStruct((m // LANES, n, LANES), F32),
        grid=(n // tm, m // tn),
        in_specs=[
            pl.BlockSpec((tm, d), lambda i, j: (i, 0)),
            _mod_spec(d, layer, mod_row, tm),
            _layer_spec(g, layer),
            pl.BlockSpec((None, d, tn), lambda i, j: (wl, 0, j)),
            pl.BlockSpec((None, 4, tn), conv_col),
            pl.BlockSpec((None, 1, tn), conv_col),
        ],
        out_specs=pl.BlockSpec((tn // LANES, tm, LANES), lambda i, j: (j, i, 0)),
        scratch_shapes=[pltpu.VMEM((tm, d), MM_DTYPE), pltpu.VMEM((tn // LANES, ROW_PAD + tm + ROW_PAD, LANES), F32)],
        compiler_params=_params(("arbitrary", "arbitrary")),
        name="rec_in",
    )(x, mod, g, w, w_conv, b_conv)


def _mmres_body(a_ref, w_ref, x_ref, mod_ref, o_ref):
    o_ref[...] = x_ref[...] + mod_ref[2:3, :] * _dot(a_ref[...], w_ref[...])


def _matmul_residual(a, w, wl, x, mod, layer, mod_row, *, tm):
    n, k = a.shape
    d = w.shape[2]
    return pl.pallas_call(
        _mmres_body,
        out_shape=jax.ShapeDtypeStruct((n, d), F32),
        grid=(n // tm,),
        in_specs=[
            pl.BlockSpec((tm, k), lambda i: (i, 0)),
            _layer_spec(w, wl),
            pl.BlockSpec((tm, d), lambda i: (i, 0)),
            _mod_spec(d, layer, mod_row, tm),
        ],
        out_specs=pl.BlockSpec((tm, d), lambda i: (i, 0)),
        compiler_params=_params(("arbitrary",)),
        name="matmul_residual",
    )(a, w, x, mod)


def _ffn_body(*refs, seq, eps, n_sub, final):
    if final:
        (x_ref, mod_ref, g_ref, wg_ref, wv_ref, cwg_ref, cwv_ref, cbg_ref, cbv_ref, wd_ref, gf_ref,
         o_ref, h_ref, act_ref, ug_ref, uv_ref) = refs
    else:
        (x_ref, mod_ref, g_ref, wg_ref, wv_ref, cwg_ref, cwv_ref, cbg_ref, cbv_ref, wd_ref,
         o_ref, h_ref, act_ref, ug_ref, uv_ref) = refs
    j = pl.program_id(1)
    n_tiles = pl.num_programs(1) - 1
    tm = h_ref.shape[0]
    ts = wg_ref.shape[1] // n_sub
    subs = [slice(s * ts, (s + 1) * ts) for s in range(n_sub)]

    def conv(u_ref, c, cw, cb):
        win = lambda off: _seq_window(u_ref.at[c], off, tm, seq)
        return cb + win(-1) * cw[0:1, :] + win(0) * cw[1:2, :] + win(1) * cw[2:3, :]

    def up():
        h = h_ref[...]
        for s, sl in enumerate(subs):
            _store_slabs(ug_ref, s * (ts // LANES), _dot(h, wg_ref[:, sl]))
            _store_slabs(uv_ref, s * (ts // LANES), _dot(h, wv_ref[:, sl]))

    def activate():
        for c in range(ug_ref.shape[0]):
            sl = slice(c * LANES, (c + 1) * LANES)
            hg = conv(ug_ref, c, 0.5 * cwg_ref[:, sl], 0.5 * cbg_ref[:, sl])
            val = conv(uv_ref, c, cwv_ref[:, sl], cbv_ref[:, sl])
            act_ref[:, sl] = (hg * (jnp.tanh(hg) + 1.0) * val).astype(MM_DTYPE)

    def down(act):
        o_ref[...] += _dot(act, wd_ref[...])

    @pl.when(j == 0)
    def _():
        h = _norm_mod(x_ref[...], g_ref[...], mod_ref, 3, 4, eps)
        h_ref[...] = h.astype(MM_DTYPE)
        o_ref[...] = jnp.zeros_like(o_ref)
        for u_ref in (ug_ref, uv_ref):
            _zero_row_pads(u_ref, tm)
        up()
        activate()

    @pl.when((j > 0) & (j < n_tiles))
    def _():
        down(act_ref[...])
        up()
        activate()

    @pl.when(j == n_tiles)
    def _():
        down(act_ref[...])
        y = x_ref[...] + mod_ref[5:6, :] * o_ref[...]
        if final:
            y = _rms(y, gf_ref[...], eps)
        o_ref[...] = y


def _conv_ffn(x, mod, layer, mod_row, g, w_up, w_conv, b_conv, w_down, g_final, *, seq, tm, tf, n_sub, x_buffers,
              eps):
    n, d = x.shape
    f = w_down.shape[1]
    assert w_conv.shape[1:] == (3, 2 * f) and tm % seq == 0 and f % tf == 0 and tf % (n_sub * LANES) == 0
    nf = f // tf
    final = g_final is not None
    up_tile = lambda j: jnp.minimum(j, nf - 1)
    lo = lambda rows: pl.BlockSpec((None, rows, tf), lambda i, j: (layer, 0, up_tile(j)))
    hi = lambda rows: pl.BlockSpec((None, rows, tf), lambda i, j: (layer, 0, up_tile(j) + nf))
    ins = [x, mod, g, w_up, w_up, w_conv, w_conv, b_conv, b_conv, w_down] + ([g_final] if final else [])
    in_specs = [
        pl.BlockSpec((tm, d), lambda i, j: (i, 0), pipeline_mode=pl.Buffered(x_buffers)),
        _mod_spec(d, layer, mod_row, tm),
        _layer_spec(g, layer),
        lo(d), hi(d), lo(3), hi(3), lo(1), hi(1),
        pl.BlockSpec((None, tf, d), lambda i, j: (layer, jnp.maximum(j - 1, 0), 0)),
    ] + ([_const_spec(g_final.shape)] if final else [])
    return pl.pallas_call(
        functools.partial(_ffn_body, seq=seq, eps=eps, n_sub=n_sub, final=final),
        out_shape=jax.ShapeDtypeStruct((n, d), F32),
        grid=(n // tm, nf + 1),
        in_specs=in_specs,
        out_specs=pl.BlockSpec((tm, d), lambda i, j: (i, 0)),
        scratch_shapes=[pltpu.VMEM((tm, d), MM_DTYPE), pltpu.VMEM((tm, tf), MM_DTYPE)]
        + [pltpu.VMEM((tf // LANES, ROW_PAD + tm + ROW_PAD, LANES), F32)] * 2,
        compiler_params=_params(("arbitrary", "arbitrary")),
        name="conv_ffn_final" if final else "conv_ffn",
    )(*ins)


def _attn_log2_scale(cfg):
    return float(cfg.nope + cfg.rope) ** -0.5 * math.log2(math.e)


def _mla_mid_body(*refs, cfg, rope):
    if rope:
        (proj_ref, gq_ref, gkv_ref, wqn_ref, wqr_ref, wqs_ref, wuk_ref, wuv_ref, cos_ref, sin_ref,
         qn_ref, qr_ref, kn_ref, v_ref, krp_ref) = refs
    else:
        (proj_ref, gq_ref, gkv_ref, wqn_ref, wqr_ref, wuk_ref, wuv_ref,
         qn_ref, qr_ref, kn_ref, v_ref, krp_ref, ckv_ref, kr_ref) = refs
    ql, kl = cfg.q_lora, cfg.kv_lora
    p = proj_ref[...]
    cqn = _rms(p[:, :ql], gq_ref[...], cfg.eps).astype(MM_DTYPE)
    ckv = _rms(p[:, ql:ql + kl], gkv_ref[...], cfg.eps)
    krp = p[:, ql + kl:ql + kl + LANES]
    qk_scale = _attn_log2_scale(cfg)
    qn_ref[...] = (_dot(cqn, wqn_ref[...]) * qk_scale).astype(MM_DTYPE)
    qr = _dot(cqn, wqr_ref[...])
    if rope:
        krs = p[:, ql + kl + LANES:ql + kl + 2 * LANES]
        qs = _dot(cqn, wqs_ref[...])
        cos = cos_ref[...]
        sin = sin_ref[...]
        for h in range(cfg.n_heads):
            sl = slice(h * LANES, (h + 1) * LANES)
            qr_ref[:, sl] = ((qr[:, sl] * cos + qs[:, sl] * sin) * qk_scale).astype(MM_DTYPE)
        krp_ref[...] = (krp * cos + krs * sin).astype(MM_DTYPE)
    else:
        qr_ref[...] = (qr * qk_scale).astype(MM_DTYPE)
        krp_ref[...] = krp.astype(MM_DTYPE)
        ckv_ref[...] = ckv
        kr_ref[...] = krp[:, :cfg.rope]
    ckv_b = ckv.astype(MM_DTYPE)
    kn_ref[...] = _dot(ckv_b, wuk_ref[...]).astype(MM_DTYPE)
    v_ref[...] = _dot(ckv_b, wuv_ref[...]).astype(MM_DTYPE)


def _mla_mid(proj, g_q, g_kv, w_qn, w_qr, w_qs, w_uk, w_uv, wl, cos, sin, *, cfg, seq, tm):
    n, pw = proj.shape
    rope = cos is not None
    hw = cfg.n_heads * LANES
    row = lambda i: (i, 0)
    ws = [g_q, g_kv, w_qn, w_qr] + ([w_qs] if rope else []) + [w_uk, w_uv]
    ins = [proj] + ws
    in_specs = [pl.BlockSpec((tm, pw), row)] + [_layer_spec(w, wl) for w in ws]
    out_shape = [jax.ShapeDtypeStruct((n, hw), MM_DTYPE)] * 4 + [jax.ShapeDtypeStruct((n, LANES), MM_DTYPE)]
    out_specs = [pl.BlockSpec((tm, hw), row)] * 4 + [pl.BlockSpec((tm, LANES), row)]
    if rope:
        bps = seq // tm
        ins += [cos, sin]
        in_specs += [pl.BlockSpec((tm, LANES), lambda i: (i % bps, 0))] * 2
    else:
        out_shape += [jax.ShapeDtypeStruct((n, cfg.kv_lora), F32), jax.ShapeDtypeStruct((n, cfg.rope), F32)]
        out_specs += [pl.BlockSpec((tm, cfg.kv_lora), row), pl.BlockSpec((tm, cfg.rope), row)]
    return pl.pallas_call(
        functools.partial(_mla_mid_body, cfg=cfg, rope=rope),
        out_shape=out_shape,
        grid=(n // tm,),
        in_specs=in_specs,
        out_specs=out_specs,
        compiler_params=_params(("arbitrary",)),
        name="mla_mid_rope" if rope else "mla_mid",
    )(*ins)


def _cache_expand_body(ckv_ref, kr_ref, wuk_ref, wuv_ref, kn_ref, v_ref, krp_ref, *, rope):
    c = ckv_ref[...].astype(MM_DTYPE)
    kn_ref[...] = _dot(c, wuk_ref[...]).astype(MM_DTYPE)
    v_ref[...] = _dot(c, wuv_ref[...]).astype(MM_DTYPE)
    krp_ref[:, :rope] = kr_ref[...].astype(MM_DTYPE)
    krp_ref[:, rope:] = jnp.zeros((krp_ref.shape[0], LANES - rope), MM_DTYPE)


def _cache_expand(cache_ckv, cache_krope, wl, w_uk, w_uv, *, cfg):
    b, _, p, kl = cache_ckv.shape
    hw = cfg.n_heads * LANES
    row = lambda i: (i, 0)
    return pl.pallas_call(
        functools.partial(_cache_expand_body, rope=cfg.rope),
        out_shape=[jax.ShapeDtypeStruct((b * p, hw), MM_DTYPE)] * 2 + [jax.ShapeDtypeStruct((b * p, LANES), MM_DTYPE)],
        grid=(b,),
        in_specs=[
            pl.BlockSpec((None, None, p, kl), lambda i: (i, wl, 0, 0)),
            pl.BlockSpec((None, None, p, cfg.rope), lambda i: (i, wl, 0, 0)),
            _layer_spec(w_uk, wl),
            _layer_spec(w_uv, wl),
        ],
        out_specs=[pl.BlockSpec((p, hw), row)] * 2 + [pl.BlockSpec((p, LANES), row)],
        compiler_params=_params(("arbitrary",)),
        name="cache_expand",
    )(cache_ckv, cache_krope, w_uk, w_uv)


def _attn_body(*refs, hb, cached, q_chunk):
    if cached:
        qn_ref, qr_ref, kn_ref, kr_ref, v_ref, knc_ref, krc_ref, vc_ref, o_ref = refs
    else:
        qn_ref, qr_ref, kn_ref, kr_ref, v_ref, o_ref = refs
    sq = qn_ref.shape[0]
    qc = min(sq, q_chunk)
    for h in range(hb):
        sl = slice(h * LANES, (h + 1) * LANES)
        k = jnp.concatenate([kn_ref[:, sl], kr_ref[...]], axis=1)
        if cached:
            kc = jnp.concatenate([knc_ref[:, sl], krc_ref[...]], axis=1)
        for r0 in range(0, sq, qc):
            rows = slice(r0, r0 + qc)
            q = jnp.concatenate([qn_ref[rows, sl], qr_ref[rows, sl]], axis=1)
            s = _dot_nt(q, k)
            m = jnp.max(s, axis=-1, keepdims=True)
            if cached:
                sc = _dot_nt(q, kc)
                m = jnp.maximum(m, jnp.max(sc, axis=-1, keepdims=True))
                pc = jnp.exp2(sc - m)
            p = jnp.exp2(s - m)
            den = jnp.sum(p, axis=-1, keepdims=True)
            o = _dot(p.astype(MM_DTYPE), v_ref[:, sl])
            if cached:
                den = den + jnp.sum(pc, axis=-1, keepdims=True)
                o = o + _dot(pc.astype(MM_DTYPE), vc_ref[:, sl])
            o_ref[rows, sl] = (o / den).astype(MM_DTYPE)


def _attention(qn, qr, kn, krp, v, cache, *, cfg, seq, hb):
    n, hw = qn.shape
    cached = cache is not None
    blk = lambda rows: pl.BlockSpec((rows, hb * LANES), lambda b, g: (b, g))
    shared = lambda rows: pl.BlockSpec((rows, LANES), lambda b, g: (b, 0))
    ins = [qn, qr, kn, krp, v]
    in_specs = [blk(seq), blk(seq), blk(seq), shared(seq), blk(seq)]
    if cached:
        knc, vc, krc = cache
        past = knc.shape[0] // (n // seq)
        ins += [knc, krc, vc]
        in_specs += [blk(past), shared(past), blk(past)]
    return pl.pallas_call(
        functools.partial(_attn_body, hb=hb, cached=cached, q_chunk=ATTN_Q_CHUNK),
        out_shape=jax.ShapeDtypeStruct((n, hw), MM_DTYPE),
        grid=(n // seq, cfg.n_heads // hb),
        in_specs=in_specs,
        out_specs=blk(seq),
        compiler_params=_params(("arbitrary", "arbitrary")),
        name="attention_cached" if cached else "attention",
    )(*ins)


SCAN_PITCH_PAD = 4
SCAN_SCRATCH_BYTES = 32 * 1024 * 1024


def _scan_lane_tiles(seq, n_blocks):
    per_tile = 6 * SUBLANES * (seq + SCAN_PITCH_PAD) * LANES * 4
    return max(t for t in (1, 2, 4) if n_blocks % t == 0 and (t == 1 or t * per_tile <= SCAN_SCRATCH_BYTES))


def _rec_scan_body(*refs, seq, cfg, has_state, tiles):
    n_in = 7 if has_state else 5
    yg_ref, xc_ref, wg_ref, bg_ref, lam_ref = refs[:5]
    o_ref = refs[n_in]
    scratch = refs[-6 * tiles:]
    nseq = SUBLANES
    pitch = seq + SCAN_PITCH_PAD
    lam = lam_ref[...]
    softplus = jnp.maximum(-lam, 0.0) + jnp.log1p(jnp.exp(-jnp.abs(lam)))
    half_decay = (-0.5 * cfg.lru_c * math.log2(math.e)) * softplus
    for k in range(tiles):
        af_ref, uf_ref, ab_ref, ub_ref = scratch[6 * k:6 * k + 4]
        lanes = slice(k * LANES, (k + 1) * LANES)
        wg = wg_ref[k]
        bg = bg_ref[k]
        for b in range(nseq):
            xc = xc_ref[k, b * seq:(b + 1) * seq, :]
            th = jnp.tanh(_dot(xc.astype(MM_DTYPE), wg) + bg)
            xh = 0.5 * xc
            for d, (a_ref, u_ref) in enumerate(((af_ref, uf_ref), (ab_ref, ub_ref))):
                tx = th[:, (2 * d) * LANES:(2 * d + 1) * LANES]
                ta = th[:, (2 * d + 1) * LANES:(2 * d + 2) * LANES]
                hd = half_decay[d:d + 1, lanes]
                a = jnp.exp2(ta * hd + hd)
                z = 1.0 - a * a
                u = jnp.where(z > 0.0, z * lax.rsqrt(z), 0.0) * ((tx + 1.0) * xh)
                a_ref[b * pitch:b * pitch + seq, :] = a
                u_ref[b * pitch:b * pitch + seq, :] = u

    if has_state:
        h0 = [refs[5 + d][:, k * LANES:(k + 1) * LANES] for k in range(tiles) for d in (0, 1)]
    else:
        h0 = [jnp.zeros((nseq, LANES), F32)] * (2 * tiles)

    def step(t, carry):
        fwd = pl.ds(t, nseq, stride=pitch)
        bwd = pl.ds(seq - 1 - t, nseq, stride=pitch)
        new = []
        for k in range(tiles):
            af_ref, uf_ref, ab_ref, ub_ref, hf_ref, hb_ref = scratch[6 * k:6 * k + 6]
            hf = af_ref[fwd, :] * carry[2 * k] + uf_ref[fwd, :]
            hf_ref[fwd, :] = hf
            hb = ab_ref[bwd, :] * carry[2 * k + 1] + ub_ref[bwd, :]
            hb_ref[bwd, :] = hb
            new += [hf, hb]
        return tuple(new)

    last = lax.fori_loop(0, seq, step, tuple(h0), unroll=8)
    for k in range(tiles):
        lanes = slice(k * LANES, (k + 1) * LANES)
        hf_ref, hb_ref = scratch[6 * k + 4:6 * k + 6]
        if not has_state:
            refs[n_in + 1][:, lanes] = last[2 * k]
            refs[n_in + 2][:, lanes] = last[2 * k + 1]
        for b in range(nseq):
            hsum = hf_ref[b * pitch:b * pitch + seq, :] + hb_ref[b * pitch:b * pitch + seq, :]
            o_ref[b * seq:(b + 1) * seq, lanes] = (hsum * yg_ref[k, b * seq:(b + 1) * seq, :]).astype(MM_DTYPE)


def _rec_scan(proj, w_gates, b_gates, lam, wl, state, *, cfg, seq, tiles):
    nb2, n, _ = proj.shape
    nb = nb2 // 2
    w = nb * LANES
    assert nb == cfg.lru_blocks and nb % tiles == 0
    rows = SUBLANES * seq
    width = tiles * LANES
    has_state = state is not None
    col = lambda g, c: (g, c)
    ins = [proj, proj, w_gates, b_gates, lam]
    in_specs = [
        pl.BlockSpec((tiles, rows, LANES), lambda g, c: (c, g, 0)),
        pl.BlockSpec((tiles, rows, LANES), lambda g, c: (c + nb // tiles, g, 0)),
        pl.BlockSpec((None, tiles, LANES, 4 * LANES), lambda g, c: (wl, c, 0, 0)),
        pl.BlockSpec((None, tiles, 1, 4 * LANES), lambda g, c: (wl, c, 0, 0)),
        pl.BlockSpec((None, 2, width), lambda g, c: (wl, 0, c)),
    ]
    out_shape = [jax.ShapeDtypeStruct((n, w), MM_DTYPE)]
    out_specs = [pl.BlockSpec((rows, width), col)]
    if has_state:
        ins += list(state)
        in_specs += [pl.BlockSpec((SUBLANES, width), col)] * 2
    else:
        nstate = n // seq
        out_shape += [jax.ShapeDtypeStruct((nstate, w), F32)] * 2
        out_specs += [pl.BlockSpec((SUBLANES, width), col)] * 2
    scratch = [pltpu.VMEM((SUBLANES * (seq + SCAN_PITCH_PAD), LANES), F32)] * (6 * tiles)
    return pl.pallas_call(
        functools.partial(_rec_scan_body, seq=seq, cfg=cfg, has_state=has_state, tiles=tiles),
        out_shape=out_shape,
        grid=(n // rows, nb // tiles),
        in_specs=in_specs,
        out_specs=out_specs,
        scratch_shapes=scratch,
        compiler_params=_params(("arbitrary", "arbitrary")),
        name="rec_scan_state" if has_state else "rec_scan",
    )(*ins)


def _swap_halves(w, rope):
    q = rope // 4
    return jnp.concatenate([w[..., q:2 * q], w[..., :q], w[..., 3 * q:], w[..., 2 * q:3 * q]], axis=-1)


def _pad_lanes(w):
    return jnp.pad(w, [(0, 0)] * (w.ndim - 1) + [(0, LANES - w.shape[-1])])


def _rope_tables(seq, cfg):
    rows = seq // cfg.grid_w
    row = jnp.repeat(jnp.arange(rows), cfg.grid_w).astype(F32)
    col = jnp.tile(jnp.arange(cfg.grid_w), rows).astype(F32)
    half = cfg.rope // 2
    inv = 1.0 / (cfg.rope_base ** (jnp.arange(0, half, 2, dtype=F32) / half))
    ar, ac = row[:, None] * inv, col[:, None] * inv
    cos = jnp.concatenate([jnp.cos(ar), jnp.cos(ar), jnp.cos(ac), jnp.cos(ac)], axis=-1)
    sin = jnp.concatenate([-jnp.sin(ar), jnp.sin(ar), -jnp.sin(ac), jnp.sin(ac)], axis=-1)
    return _pad_lanes(cos), _pad_lanes(sin)


def _pick(n, pref):
    return pref if n % pref == 0 else n


def _forward(cfg, x_prompt, x_sample, cache_ckv, cache_krope, state_lru, c, c_ctx,
             g_mix, g_ffn, g_final, w_ada, b_ada,
             w_mla_in, g_mla_q, g_mla_kv, w_mla_uq, w_mla_uk, w_mla_uv, w_mla_o,
             w_rec_in, w_rec_conv, b_rec_conv, w_rec_gx, b_rec_gx, w_rec_ga, b_rec_ga,
             rec_lambda, w_rec_out,
             w_ffn_up, w_ffn_conv, b_ffn_conv, w_ffn_down):
    depth, d = g_mix.shape
    bp, sp, _ = x_prompt.shape
    bs, ss, _ = x_sample.shape
    assert cfg.nope == LANES and cfg.v_dim == LANES and cfg.rope <= LANES and bs + 1 <= MOD_ROWS
    assert bp % SUBLANES == 0 and bs % SUBLANES == 0
    eps = cfg.eps
    cast = lambda w: w.astype(MM_DTYPE)
    row3 = lambda v: v[:, None, :]

    cond = jnp.zeros((MOD_ROWS, d), F32).at[0].set(c_ctx).at[1:1 + bs].set(c)
    mod = _adaln(cond, w_ada, b_ada, _pick(w_ada.shape[-1], ADALN_TILE)).reshape(depth, MOD_ROWS, N_MOD, d)

    ql, kl, r, nh = cfg.q_lora, cfg.kv_lora, cfg.rope, cfg.n_heads
    w_kr = w_mla_in[..., ql + kl:]
    w_in_p = cast(jnp.concatenate(
        [w_mla_in[..., :ql + kl], _pad_lanes(w_kr), _pad_lanes(_swap_halves(w_kr, r))], axis=-1))
    uq = w_mla_uq.reshape(w_mla_uq.shape[0], ql, nh, cfg.nope + r)
    flat = lambda w: cast(w.reshape(w.shape[0], ql, nh * LANES))
    w_qn, w_qr, w_qs = flat(uq[..., :cfg.nope]), flat(_pad_lanes(uq[..., cfg.nope:])), flat(
        _pad_lanes(_swap_halves(uq[..., cfg.nope:], r)))
    w_uk, w_uv, w_o = cast(w_mla_uk), cast(w_mla_uv), cast(w_mla_o)
    g_q, g_kv = row3(g_mla_q), row3(g_mla_kv)
    cos, sin = _rope_tables(ss, cfg)

    w_gates = cast(0.5 * jnp.concatenate([w_rec_gx[:, 0], w_rec_ga[:, 0], w_rec_gx[:, 1], w_rec_ga[:, 1]], axis=-1))
    nb = cfg.lru_blocks
    blk = lambda b: b.reshape(b.shape[0], nb, 1, LANES)
    b_gates = 0.5 * jnp.concatenate(
        [blk(b_rec_gx[:, 0]), blk(b_rec_ga[:, 0]), blk(b_rec_gx[:, 1]), blk(b_rec_ga[:, 1])], axis=-1)
    tf = _pick(w_ffn_down.shape[1], FF_TILE)
    w_rin, w_rout = cast(w_rec_in), cast(w_rec_out)
    w_up, w_down = cast(w_ffn_up), cast(w_ffn_down)
    g_mix3, g_ffn3, b_rconv, b_fconv = row3(g_mix), row3(g_ffn), row3(b_rec_conv), row3(b_ffn_conv)

    ctx_row = lambda i, tm: 0

    def run_group(x, seq, mod_row, sample):
        n = x.shape[0]
        rows = lambda pref: min(seq, pref) if sample else _pick(n, pref)
        tm_seq = max(seq, rows(TOKEN_BLOCK))
        ckv_new, kr_new, lru_new = [], [], []
        for layer in range(depth):
            j = layer // 2
            if layer % 2 == 0:
                proj = _norm_mod_matmul(x, mod, mod_row, g_mix3, w_in_p, layer, j, tm=rows(TOKEN_BLOCK),
                                        tn=w_in_p.shape[-1], eps=eps)
                if sample:
                    qn, qr, kn, v, krp = _mla_mid(proj, g_q, g_kv, w_qn, w_qr, w_qs, w_uk, w_uv, j, cos, sin,
                                                  cfg=cfg, seq=seq, tm=rows(MLA_MID_BLOCK))
                    cache = _cache_expand(cache_ckv, cache_krope, j, w_uk, w_uv, cfg=cfg)
                    att = _attention(qn, qr, kn, krp, v, cache, cfg=cfg, seq=seq, hb=min(ATTN_CACHED_HEADS, nh))
                else:
                    qn, qr, kn, v, krp, ckv, kr = _mla_mid(proj, g_q, g_kv, w_qn, w_qr, None, w_uk, w_uv, j,
                                                           None, None, cfg=cfg, seq=seq, tm=rows(MLA_MID_BLOCK))
                    ckv_new.append(ckv)
                    kr_new.append(kr)
                    att = _attention(qn, qr, kn, krp, v, None, cfg=cfg, seq=seq, hb=nh)
                x = _matmul_residual(att, w_o, j, x, mod, layer, mod_row, tm=rows(RESIDUAL_BLOCK))
            else:
                proj = _rec_in(x, mod, mod_row, g_mix3, w_rin, w_rec_conv, b_rconv, layer, j, seq=seq,
                               tm=tm_seq, tn=_pick(w_rin.shape[-1] // 2, IN_PROJ_TILE), eps=eps)
                outs = _rec_scan(proj, w_gates, b_gates, rec_lambda, j,
                                 (state_lru[:, j, 0], state_lru[:, j, 1]) if sample else None, cfg=cfg, seq=seq,
                                 tiles=_scan_lane_tiles(seq, nb))
                if not sample:
                    lru_new.append(jnp.stack(outs[1:], axis=1))
                x = _matmul_residual(outs[0], w_rout, j, x, mod, layer, mod_row, tm=rows(RESIDUAL_BLOCK))
            x = _conv_ffn(x, mod, layer, mod_row, g_ffn3, w_up, w_ffn_conv, b_fconv, w_down,
                          g_final[None] if layer == depth - 1 else None, seq=seq, tm=tm_seq, tf=tf,
                          n_sub=2 if tf % (2 * LANES) == 0 else 1, x_buffers=2, eps=eps)
        return x, ckv_new, kr_new, lru_new

    y_p, ckv_new, kr_new, lru_new = run_group(x_prompt.reshape(bp * sp, d), sp, ctx_row, False)
    y_s, _, _, _ = run_group(x_sample.reshape(bs * ss, d), ss, lambda i, tm: 1 + (i * tm) // ss, True)
    new_ckv = jnp.stack([t.reshape(bp, sp, kl) for t in ckv_new], axis=1)
    new_kr = jnp.stack([t.reshape(bp, sp, r) for t in kr_new], axis=1)
    new_lru = jnp.stack(lru_new, axis=1)
    return y_p.reshape(bp, sp, d), y_s.reshape(bs, ss, d), new_ckv, new_kr, new_lru


CFG = Cfg(n_heads=16, q_lora=512, kv_lora=512, nope=128, rope=64, v_dim=128, grid_w=64, rope_base=10000.0,
          lru_blocks=16, lru_c=8.0, eps=1e-6)


def kernel(x_prompt, x_sample, cache_ckv, cache_krope, state_lru, c, c_ctx, g_mix, g_ffn, g_final, w_ada, b_ada, w_mla_in, g_mla_q, g_mla_kv, w_mla_uq, w_mla_uk, w_mla_uv, w_mla_o, w_rec_in, w_rec_conv, b_rec_conv, w_rec_gx, b_rec_gx, w_rec_ga, b_rec_ga, rec_lambda, w_rec_out, w_ffn_up, w_ffn_conv, b_ffn_conv, w_ffn_down):
    return _forward(CFG, x_prompt, x_sample, cache_ckv, cache_krope, state_lru, c, c_ctx, g_mix, g_ffn, g_final,
                    w_ada, b_ada, w_mla_in, g_mla_q, g_mla_kv, w_mla_uq, w_mla_uk, w_mla_uv, w_mla_o,
                    w_rec_in, w_rec_conv, b_rec_conv, w_rec_gx, b_rec_gx, w_rec_ga, b_rec_ga, rec_lambda, w_rec_out,
                    w_ffn_up, w_ffn_conv, b_ffn_conv, w_ffn_down)
```

```python
import functools
import math
from typing import NamedTuple

import jax
import jax.numpy as jnp
from jax import lax
from jax.experimental import pallas as pl
from jax.experimental.pallas import tpu as pltpu

F32 = jnp.float32
MM_DTYPE = jnp.bfloat16

LANES = 128
SUBLANES = 8
VMEM_LIMIT_BYTES = 60 * 1024 * 1024
N_MOD = 6
MOD_ROWS = 16

TOKEN_BLOCK = 1024
RESIDUAL_BLOCK = 512
MLA_MID_BLOCK = 512
FF_TILE = 512
IN_PROJ_TILE = 1024
ADALN_TILE = 1024
ATTN_Q_CHUNK = 512
ATTN_CACHED_HEADS = 4


class Cfg(NamedTuple):
    n_heads: int
    q_lora: int
    kv_lora: int
    nope: int
    rope: int
    v_dim: int
    grid_w: int
    rope_base: float
    lru_blocks: int
    lru_c: float
    eps: float


def _params(sem):
    return pltpu.CompilerParams(dimension_semantics=sem, vmem_limit_bytes=VMEM_LIMIT_BYTES)


def _dot(a, b):
    return jnp.dot(a, b, preferred_element_type=F32)


def _dot_nt(a, b):
    return lax.dot_general(a, b, (((1,), (1,)), ((), ())), preferred_element_type=F32)


def _const_spec(shape):
    nd = len(shape)
    return pl.BlockSpec(shape, lambda *_: (0,) * nd)


def _layer_spec(arr, layer):
    nd = arr.ndim
    return pl.BlockSpec((None,) + arr.shape[1:], lambda *_: (layer,) + (0,) * (nd - 1))


def _mod_spec(d, layer, mod_row, tm):
    return pl.BlockSpec((None, None, N_MOD, d), lambda i, *_: (layer, mod_row(i, tm), 0, 0))


def _rms(x, g, eps):
    return x * lax.rsqrt(jnp.mean(x * x, axis=-1, keepdims=True) + eps) * g


def _norm_mod(x, g, mod_ref, shift_idx, scale_idx, eps):
    y = _rms(x, g, eps)
    return y * (1.0 + mod_ref[scale_idx:scale_idx + 1, :]) + mod_ref[shift_idx:shift_idx + 1, :]


ROW_PAD = SUBLANES


def _zero_row_pads(u_ref, rows):
    lead = (slice(None),) * (len(u_ref.shape) - 2)
    zeros = jnp.zeros(u_ref.shape[:-2] + (ROW_PAD, u_ref.shape[-1]), u_ref.dtype)
    u_ref[lead + (slice(0, ROW_PAD), slice(None))] = zeros
    u_ref[lead + (slice(ROW_PAD + rows, ROW_PAD + rows + ROW_PAD), slice(None))] = zeros


def _store_slabs(u_ref, first, u):
    for k in range(u.shape[1] // LANES):
        u_ref[first + k, ROW_PAD:ROW_PAD + u.shape[0], :] = u[:, k * LANES:(k + 1) * LANES]


def _seq_window(u_ref, off, rows, seq):
    w = u_ref[ROW_PAD + off:ROW_PAD + off + rows, :]
    if off == 0 or seq == rows:
        return w
    sub_row = lax.broadcasted_iota(jnp.int32, (SUBLANES, 1), 0)
    pieces = []
    for base in range(0, rows, seq):
        if off < 0:
            tile, bad = base, sub_row < -off
        else:
            tile, bad = base + seq - SUBLANES, sub_row >= SUBLANES - off
        pieces += [w[base:tile], jnp.where(bad, 0.0, w[tile:tile + SUBLANES]), w[tile + SUBLANES:base + seq]]
    return jnp.concatenate([p for p in pieces if p.shape[0]], axis=0)


def _adaln_body(cond_ref, w_ref, b_ref, o_ref):
    c = cond_ref[...]
    s = (c * jax.nn.sigmoid(c)).astype(MM_DTYPE)
    o_ref[...] = _dot(s, w_ref[...].astype(MM_DTYPE)) + b_ref[...]


def _adaln(cond, w_ada, b_ada, tn):
    n_layers, d, n_out = w_ada.shape
    return pl.pallas_call(
        _adaln_body,
        out_shape=jax.ShapeDtypeStruct((n_layers, MOD_ROWS, n_out), F32),
        grid=(n_layers, n_out // tn),
        in_specs=[
            _const_spec((MOD_ROWS, d)),
            pl.BlockSpec((None, d, tn), lambda l, n: (l, 0, n)),
            pl.BlockSpec((None, 1, tn), lambda l, n: (l, 0, n)),
        ],
        out_specs=pl.BlockSpec((None, MOD_ROWS, tn), lambda l, n: (l, 0, n)),
        compiler_params=_params(("arbitrary", "arbitrary")),
        name="adaln",
    )(cond, w_ada, b_ada.reshape(n_layers, 1, n_out))


def _nmm_body(x_ref, mod_ref, g_ref, w_ref, o_ref, h_ref, *, eps):
    @pl.when(pl.program_id(1) == 0)
    def _():
        h = _norm_mod(x_ref[...], g_ref[...], mod_ref, 0, 1, eps).astype(MM_DTYPE)
        h_ref[...] = h
        o_ref[...] = _dot(h, w_ref[...])

    @pl.when(pl.program_id(1) > 0)
    def _():
        o_ref[...] = _dot(h_ref[...], w_ref[...])


def _norm_mod_matmul(x, mod, mod_row, g, w, layer, wl, *, tm, tn, eps):
    n, d = x.shape
    m = w.shape[2]
    return pl.pallas_call(
        functools.partial(_nmm_body, eps=eps),
        out_shape=jax.ShapeDtypeStruct((n, m), F32),
        grid=(n // tm, m // tn),
        in_specs=[
            pl.BlockSpec((tm, d), lambda i, j: (i, 0)),
            _mod_spec(d, layer, mod_row, tm),
            _layer_spec(g, layer),
            pl.BlockSpec((None, d, tn), lambda i, j: (wl, 0, j)),
        ],
        out_specs=pl.BlockSpec((tm, tn), lambda i, j: (i, j)),
        scratch_shapes=[pltpu.VMEM((tm, d), MM_DTYPE)],
        compiler_params=_params(("arbitrary", "arbitrary")),
        name="norm_mod_matmul",
    )(x, mod, g, w)


def _rec_in_body(x_ref, mod_ref, g_ref, w_ref, cw_ref, cb_ref, o_ref, h_ref, u_ref, *, seq, n_y, eps):
    j = pl.program_id(1)
    tm = h_ref.shape[0]

    def gelu_tile(h):
        y = jax.nn.gelu(_dot(h, w_ref[...]))
        for c in range(o_ref.shape[0]):
            o_ref[c] = y[:, c * LANES:(c + 1) * LANES]

    @pl.when(j == 0)
    def _():
        h = _norm_mod(x_ref[...], g_ref[...], mod_ref, 0, 1, eps).astype(MM_DTYPE)
        h_ref[...] = h
        _zero_row_pads(u_ref, tm)
        gelu_tile(h)

    @pl.when((j > 0) & (j < n_y))
    def _():
        gelu_tile(h_ref[...])

    @pl.when(j >= n_y)
    def _():
        _store_slabs(u_ref, 0, _dot(h_ref[...], w_ref[...]))
        for c in range(u_ref.shape[0]):
            sl = slice(c * LANES, (c + 1) * LANES)
            xc = cb_ref[:, sl]
            for tap, off in enumerate((-2, -1, 0, 1)):
                xc = xc + _seq_window(u_ref.at[c], off, tm, seq) * cw_ref[tap:tap + 1, sl]
            o_ref[c] = xc


def _rec_in(x, mod, mod_row, g, w, w_conv, b_conv, layer, wl, *, seq, tm, tn, eps):
    n, d = x.shape
    m = w.shape[2]
    n_y = m // 2 // tn
    assert tm % seq == 0 and w_conv.shape[1] == 4 and m % (2 * tn) == 0
    conv_col = lambda i, j: (wl, 0, jnp.maximum(j - n_y, 0))
    return pl.pallas_call(
        functools.partial(_rec_in_body, seq=seq, n_y=n_y, eps=eps),
        out_shape=jax.ShapeDtypeStruct((m // LANES, n, LANES), F32),
        grid=(n // tm, m // tn),
        in_specs=[
            pl.BlockSpec((tm, d), lambda i, j: (i, 0)),
            _mod_spec(d, layer, mod_row, tm),
            _layer_spec(g, layer),
            pl.BlockSpec((None, d, tn), lambda i, j: (wl, 0, j)),
            pl.BlockSpec((None, 4, tn), conv_col),
            pl.BlockSpec((None, 1, tn), conv_col),
        ],
        out_specs=pl.BlockSpec((tn // LANES, tm, LANES), lambda i, j: (j, i, 0)),
        scratch_shapes=[pltpu.VMEM((tm, d), MM_DTYPE), pltpu.VMEM((tn // LANES, ROW_PAD + tm + ROW_PAD, LANES), F32)],
        compiler_params=_params(("arbitrary", "arbitrary")),
        name="rec_in",
    )(x, mod, g, w, w_conv, b_conv)


def _mmres_body(a_ref, w_ref, x_ref, mod_ref, o_ref):
    o_ref[...] = x_ref[...] + mod_ref[2:3, :] * _dot(a_ref[...], w_ref[...])


def _matmul_residual(a, w, wl, x, mod, layer, mod_row, *, tm):
    n, k = a.shape
    d = w.shape[2]
    return pl.pallas_call(
        _mmres_body,
        out_shape=jax.ShapeDtypeStruct((n, d), F32),
        grid=(n // tm,),
        in_specs=[
            pl.BlockSpec((tm, k), lambda i: (i, 0)),
            _layer_spec(w, wl),
            pl.BlockSpec((tm, d), lambda i: (i, 0)),
            _mod_spec(d, layer, mod_row, tm),
        ],
        out_specs=pl.BlockSpec((tm, d), lambda i: (i, 0)),
        compiler_params=_params(("arbitrary",)),
        name="matmul_residual",
    )(a, w, x, mod)


def _ffn_body(*refs, seq, eps, n_sub, final):
    if final:
        (x_ref, mod_ref, g_ref, wg_ref, wv_ref, cwg_ref, cwv_ref, cbg_ref, cbv_ref, wd_ref, gf_ref,
         o_ref, h_ref, act_ref, ug_ref, uv_ref) = refs
    else:
        (x_ref, mod_ref, g_ref, wg_ref, wv_ref, cwg_ref, cwv_ref, cbg_ref, cbv_ref, wd_ref,
         o_ref, h_ref, act_ref, ug_ref, uv_ref) = refs
    j = pl.program_id(1)
    n_tiles = pl.num_programs(1) - 1
    tm = h_ref.shape[0]
    ts = wg_ref.shape[1] // n_sub
    subs = [slice(s * ts, (s + 1) * ts) for s in range(n_sub)]

    def conv(u_ref, c, cw, cb):
        win = lambda off: _seq_window(u_ref.at[c], off, tm, seq)
        return cb + win(-1) * cw[0:1, :] + win(0) * cw[1:2, :] + win(1) * cw[2:3, :]

    def up():
        h = h_ref[...]
        for s, sl in enumerate(subs):
            _store_slabs(ug_ref, s * (ts // LANES), _dot(h, wg_ref[:, sl]))
            _store_slabs(uv_ref, s * (ts // LANES), _dot(h, wv_ref[:, sl]))

    def activate():
        for c in range(ug_ref.shape[0]):
            sl = slice(c * LANES, (c + 1) * LANES)
            hg = conv(ug_ref, c, 0.5 * cwg_ref[:, sl], 0.5 * cbg_ref[:, sl])
            val = conv(uv_ref, c, cwv_ref[:, sl], cbv_ref[:, sl])
            act_ref[:, sl] = (hg * (jnp.tanh(hg) + 1.0) * val).astype(MM_DTYPE)

    def down(act):
        o_ref[...] += _dot(act, wd_ref[...])

    @pl.when(j == 0)
    def _():
        h = _norm_mod(x_ref[...], g_ref[...], mod_ref, 3, 4, eps)
        h_ref[...] = h.astype(MM_DTYPE)
        o_ref[...] = jnp.zeros_like(o_ref)
        for u_ref in (ug_ref, uv_ref):
            _zero_row_pads(u_ref, tm)
        up()
        activate()

    @pl.when((j > 0) & (j < n_tiles))
    def _():
        down(act_ref[...])
        up()
        activate()

    @pl.when(j == n_tiles)
    def _():
        down(act_ref[...])
        y = x_ref[...] + mod_ref[5:6, :] * o_ref[...]
        if final:
            y = _rms(y, gf_ref[...], eps)
        o_ref[...] = y


def _conv_ffn(x, mod, layer, mod_row, g, w_up, w_conv, b_conv, w_down, g_final, *, seq, tm, tf, n_sub, x_buffers,
              eps):
    n, d = x.shape
    f = w_down.shape[1]
    assert w_conv.shape[1:] == (3, 2 * f) and tm % seq == 0 and f % tf == 0 and tf % (n_sub * LANES) == 0
    nf = f // tf
    final = g_final is not None
    up_tile = lambda j: jnp.minimum(j, nf - 1)
    lo = lambda rows: pl.BlockSpec((None, rows, tf), lambda i, j: (layer, 0, up_tile(j)))
    hi = lambda rows: pl.BlockSpec((None, rows, tf), lambda i, j: (layer, 0, up_tile(j) + nf))
    ins = [x, mod, g, w_up, w_up, w_conv, w_conv, b_conv, b_conv, w_down] + ([g_final] if final else [])
    in_specs = [
        pl.BlockSpec((tm, d), lambda i, j: (i, 0), pipeline_mode=pl.Buffered(x_buffers)),
        _mod_spec(d, layer, mod_row, tm),
        _layer_spec(g, layer),
        lo(d), hi(d), lo(3), hi(3), lo(1), hi(1),
        pl.BlockSpec((None, tf, d), lambda i, j: (layer, jnp.maximum(j - 1, 0), 0)),
    ] + ([_const_spec(g_final.shape)] if final else [])
    return pl.pallas_call(
        functools.partial(_ffn_body, seq=seq, eps=eps, n_sub=n_sub, final=final),
        out_shape=jax.ShapeDtypeStruct((n, d), F32),
        grid=(n // tm, nf + 1),
        in_specs=in_specs,
        out_specs=pl.BlockSpec((tm, d), lambda i, j: (i, 0)),
        scratch_shapes=[pltpu.VMEM((tm, d), MM_DTYPE), pltpu.VMEM((tm, tf), MM_DTYPE)]
        + [pltpu.VMEM((tf // LANES, ROW_PAD + tm + ROW_PAD, LANES), F32)] * 2,
        compiler_params=_params(("arbitrary", "arbitrary")),
        name="conv_ffn_final" if final else "conv_ffn",
    )(*ins)


def _attn_log2_scale(cfg):
    return float(cfg.nope + cfg.rope) ** -0.5 * math.log2(math.e)


def _mla_mid_body(*refs, cfg, rope):
    if rope:
        (proj_ref, gq_ref, gkv_ref, wqn_ref, wqr_ref, wqs_ref, wuk_ref, wuv_ref, cos_ref, sin_ref,
         qn_ref, qr_ref, kn_ref, v_ref, krp_ref) = refs
    else:
        (proj_ref, gq_ref, gkv_ref, wqn_ref, wqr_ref, wuk_ref, wuv_ref,
         qn_ref, qr_ref, kn_ref, v_ref, krp_ref, ckv_ref, kr_ref) = refs
    ql, kl = cfg.q_lora, cfg.kv_lora
    p = proj_ref[...]
    cqn = _rms(p[:, :ql], gq_ref[...], cfg.eps).astype(MM_DTYPE)
    ckv = _rms(p[:, ql:ql + kl], gkv_ref[...], cfg.eps)
    krp = p[:, ql + kl:ql + kl + LANES]
    qk_scale = _attn_log2_scale(cfg)
    qn_ref[...] = (_dot(cqn, wqn_ref[...]) * qk_scale).astype(MM_DTYPE)
    qr = _dot(cqn, wqr_ref[...])
    if rope:
        krs = p[:, ql + kl + LANES:ql + kl + 2 * LANES]
        qs = _dot(cqn, wqs_ref[...])
        cos = cos_ref[...]
        sin = sin_ref[...]
        for h in range(cfg.n_heads):
            sl = slice(h * LANES, (h + 1) * LANES)
            qr_ref[:, sl] = ((qr[:, sl] * cos + qs[:, sl] * sin) * qk_scale).astype(MM_DTYPE)
        krp_ref[...] = (krp * cos + krs * sin).astype(MM_DTYPE)
    else:
        qr_ref[...] = (qr * qk_scale).astype(MM_DTYPE)
        krp_ref[...] = krp.astype(MM_DTYPE)
        ckv_ref[...] = ckv
        kr_ref[...] = krp[:, :cfg.rope]
    ckv_b = ckv.astype(MM_DTYPE)
    kn_ref[...] = _dot(ckv_b, wuk_ref[...]).astype(MM_DTYPE)
    v_ref[...] = _dot(ckv_b, wuv_ref[...]).astype(MM_DTYPE)


def _mla_mid(proj, g_q, g_kv, w_qn, w_qr, w_qs, w_uk, w_uv, wl, cos, sin, *, cfg, seq, tm):
    n, pw = proj.shape
    rope = cos is not None
    hw = cfg.n_heads * LANES
    row = lambda i: (i, 0)
    ws = [g_q, g_kv, w_qn, w_qr] + ([w_qs] if rope else []) + [w_uk, w_uv]
    ins = [proj] + ws
    in_specs = [pl.BlockSpec((tm, pw), row)] + [_layer_spec(w, wl) for w in ws]
    out_shape = [jax.ShapeDtypeStruct((n, hw), MM_DTYPE)] * 4 + [jax.ShapeDtypeStruct((n, LANES), MM_DTYPE)]
    out_specs = [pl.BlockSpec((tm, hw), row)] * 4 + [pl.BlockSpec((tm, LANES), row)]
    if rope:
        bps = seq // tm
        ins += [cos, sin]
        in_specs += [pl.BlockSpec((tm, LANES), lambda i: (i % bps, 0))] * 2
    else:
        out_shape += [jax.ShapeDtypeStruct((n, cfg.kv_lora), F32), jax.ShapeDtypeStruct((n, cfg.rope), F32)]
        out_specs += [pl.BlockSpec((tm, cfg.kv_lora), row), pl.BlockSpec((tm, cfg.rope), row)]
    return pl.pallas_call(
        functools.partial(_mla_mid_body, cfg=cfg, rope=rope),
        out_shape=out_shape,
        grid=(n // tm,),
        in_specs=in_specs,
        out_specs=out_specs,
        compiler_params=_params(("arbitrary",)),
        name="mla_mid_rope" if rope else "mla_mid",
    )(*ins)


def _attn_body(*refs, hb, cached, q_chunk, rope):
    if cached:
        (qn_ref, qr_ref, kn_ref, kr_ref, v_ref, ckvc_ref, krope_ref, wuk_ref, wuv_ref,
         o_ref, krc_ref) = refs
        c = ckvc_ref[...].astype(MM_DTYPE)
        knc = _dot(c, wuk_ref[...]).astype(MM_DTYPE)
        vc = _dot(c, wuv_ref[...]).astype(MM_DTYPE)
        krc_ref[:, :rope] = krope_ref[...].astype(MM_DTYPE)
        krc_ref[:, rope:] = jnp.zeros((krc_ref.shape[0], LANES - rope), MM_DTYPE)
    else:
        qn_ref, qr_ref, kn_ref, kr_ref, v_ref, o_ref = refs
    sq = qn_ref.shape[0]
    qc = min(sq, q_chunk)
    for h in range(hb):
        sl = slice(h * LANES, (h + 1) * LANES)
        k = jnp.concatenate([kn_ref[:, sl], kr_ref[...]], axis=1)
        if cached:
            kc = jnp.concatenate([knc[:, sl], krc_ref[...]], axis=1)
        for r0 in range(0, sq, qc):
            rows = slice(r0, r0 + qc)
            q = jnp.concatenate([qn_ref[rows, sl], qr_ref[rows, sl]], axis=1)
            s = _dot_nt(q, k)
            m = jnp.max(s, axis=-1, keepdims=True)
            if cached:
                sc = _dot_nt(q, kc)
                m = jnp.maximum(m, jnp.max(sc, axis=-1, keepdims=True))
                pc = jnp.exp2(sc - m)
            p = jnp.exp2(s - m)
            den = jnp.sum(p, axis=-1, keepdims=True)
            o = _dot(p.astype(MM_DTYPE), v_ref[:, sl])
            if cached:
                den = den + jnp.sum(pc, axis=-1, keepdims=True)
                o = o + _dot(pc.astype(MM_DTYPE), vc[:, sl])
            o_ref[rows, sl] = (o / den).astype(MM_DTYPE)


def _attention(qn, qr, kn, krp, v, cache, *, cfg, seq, hb):
    n, hw = qn.shape
    cached = cache is not None
    blk = lambda rows: pl.BlockSpec((rows, hb * LANES), lambda b, g: (b, g))
    shared = lambda rows: pl.BlockSpec((rows, LANES), lambda b, g: (b, 0))
    ins = [qn, qr, kn, krp, v]
    in_specs = [blk(seq), blk(seq), blk(seq), shared(seq), blk(seq)]
    scratch = []
    if cached:
        cache_ckv, cache_krope, wl, w_uk, w_uv = cache
        past, kl = cache_ckv.shape[2:]
        ins += [cache_ckv, cache_krope, w_uk, w_uv]
        in_specs += [
            pl.BlockSpec((None, None, past, kl), lambda b, g: (b, wl, 0, 0)),
            pl.BlockSpec((None, None, past, cfg.rope), lambda b, g: (b, wl, 0, 0)),
            pl.BlockSpec((None, kl, hb * LANES), lambda b, g: (wl, 0, g)),
            pl.BlockSpec((None, kl, hb * LANES), lambda b, g: (wl, 0, g)),
        ]
        scratch = [pltpu.VMEM((past, LANES), MM_DTYPE)]
    return pl.pallas_call(
        functools.partial(_attn_body, hb=hb, cached=cached, q_chunk=ATTN_Q_CHUNK, rope=cfg.rope),
        out_shape=jax.ShapeDtypeStruct((n, hw), MM_DTYPE),
        grid=(n // seq, cfg.n_heads // hb),
        in_specs=in_specs,
        out_specs=blk(seq),
        scratch_shapes=scratch,
        compiler_params=_params(("arbitrary", "arbitrary")),
        name="attention_cached" if cached else "attention",
    )(*ins)


SCAN_PITCH_PAD = 4
SCAN_SCRATCH_BYTES = 32 * 1024 * 1024


def _scan_lane_tiles(seq, n_blocks):
    per_tile = 6 * SUBLANES * (seq + SCAN_PITCH_PAD) * LANES * 4
    return max(t for t in (1, 2, 4) if n_blocks % t == 0 and (t == 1 or t * per_tile <= SCAN_SCRATCH_BYTES))


def _rec_scan_body(*refs, seq, cfg, has_state, tiles):
    n_in = 7 if has_state else 5
    yg_ref, xc_ref, wg_ref, bg_ref, lam_ref = refs[:5]
    o_ref = refs[n_in]
    scratch = refs[-6 * tiles:]
    nseq = SUBLANES
    pitch = seq + SCAN_PITCH_PAD
    lam = lam_ref[...]
    softplus = jnp.maximum(-lam, 0.0) + jnp.log1p(jnp.exp(-jnp.abs(lam)))
    half_decay = (-0.5 * cfg.lru_c * math.log2(math.e)) * softplus
    for k in range(tiles):
        af_ref, uf_ref, ab_ref, ub_ref = scratch[6 * k:6 * k + 4]
        lanes = slice(k * LANES, (k + 1) * LANES)
        wg = wg_ref[k]
        bg = bg_ref[k]
        for b in range(nseq):
            xc = xc_ref[k, b * seq:(b + 1) * seq, :]
            th = jnp.tanh(_dot(xc.astype(MM_DTYPE), wg) + bg)
            xh = 0.5 * xc
            for d, (a_ref, u_ref) in enumerate(((af_ref, uf_ref), (ab_ref, ub_ref))):
                tx = th[:, (2 * d) * LANES:(2 * d + 1) * LANES]
                ta = th[:, (2 * d + 1) * LANES:(2 * d + 2) * LANES]
                hd = half_decay[d:d + 1, lanes]
                a = jnp.exp2(ta * hd + hd)
                z = 1.0 - a * a
                u = jnp.where(z > 0.0, z * lax.rsqrt(z), 0.0) * ((tx + 1.0) * xh)
                a_ref[b * pitch:b * pitch + seq, :] = a
                u_ref[b * pitch:b * pitch + seq, :] = u

    if has_state:
        h0 = [refs[5 + d][:, k * LANES:(k + 1) * LANES] for k in range(tiles) for d in (0, 1)]
    else:
        h0 = [jnp.zeros((nseq, LANES), F32)] * (2 * tiles)

    def step(t, carry):
        fwd = pl.ds(t, nseq, stride=pitch)
        bwd = pl.ds(seq - 1 - t, nseq, stride=pitch)
        new = []
        for k in range(tiles):
            af_ref, uf_ref, ab_ref, ub_ref, hf_ref, hb_ref = scratch[6 * k:6 * k + 6]
            hf = af_ref[fwd, :] * carry[2 * k] + uf_ref[fwd, :]
            hf_ref[fwd, :] = hf
            hb = ab_ref[bwd, :] * carry[2 * k + 1] + ub_ref[bwd, :]
            hb_ref[bwd, :] = hb
            new += [hf, hb]
        return tuple(new)

    last = lax.fori_loop(0, seq, step, tuple(h0), unroll=8)
    for k in range(tiles):
        lanes = slice(k * LANES, (k + 1) * LANES)
        hf_ref, hb_ref = scratch[6 * k + 4:6 * k + 6]
        if not has_state:
            refs[n_in + 1][:, lanes] = last[2 * k]
            refs[n_in + 2][:, lanes] = last[2 * k + 1]
        for b in range(nseq):
            hsum = hf_ref[b * pitch:b * pitch + seq, :] + hb_ref[b * pitch:b * pitch + seq, :]
            o_ref[b * seq:(b + 1) * seq, lanes] = (hsum * yg_ref[k, b * seq:(b + 1) * seq, :]).astype(MM_DTYPE)


def _rec_scan(proj, w_gates, b_gates, lam, wl, state, *, cfg, seq, tiles):
    nb2, n, _ = proj.shape
    nb = nb2 // 2
    w = nb * LANES
    assert nb == cfg.lru_blocks and nb % tiles == 0
    rows = SUBLANES * seq
    width = tiles * LANES
    has_state = state is not None
    col = lambda g, c: (g, c)
    ins = [proj, proj, w_gates, b_gates, lam]
    in_specs = [
        pl.BlockSpec((tiles, rows, LANES), lambda g, c: (c, g, 0)),
        pl.BlockSpec((tiles, rows, LANES), lambda g, c: (c + nb // tiles, g, 0)),
        pl.BlockSpec((None, tiles, LANES, 4 * LANES), lambda g, c: (wl, c, 0, 0)),
        pl.BlockSpec((None, tiles, 1, 4 * LANES), lambda g, c: (wl, c, 0, 0)),
        pl.BlockSpec((None, 2, width), lambda g, c: (wl, 0, c)),
    ]
    out_shape = [jax.ShapeDtypeStruct((n, w), MM_DTYPE)]
    out_specs = [pl.BlockSpec((rows, width), col)]
    if has_state:
        ins += list(state)
        in_specs += [pl.BlockSpec((SUBLANES, width), col)] * 2
    else:
        nstate = n // seq
        out_shape += [jax.ShapeDtypeStruct((nstate, w), F32)] * 2
        out_specs += [pl.BlockSpec((SUBLANES, width), col)] * 2
    scratch = [pltpu.VMEM((SUBLANES * (seq + SCAN_PITCH_PAD), LANES), F32)] * (6 * tiles)
    return pl.pallas_call(
        functools.partial(_rec_scan_body, seq=seq, cfg=cfg, has_state=has_state, tiles=tiles),
        out_shape=out_shape,
        grid=(n // rows, nb // tiles),
        in_specs=in_specs,
        out_specs=out_specs,
        scratch_shapes=scratch,
        compiler_params=_params(("arbitrary", "arbitrary")),
        name="rec_scan_state" if has_state else "rec_scan",
    )(*ins)


def _swap_halves(w, rope):
    q = rope // 4
    return jnp.concatenate([w[..., q:2 * q], w[..., :q], w[..., 3 * q:], w[..., 2 * q:3 * q]], axis=-1)


def _pad_lanes(w):
    return jnp.pad(w, [(0, 0)] * (w.ndim - 1) + [(0, LANES - w.shape[-1])])


def _rope_tables(seq, cfg):
    rows = seq // cfg.grid_w
    row = jnp.repeat(jnp.arange(rows), cfg.grid_w).astype(F32)
    col = jnp.tile(jnp.arange(cfg.grid_w), rows).astype(F32)
    half = cfg.rope // 2
    inv = 1.0 / (cfg.rope_base ** (jnp.arange(0, half, 2, dtype=F32) / half))
    ar, ac = row[:, None] * inv, col[:, None] * inv
    cos = jnp.concatenate([jnp.cos(ar), jnp.cos(ar), jnp.cos(ac), jnp.cos(ac)], axis=-1)
    sin = jnp.concatenate([-jnp.sin(ar), jnp.sin(ar), -jnp.sin(ac), jnp.sin(ac)], axis=-1)
    return _pad_lanes(cos), _pad_lanes(sin)


def _pick(n, pref):
    return pref if n % pref == 0 else n


def _forward(cfg, x_prompt, x_sample, cache_ckv, cache_krope, state_lru, c, c_ctx,
             g_mix, g_ffn, g_final, w_ada, b_ada,
             w_mla_in, g_mla_q, g_mla_kv, w_mla_uq, w_mla_uk, w_mla_uv, w_mla_o,
             w_rec_in, w_rec_conv, b_rec_conv, w_rec_gx, b_rec_gx, w_rec_ga, b_rec_ga,
             rec_lambda, w_rec_out,
             w_ffn_up, w_ffn_conv, b_ffn_conv, w_ffn_down):
    depth, d = g_mix.shape
    bp, sp, _ = x_prompt.shape
    bs, ss, _ = x_sample.shape
    assert cfg.nope == LANES and cfg.v_dim == LANES and cfg.rope <= LANES and bs + 1 <= MOD_ROWS
    assert bp % SUBLANES == 0 and bs % SUBLANES == 0
    eps = cfg.eps
    cast = lambda w: w.astype(MM_DTYPE)
    row3 = lambda v: v[:, None, :]

    cond = jnp.zeros((MOD_ROWS, d), F32).at[0].set(c_ctx).at[1:1 + bs].set(c)
    mod = _adaln(cond, w_ada, b_ada, _pick(w_ada.shape[-1], ADALN_TILE)).reshape(depth, MOD_ROWS, N_MOD, d)

    ql, kl, r, nh = cfg.q_lora, cfg.kv_lora, cfg.rope, cfg.n_heads
    w_kr = w_mla_in[..., ql + kl:]
    w_in_p = cast(jnp.concatenate(
        [w_mla_in[..., :ql + kl], _pad_lanes(w_kr), _pad_lanes(_swap_halves(w_kr, r))], axis=-1))
    uq = w_mla_uq.reshape(w_mla_uq.shape[0], ql, nh, cfg.nope + r)
    flat = lambda w: cast(w.reshape(w.shape[0], ql, nh * LANES))
    w_qn, w_qr, w_qs = flat(uq[..., :cfg.nope]), flat(_pad_lanes(uq[..., cfg.nope:])), flat(
        _pad_lanes(_swap_halves(uq[..., cfg.nope:], r)))
    w_uk, w_uv, w_o = cast(w_mla_uk), cast(w_mla_uv), cast(w_mla_o)
    g_q, g_kv = row3(g_mla_q), row3(g_mla_kv)
    cos, sin = _rope_tables(ss, cfg)

    w_gates = cast(0.5 * jnp.concatenate([w_rec_gx[:, 0], w_rec_ga[:, 0], w_rec_gx[:, 1], w_rec_ga[:, 1]], axis=-1))
    nb = cfg.lru_blocks
    blk = lambda b: b.reshape(b.shape[0], nb, 1, LANES)
    b_gates = 0.5 * jnp.concatenate(
        [blk(b_rec_gx[:, 0]), blk(b_rec_ga[:, 0]), blk(b_rec_gx[:, 1]), blk(b_rec_ga[:, 1])], axis=-1)
    tf = _pick(w_ffn_down.shape[1], FF_TILE)
    w_rin, w_rout = cast(w_rec_in), cast(w_rec_out)
    w_up, w_down = cast(w_ffn_up), cast(w_ffn_down)
    g_mix3, g_ffn3, b_rconv, b_fconv = row3(g_mix), row3(g_ffn), row3(b_rec_conv), row3(b_ffn_conv)

    ctx_row = lambda i, tm: 0

    def run_group(x, seq, mod_row, sample):
        n = x.shape[0]
        rows = lambda pref: min(seq, pref) if sample else _pick(n, pref)
        tm_seq = max(seq, rows(TOKEN_BLOCK))
        ckv_new, kr_new, lru_new = [], [], []
        for layer in range(depth):
            j = layer // 2
            if layer % 2 == 0:
                proj = _norm_mod_matmul(x, mod, mod_row, g_mix3, w_in_p, layer, j, tm=rows(TOKEN_BLOCK),
                                        tn=w_in_p.shape[-1], eps=eps)
                if sample:
                    qn, qr, kn, v, krp = _mla_mid(proj, g_q, g_kv, w_qn, w_qr, w_qs, w_uk, w_uv, j, cos, sin,
                                                  cfg=cfg, seq=seq, tm=rows(MLA_MID_BLOCK))
                    cache = (cache_ckv, cache_krope, j, w_uk, w_uv)
                    att = _attention(qn, qr, kn, krp, v, cache, cfg=cfg, seq=seq, hb=min(ATTN_CACHED_HEADS, nh))
                else:
                    qn, qr, kn, v, krp, ckv, kr = _mla_mid(proj, g_q, g_kv, w_qn, w_qr, None, w_uk, w_uv, j,
                                                           None, None, cfg=cfg, seq=seq, tm=rows(MLA_MID_BLOCK))
                    ckv_new.append(ckv)
                    kr_new.append(kr)
                    att = _attention(qn, qr, kn, krp, v, None, cfg=cfg, seq=seq, hb=nh)
                x = _matmul_residual(att, w_o, j, x, mod, layer, mod_row, tm=rows(RESIDUAL_BLOCK))
            else:
                proj = _rec_in(x, mod, mod_row, g_mix3, w_rin, w_rec_conv, b_rconv, layer, j, seq=seq,
                               tm=tm_seq, tn=_pick(w_rin.shape[-1] // 2, IN_PROJ_TILE), eps=eps)
                outs = _rec_scan(proj, w_gates, b_gates, rec_lambda, j,
                                 (state_lru[:, j, 0], state_lru[:, j, 1]) if sample else None, cfg=cfg, seq=seq,
                                 tiles=_scan_lane_tiles(seq, nb))
                if not sample:
                    lru_new.append(jnp.stack(outs[1:], axis=1))
                x = _matmul_residual(outs[0], w_rout, j, x, mod, layer, mod_row, tm=rows(RESIDUAL_BLOCK))
            x = _conv_ffn(x, mod, layer, mod_row, g_ffn3, w_up, w_ffn_conv, b_fconv, w_down,
                          g_final[None] if layer == depth - 1 else None, seq=seq, tm=tm_seq, tf=tf,
                          n_sub=2 if tf % (2 * LANES) == 0 else 1, x_buffers=2, eps=eps)
        return x, ckv_new, kr_new, lru_new

    y_p, ckv_new, kr_new, lru_new = run_group(x_prompt.reshape(bp * sp, d), sp, ctx_row, False)
    y_s, _, _, _ = run_group(x_sample.reshape(bs * ss, d), ss, lambda i, tm: 1 + (i * tm) // ss, True)
    new_ckv = jnp.stack([t.reshape(bp, sp, kl) for t in ckv_new], axis=1)
    new_kr = jnp.stack([t.reshape(bp, sp, r) for t in kr_new], axis=1)
    new_lru = jnp.stack(lru_new, axis=1)
    return y_p.reshape(bp, sp, d), y_s.reshape(bs, ss, d), new_ckv, new_kr, new_lru


CFG = Cfg(n_heads=16, q_lora=512, kv_lora=512, nope=128, rope=64, v_dim=128, grid_w=64, rope_base=10000.0,
          lru_blocks=16, lru_c=8.0, eps=1e-6)


def kernel(x_prompt, x_sample, cache_ckv, cache_krope, state_lru, c, c_ctx, g_mix, g_ffn, g_final, w_ada, b_ada, w_mla_in, g_mla_q, g_mla_kv, w_mla_uq, w_mla_uk, w_mla_uv, w_mla_o, w_rec_in, w_rec_conv, b_rec_conv, w_rec_gx, b_rec_gx, w_rec_ga, b_rec_ga, rec_lambda, w_rec_out, w_ffn_up, w_ffn_conv, b_ffn_conv, w_ffn_down):
    return _forward(CFG, x_prompt, x_sample, cache_ckv, cache_krope, state_lru, c, c_ctx, g_mix, g_ffn, g_final,
                    w_ada, b_ada, w_mla_in, g_mla_q, g_mla_kv, w_mla_uq, w_mla_uk, w_mla_uv, w_mla_o,
                    w_rec_in, w_rec_conv, b_rec_conv, w_rec_gx, b_rec_gx, w_rec_ga, b_rec_ga, rec_lambda, w_rec_out,
                    w_ffn_up, w_ffn_conv, b_ffn_conv, w_ffn_down)
```
